```python
import math
import jax, jax.numpy as jnp
from jax import lax
import numpy as np

D_MODEL = 1024
BATCH = 8
SEQ = 4096
DEPTH = 1

CHUNK = 64
MIX_WIDTH = D_MODEL
ATTN_WIDTH = MIX_WIDTH // 2
SSM_WIDTH = MIX_WIDTH - ATTN_WIDTH
HEAD_DIM = 64
N_HEADS = ATTN_WIDTH // HEAD_DIM
Q_BLOCK = 128
SSM_GROUP = 16
N_SSM_GROUPS = SSM_WIDTH // SSM_GROUP
SSM_STATE = 64
DT_MIN = 1e-3
DT_MAX = 1e-1
IN_WIDTH = 3 * ATTN_WIDTH + N_HEADS + SSM_WIDTH
N_EXPERTS = 32
TOP_K = 4
D_FF = D_MODEL
SWIGLU_LIMIT = 7.0
SWIGLU_ALPHA = 1.702
MOE_BLOCK = 128
PLE_DIM = 256
NORM_EPS = 1e-5

kernel_name = "hymba_s5_fox_moe_ple_encoder"


def rmsnorm(x, g):
    xf = x.astype(jnp.float32)
    y = xf * lax.rsqrt(jnp.mean(xf * xf, axis=-1, keepdims=True) + NORM_EPS)
    return (y * g.astype(jnp.float32)).astype(x.dtype)


def forgetting_attention(q, k, v, log_f):
    b, s, h, dh = q.shape
    n_blk = s // Q_BLOCK
    c_t = jnp.cumsum(log_f, axis=1).transpose(0, 2, 1)
    c_blocks = c_t.reshape(b, h, n_blk, Q_BLOCK).transpose(2, 0, 1, 3)
    q_blocks = q.reshape(b, n_blk, Q_BLOCK, h, dh).transpose(1, 0, 2, 3, 4)
    key_pos = jnp.arange(s)
    scale = dh ** -0.5

    def one_block(args):
        i, q_i, c_i = args
        q_pos = i * Q_BLOCK + jnp.arange(Q_BLOCK)
        logits = jnp.einsum('bqhd,bkhd->bhqk', q_i, k,
                            preferred_element_type=jnp.float32) * scale
        logits = logits + c_i[..., :, None] - c_t[..., None, :]
        logits = jnp.where(key_pos[None, :] <= q_pos[:, None], logits, -jnp.inf)
        w = jax.nn.softmax(logits, axis=-1)
        return jnp.einsum('bhqk,bkhd->bqhd', w.astype(v.dtype), v)

    out = lax.map(one_block, (jnp.arange(n_blk), q_blocks, c_blocks))
    return out.transpose(1, 0, 2, 3, 4).reshape(b, s, h, dh)


def s5_mixer(u, lam_re, lam_im, log_dt, b_re, b_im, c_re, c_im, d_skip, w_glu, b_glu):
    bsz, s, _ = u.shape
    f32 = jnp.float32
    uf = u.astype(f32).reshape(bsz, s, N_SSM_GROUPS, SSM_GROUP)
    dt = jnp.exp(log_dt.astype(f32))[:, None]
    lr = lam_re.astype(f32)
    li = lam_im.astype(f32)
    mag = jnp.exp(lr * dt)
    ab_re = mag * jnp.cos(li * dt)
    ab_im = mag * jnp.sin(li * dt)
    den = lr * lr + li * li
    nr = ab_re - 1.0
    z_re = (nr * lr + ab_im * li) / den
    z_im = (ab_im * lr - nr * li) / den
    br = b_re.astype(f32)
    bi = b_im.astype(f32)
    bb_re = z_re[..., None] * br - z_im[..., None] * bi
    bb_im = z_re[..., None] * bi + z_im[..., None] * br
    bu_re = jnp.einsum('bsgc,gpc->bsgp', uf, bb_re)
    bu_im = jnp.einsum('bsgc,gpc->bsgp', uf, bb_im)
    a_re = jnp.broadcast_to(ab_re, (1, s) + ab_re.shape)
    a_im = jnp.broadcast_to(ab_im, (1, s) + ab_im.shape)

    def combine(e1, e2):
        a1r, a1i, b1r, b1i = e1
        a2r, a2i, b2r, b2i = e2
        return (a2r * a1r - a2i * a1i,
                a2r * a1i + a2i * a1r,
                a2r * b1r - a2i * b1i + b2r,
                a2r * b1i + a2i * b1r + b2i)

    _, _, x_re, x_im = lax.associative_scan(combine, (a_re, a_im, bu_re, bu_im), axis=1)
    y = (jnp.einsum('bsgp,gcp->bsgc', x_re, c_re.astype(f32))
         - jnp.einsum('bsgp,gcp->bsgc', x_im, c_im.astype(f32)))
    y = y.reshape(bsz, s, SSM_WIDTH) + d_skip.astype(f32) * u.astype(f32)
    g = jax.nn.gelu(y)
    out = g * jax.nn.sigmoid(g @ w_glu.astype(f32) + b_glu.astype(f32))
    return out.astype(u.dtype)


def clamped_swiglu(gu):
    gate = jnp.minimum(gu[:, :D_FF], SWIGLU_LIMIT)
    up = jnp.clip(gu[:, D_FF:], -SWIGLU_LIMIT, SWIGLU_LIMIT)
    return gate * jax.nn.sigmoid(SWIGLU_ALPHA * gate) * (up + 1.0)


def moe_ffn(h, w_router, b_router, w_gu, b_gu, w_dn, b_dn):
    t, d = h.shape
    logits = (h @ w_router + b_router).astype(jnp.float32)
    top_val, top_idx = lax.top_k(logits, TOP_K)
    gates = jax.nn.softmax(top_val, axis=-1)
    n_assign = t * TOP_K
    flat_e = top_idx.reshape(-1)
    flat_tok = jnp.broadcast_to(jnp.arange(t, dtype=jnp.int32)[:, None], (t, TOP_K)).reshape(-1)
    flat_g = gates.reshape(-1)
    order = jnp.argsort(flat_e)
    se = flat_e[order]
    stok = flat_tok[order]
    sg = flat_g[order]
    counts = jnp.bincount(flat_e, length=N_EXPERTS)
    starts = jnp.cumsum(counts) - counts
    padded = (counts + MOE_BLOCK - 1) // MOE_BLOCK * MOE_BLOCK
    pad_end = jnp.cumsum(padded)
    pad_start = pad_end - padded
    dest = pad_start[se] + jnp.arange(n_assign) - starts[se]
    cap = n_assign + N_EXPERTS * MOE_BLOCK
    n_blocks = cap // MOE_BLOCK
    row_tok = jnp.full((cap,), t, dtype=jnp.int32).at[dest].set(stok)
    row_gate = jnp.zeros((cap,), jnp.float32).at[dest].set(sg)
    blk_expert = jnp.minimum(
        jnp.searchsorted(pad_end, jnp.arange(n_blocks) * MOE_BLOCK, side='right'),
        N_EXPERTS - 1)
    h_pad = jnp.concatenate([h, jnp.zeros((1, d), h.dtype)], axis=0)

    def expert_block(args):
        e, toks = args
        xb = h_pad[toks]
        act = clamped_swiglu(xb @ w_gu[e] + b_gu[e])
        return act @ w_dn[e] + b_dn[e]

    y = lax.map(expert_block, (blk_expert, row_tok.reshape(n_blocks, MOE_BLOCK)))
    y = y.reshape(cap, d) * row_gate[:, None].astype(y.dtype)
    return jnp.zeros((t + 1, d), y.dtype).at[row_tok].add(y)[:t]


def setup_inputs(seed: int = 0) -> dict:
    key = jax.random.key(seed)
    ks = jax.random.split(key, 32)
    f32 = jnp.float32
    nrm = lambda k, shape, scale: jax.random.normal(k, shape, f32) * scale
    G, P, C = N_SSM_GROUPS, SSM_STATE, SSM_GROUP
    x = nrm(ks[0], (BATCH, SEQ, D_MODEL), 1.0)
    p = nrm(ks[1], (DEPTH, BATCH, SEQ, PLE_DIM), 1.0)
    norm_mix_g = 1.0 + nrm(ks[2], (DEPTH, D_MODEL), 0.02)
    w_in = nrm(ks[3], (DEPTH, D_MODEL, IN_WIDTH), D_MODEL ** -0.5)
    b_f = jnp.linspace(1.0, 6.0, N_HEADS, dtype=f32)[None, :] + nrm(ks[4], (DEPTH, N_HEADS), 0.1)
    lam_re = -0.5 + nrm(ks[5], (DEPTH, G, P), 0.01)
    lam_im = jnp.broadcast_to(math.pi * jnp.arange(P, dtype=f32), (DEPTH, G, P)) + nrm(ks[6], (DEPTH, G, P), 0.01)
    log_dt = jax.random.uniform(ks[7], (DEPTH, G), f32, math.log(DT_MIN), math.log(DT_MAX))
    b_re = nrm(ks[8], (DEPTH, G, P, C), (2 * C) ** -0.5)
    b_im = nrm(ks[9], (DEPTH, G, P, C), (2 * C) ** -0.5)
    c_re = nrm(ks[10], (DEPTH, G, C, P), (2 * P) ** -0.5)
    c_im = nrm(ks[11], (DEPTH, G, C, P), (2 * P) ** -0.5)
    d_skip = nrm(ks[12], (DEPTH, SSM_WIDTH), 1.0)
    w_glu = nrm(ks[13], (DEPTH, SSM_WIDTH, SSM_WIDTH), SSM_WIDTH ** -0.5)
    b_glu = nrm(ks[14], (DEPTH, SSM_WIDTH), 0.01)
    attn_out_g = 1.0 + nrm(ks[15], (DEPTH, ATTN_WIDTH), 0.02)
    ssm_out_g = 1.0 + nrm(ks[16], (DEPTH, SSM_WIDTH), 0.02)
    w_out = nrm(ks[17], (DEPTH, MIX_WIDTH, D_MODEL), MIX_WIDTH ** -0.5)
    norm_ffn_g = 1.0 + nrm(ks[18], (DEPTH, D_MODEL), 0.02)
    w_router = nrm(ks[19], (DEPTH, D_MODEL, N_EXPERTS), D_MODEL ** -0.5)
    b_router = nrm(ks[20], (DEPTH, N_EXPERTS), 0.01)
    w_gu = nrm(ks[21], (DEPTH, N_EXPERTS, D_MODEL, 2 * D_FF), D_MODEL ** -0.5)
    b_gu = nrm(ks[22], (DEPTH, N_EXPERTS, 2 * D_FF), 0.01)
    w_dn = nrm(ks[23], (DEPTH, N_EXPERTS, D_FF, D_MODEL), D_FF ** -0.5)
    b_dn = nrm(ks[24], (DEPTH, N_EXPERTS, D_MODEL), 0.01)
    norm_ple_g = 1.0 + nrm(ks[25], (DEPTH, D_MODEL), 0.02)
    w_ple_gate = nrm(ks[26], (DEPTH, D_MODEL, D_MODEL), D_MODEL ** -0.5)
    w_ple_proj = nrm(ks[27], (DEPTH, PLE_DIM, D_MODEL), PLE_DIM ** -0.5)
    norm_final_g = 1.0 + nrm(ks[28], (D_MODEL,), 0.02)
    return {"x": x, "p": p, "norm_mix_g": norm_mix_g, "w_in": w_in, "b_f": b_f,
            "lam_re": lam_re, "lam_im": lam_im, "log_dt": log_dt,
            "b_re": b_re, "b_im": b_im, "c_re": c_re, "c_im": c_im,
            "d_skip": d_skip, "w_glu": w_glu, "b_glu": b_glu,
            "attn_out_g": attn_out_g, "ssm_out_g": ssm_out_g, "w_out": w_out,
            "norm_ffn_g": norm_ffn_g, "w_router": w_router, "b_router": b_router,
            "w_gu": w_gu, "b_gu": b_gu, "w_dn": w_dn, "b_dn": b_dn,
            "norm_ple_g": norm_ple_g, "w_ple_gate": w_ple_gate, "w_ple_proj": w_ple_proj,
            "norm_final_g": norm_final_g}


def reference(x, p, norm_mix_g, w_in, b_f, lam_re, lam_im, log_dt, b_re, b_im, c_re, c_im,
              d_skip, w_glu, b_glu, attn_out_g, ssm_out_g, w_out, norm_ffn_g,
              w_router, b_router, w_gu, b_gu, w_dn, b_dn,
              norm_ple_g, w_ple_gate, w_ple_proj, norm_final_g):
    bsz, s, d = x.shape
    split_at = [ATTN_WIDTH, 2 * ATTN_WIDTH, 3 * ATTN_WIDTH, 3 * ATTN_WIDTH + N_HEADS]
    for i in range(DEPTH):
        h = rmsnorm(x, norm_mix_g[i])
        proj = h @ w_in[i]
        q, k, v, f_logit, u = jnp.split(proj, split_at, axis=-1)
        q = q.reshape(bsz, s, N_HEADS, HEAD_DIM)
        k = k.reshape(bsz, s, N_HEADS, HEAD_DIM)
        v = v.reshape(bsz, s, N_HEADS, HEAD_DIM)
        log_f = jax.nn.log_sigmoid(f_logit.astype(jnp.float32) + b_f[i].astype(jnp.float32))
        attn = forgetting_attention(q, k, v, log_f).reshape(bsz, s, ATTN_WIDTH)
        ssm = s5_mixer(u, lam_re[i], lam_im[i], log_dt[i], b_re[i], b_im[i], c_re[i], c_im[i],
                       d_skip[i], w_glu[i], b_glu[i])
        mixed = jnp.concatenate([rmsnorm(attn, attn_out_g[i]), rmsnorm(ssm, ssm_out_g[i])], axis=-1)
        x = x + mixed @ w_out[i]
        h = rmsnorm(x, norm_ffn_g[i]).reshape(bsz * s, d)
        x = x + moe_ffn(h, w_router[i], b_router[i], w_gu[i], b_gu[i], w_dn[i], b_dn[i]).reshape(bsz, s, d)
        gate = jax.nn.sigmoid(rmsnorm(x, norm_ple_g[i]) @ w_ple_gate[i])
        x = x + gate * (p[i] @ w_ple_proj[i])
    return rmsnorm(x, norm_final_g)
```

```python
import functools
import math

import jax
import jax.numpy as jnp
from jax import lax
from jax.experimental import pallas as pl
from jax.experimental.pallas import tpu as pltpu

F32 = jnp.float32
BF16 = jnp.bfloat16
I32 = jnp.int32

NORM_EPS = 1e-5
HEAD_DIM = 64
N_HEADS = 8
ATTN_WIDTH = 512
SSM_WIDTH = 512
SSM_GROUP = 16
N_SSM_GROUPS = 32
SSM_STATE = 64
N_STATE = N_SSM_GROUPS * SSM_STATE
N_EXPERTS = 32
TOP_K = 4
SWIGLU_LIMIT = 7.0
SWIGLU_ALPHA = 1.702
LANES = 128
SUBLANES = 8
ROW_CHUNKS = 8
MOE_ROWS = 256
NEG_BIG = -1e30
VMEM_LIMIT = 56 * 1024 * 1024

_NT = (((1,), (1,)), ((), ()))


def _rms(xf, g):
    ms = jnp.mean(xf * xf, axis=-1, keepdims=True)
    return xf * lax.rsqrt(ms + NORM_EPS) * g


def _sigmoid(x):
    return 1.0 / (1.0 + jnp.exp(-x))


def _ssm_prep_kernel(lr_ref, li_ref, ldt_ref, brt_ref, bit_ref,
                     ar_ref, ai_ref, bbr_ref, bbi_ref):
    lr = lr_ref[...]
    li = li_ref[...]
    dt = jnp.exp(ldt_ref[...])
    mag = jnp.exp(lr * dt)
    ab_re = mag * jnp.cos(li * dt)
    ab_im = mag * jnp.sin(li * dt)
    den = lr * lr + li * li
    nr = ab_re - 1.0
    z_re = (nr * lr + ab_im * li) / den
    z_im = (ab_im * lr - nr * li) / den
    ar_ref[...] = ab_re
    ai_ref[...] = ab_im
    br = brt_ref[...]
    bi = bit_ref[...]
    bbr_ref[...] = z_re * br - z_im * bi
    bbi_ref[...] = z_re * bi + z_im * br


def _ssm_prep(lam_re, lam_im, log_dt, b_re, b_im):
    g, p, c = b_re.shape
    brt = jnp.transpose(b_re, (0, 2, 1))
    bit = jnp.transpose(b_im, (0, 2, 1))
    return pl.pallas_call(
        _ssm_prep_kernel,
        out_shape=(jax.ShapeDtypeStruct((g, 1, p), F32), jax.ShapeDtypeStruct((g, 1, p), F32),
                   jax.ShapeDtypeStruct((g, c, p), F32), jax.ShapeDtypeStruct((g, c, p), F32)),
        name="ssm_prep",
    )(lam_re.reshape(g, 1, p), lam_im.reshape(g, 1, p), log_dt.reshape(g, 1, 1), brt, bit)


def _block_diag(w, rows_first):
    g, a, b = w.shape
    half = g // 2
    eye = jnp.eye(half, dtype=w.dtype)
    w4 = w.reshape(2, half, a, b)
    out = jnp.einsum('hgab,gk->hgakb', w4, eye)
    del rows_first
    return out.reshape(2, half * a, half * b)


def _inproj_kernel(x_ref, g_ref, w_ref, wft_ref, bf_ref, tri_ref,
                   q_ref, k_ref, v_ref, u_ref, c_ref, carry_ref, *, tiles_per_seq, tm):
    i = pl.program_id(0)

    @pl.when(i % tiles_per_seq == 0)
    def _():
        carry_ref[...] = jnp.zeros_like(carry_ref)

    h = _rms(x_ref[...], g_ref[...]).astype(BF16)
    proj = jnp.dot(h, w_ref[...], preferred_element_type=F32)
    aw = ATTN_WIDTH
    q_ref[...] = (proj[:, 0:aw] * (HEAD_DIM ** -0.5)).astype(BF16)
    k_ref[...] = proj[:, aw:2 * aw].astype(BF16)
    v_ref[...] = proj[:, 2 * aw:3 * aw].astype(BF16)
    u_ref[...] = proj[:, 3 * aw:3 * aw + SSM_WIDTH].astype(BF16)

    fl = lax.dot_general(wft_ref[...], h, _NT, preferred_element_type=F32)
    z = fl + bf_ref[...]
    lf = jnp.minimum(z, 0.0) - jnp.log1p(jnp.exp(-jnp.abs(z)))
    hi = lf.astype(BF16)
    lo = (lf - hi.astype(F32)).astype(BF16)
    tri = tri_ref[...]
    cs = (jnp.dot(hi, tri, preferred_element_type=F32)
          + jnp.dot(lo, tri, preferred_element_type=F32))
    c = cs + carry_ref[:, 0:1]
    c_ref[...] = c
    carry_ref[...] = jnp.broadcast_to(c[:, tm - 1:tm], carry_ref.shape)


def _in_proj(x2, norm_g, w_main, wft, b_f, *, seq, tm=512):
    t, d = x2.shape
    n_main = w_main.shape[1]
    tri = jnp.triu(jnp.ones((tm, tm), F32)).astype(BF16)
    kern = functools.partial(_inproj_kernel, tiles_per_seq=seq // tm, tm=tm)
    row = lambda i: (i, 0)
    const = lambda i: (0, 0)
    act = jax.ShapeDtypeStruct((t, ATTN_WIDTH), BF16)
    return pl.pallas_call(
        kern,
        grid=(t // tm,),
        in_specs=[pl.BlockSpec((tm, d), row),
                  pl.BlockSpec((1, d), const),
                  pl.BlockSpec((d, n_main), const),
                  pl.BlockSpec((N_HEADS, d), const),
                  pl.BlockSpec((N_HEADS, 1), const),
                  pl.BlockSpec((tm, tm), const)],
        out_specs=[pl.BlockSpec((tm, ATTN_WIDTH), row)] * 4
        + [pl.BlockSpec((N_HEADS, tm), lambda i: (0, i))],
        out_shape=[act, act, act, act, jax.ShapeDtypeStruct((N_HEADS, t), F32)],
        scratch_shapes=[pltpu.VMEM((N_HEADS, LANES), F32)],
        compiler_params=pltpu.CompilerParams(
            dimension_semantics=("arbitrary",), vmem_limit_bytes=VMEM_LIMIT),
        name="in_proj",
    )(x2, norm_g.reshape(1, d), w_main, wft, b_f.reshape(N_HEADS, 1), tri)


def _attn_kernel(q_ref, k_ref, v_ref, c_ref, o_ref, *, tq):
    i = pl.program_id(2)
    q2 = q_ref[...]
    lane = lax.broadcasted_iota(I32, (1, LANES), 1)
    first = lane < HEAD_DIM
    zero = jnp.zeros_like(q2)
    q_heads = (jnp.where(first, q2, zero), jnp.where(first, zero, q2))

    def update(s, m, l, acc, vj):
        m_new = jnp.maximum(m, jnp.max(s, axis=-1, keepdims=True))
        alpha = jnp.exp(m - m_new)
        p = jnp.exp(s - m_new)
        l_new = alpha * l + jnp.sum(p, axis=-1, keepdims=True)
        acc_new = alpha * acc + jnp.dot(p.astype(BF16), vj, preferred_element_type=F32)
        return m_new, l_new, acc_new

    def block(j, carry, masked):
        off = pl.multiple_of(j * tq, tq)
        kj = k_ref[pl.ds(off, tq), :]
        vj = v_ref[pl.ds(off, tq), :]
        cj = c_ref[:, pl.ds(off, tq)]
        out = []
        for h in range(2):
            m, l, acc = carry[h]
            s = lax.dot_general(q_heads[h], kj, _NT, preferred_element_type=F32)
            s = s - cj[h:h + 1, :]
            if masked:
                r = lax.broadcasted_iota(I32, (tq, tq), 0)
                cc = lax.broadcasted_iota(I32, (tq, tq), 1)
                s = jnp.where(cc <= r, s, NEG_BIG)
            out.append(update(s, m, l, acc, vj))
        return tuple(out)

    init_one = (jnp.full((tq, 1), NEG_BIG, F32), jnp.zeros((tq, 1), F32),
                jnp.zeros((tq, LANES), F32))
    carry = lax.fori_loop(0, i, lambda j, c: block(j, c, False), (init_one, init_one))
    (_, l_a, acc_a), (_, l_b, acc_b) = block(i, carry, True)
    o = jnp.where(first, acc_a / l_a, acc_b / l_b)
    o_ref[...] = o.astype(BF16)


def _attention(q, k, v, c3, *, tq=512):
    b, s, w = q.shape
    n_pairs = w // LANES
    kern = functools.partial(_attn_kernel, tq=tq)
    return pl.pallas_call(
        kern,
        grid=(b, n_pairs, s // tq),
        in_specs=[pl.BlockSpec((None, tq, LANES), lambda bi, hp, i: (bi, i, hp)),
                  pl.BlockSpec((None, s, LANES), lambda bi, hp, i: (bi, 0, hp)),
                  pl.BlockSpec((None, s, LANES), lambda bi, hp, i: (bi, 0, hp)),
                  pl.BlockSpec((None, 2, s), lambda bi, hp, i: (hp, 0, bi))],
        out_specs=pl.BlockSpec((None, tq, LANES), lambda bi, hp, i: (bi, i, hp)),
        out_shape=jax.ShapeDtypeStruct((b, s, w), BF16),
        compiler_params=pltpu.CompilerParams(
            dimension_semantics=("arbitrary", "arbitrary", "arbitrary"),
            vmem_limit_bytes=VMEM_LIMIT),
        name="attention",
    )(q, k, v, c3)


def _ssm_kernel(u_ref, bre_ref, bim_ref, cre_ref, cim_ref, ar_ref, ai_ref, dsk_ref,
                wglu_ref, bglu_ref, g_ref, o_ref,
                us_ref, xr_ref, xi_ref, str_ref, sti_ref, res_ref, *, tt, nb):
    i = pl.program_id(0)

    @pl.when(i == 0)
    def _():
        str_ref[...] = jnp.zeros_like(str_ref)
        sti_ref[...] = jnp.zeros_like(sti_ref)

    n_chunks = SSM_WIDTH // LANES
    for b in range(nb):
        ub32 = u_ref[b].astype(F32)
        for c in range(n_chunks):
            us_ref[c, pl.ds(b, tt, stride=nb), :] = ub32[:, c * LANES:(c + 1) * LANES]
    uf = jnp.concatenate([us_ref[c] for c in range(n_chunks)], axis=-1)
    ub = uf.astype(BF16)
    half_in = SSM_WIDTH // 2
    half_st = N_STATE // 2
    for hf in range(2):
        uh = ub[:, hf * half_in:(hf + 1) * half_in]
        xr_ref[:, hf * half_st:(hf + 1) * half_st] = jnp.dot(
            uh, bre_ref[hf], preferred_element_type=F32)
        xi_ref[:, hf * half_st:(hf + 1) * half_st] = jnp.dot(
            uh, bim_ref[hf], preferred_element_type=F32)

    n_col_groups = 2
    wcol = N_STATE // n_col_groups
    unroll = 4
    for cg in range(n_col_groups):
        cols = slice(cg * wcol, (cg + 1) * wcol)
        ar = jnp.broadcast_to(ar_ref[:, cols], (nb, wcol))
        ai = jnp.broadcast_to(ai_ref[:, cols], (nb, wcol))

        def steps(tb, carry, cols=cols, ar=ar, ai=ai):
            sr, si = carry
            for k in range(unroll):
                r0 = pl.multiple_of((tb * unroll + k) * nb, nb)
                br = xr_ref[pl.ds(r0, nb), cols]
                bi = xi_ref[pl.ds(r0, nb), cols]
                nr = ar * sr - ai * si + br
                ni = ar * si + ai * sr + bi
                xr_ref[pl.ds(r0, nb), cols] = nr
                xi_ref[pl.ds(r0, nb), cols] = ni
                sr, si = nr, ni
            return sr, si

        sr, si = lax.fori_loop(0, tt // unroll, steps, (str_ref[:, cols], sti_ref[:, cols]))
        str_ref[:, cols] = sr
        sti_ref[:, cols] = si

    ys = []
    for hf in range(2):
        xr = xr_ref[:, hf * half_st:(hf + 1) * half_st].astype(BF16)
        xi = xi_ref[:, hf * half_st:(hf + 1) * half_st].astype(BF16)
        ys.append(jnp.dot(xr, cre_ref[hf], preferred_element_type=F32)
                  - jnp.dot(xi, cim_ref[hf], preferred_element_type=F32))
    y = jnp.concatenate(ys, axis=-1) + dsk_ref[...] * uf
    gl = 0.5 * y * (1.0 + jnp.tanh(math.sqrt(2.0 / math.pi) * (y + 0.044715 * (y * y * y))))
    zz = jnp.dot(gl.astype(BF16), wglu_ref[...], preferred_element_type=F32) + bglu_ref[...]
    out = gl * _sigmoid(zz)
    res = _rms(out, g_ref[...])
    for c in range(n_chunks):
        res_ref[c] = res[:, c * LANES:(c + 1) * LANES]
    for b in range(nb):
        o_ref[b] = jnp.concatenate(
            [res_ref[c, pl.ds(b, tt, stride=nb), :] for c in range(n_chunks)],
            axis=-1).astype(BF16)


def _ssm(u3, bre, bim, cre, cim, a_re, a_im, d_skip, w_glu, b_glu, out_g, *, tt=64):
    nb, s, w = u3.shape
    rows = tt * nb
    kern = functools.partial(_ssm_kernel, tt=tt, nb=nb)
    c3 = lambda i: (0, 0, 0)
    c2 = lambda i: (0, 0)
    return pl.pallas_call(
        kern,
        grid=(s // tt,),
        in_specs=[pl.BlockSpec((nb, tt, w), lambda i: (0, i, 0)),
                  pl.BlockSpec(bre.shape, c3), pl.BlockSpec(bim.shape, c3),
                  pl.BlockSpec(cre.shape, c3), pl.BlockSpec(cim.shape, c3),
                  pl.BlockSpec((1, N_STATE), c2), pl.BlockSpec((1, N_STATE), c2),
                  pl.BlockSpec((1, w), c2),
                  pl.BlockSpec((w, w), c2), pl.BlockSpec((1, w), c2), pl.BlockSpec((1, w), c2)],
        out_specs=pl.BlockSpec((nb, tt, w), lambda i: (0, i, 0)),
        out_shape=jax.ShapeDtypeStruct((nb, s, w), BF16),
        scratch_shapes=[pltpu.VMEM((w // LANES, rows, LANES), F32),
                        pltpu.VMEM((rows, N_STATE), F32), pltpu.VMEM((rows, N_STATE), F32),
                        pltpu.VMEM((nb, N_STATE), F32), pltpu.VMEM((nb, N_STATE), F32),
                        pltpu.VMEM((w // LANES, rows, LANES), F32)],
        compiler_params=pltpu.CompilerParams(
            dimension_semantics=("arbitrary",), vmem_limit_bytes=VMEM_LIMIT),
        name="ssm",
    )(u3, bre, bim, cre, cim, a_re.reshape(1, N_STATE), a_im.reshape(1, N_STATE),
      d_skip.reshape(1, w), w_glu, b_glu.reshape(1, w), out_g.reshape(1, w))


def _outproj_kernel(x_ref, a_ref, s_ref, ga_ref, woa_ref, wos_ref, gf_ref, wrt_ref, br_ref,
                    tri_ref, x1_ref, h2_ref, topi_ref, gate_ref, rank_ref, cnt_ref,
                    carry_ref, *, tm):
    i = pl.program_id(0)

    @pl.when(i == 0)
    def _():
        carry_ref[...] = jnp.zeros_like(carry_ref)

    a = _rms(a_ref[...].astype(F32), ga_ref[...]).astype(BF16)
    x1 = (x_ref[...] + jnp.dot(a, woa_ref[...], preferred_element_type=F32)
          + jnp.dot(s_ref[...], wos_ref[...], preferred_element_type=F32))
    x1_ref[...] = x1
    h2 = _rms(x1, gf_ref[...])
    for c in range(ROW_CHUNKS):
        h2_ref[:, c, :] = h2[:, c * LANES:(c + 1) * LANES]

    lg = lax.dot_general(wrt_ref[...], h2.astype(BF16), _NT,
                         preferred_element_type=F32) + br_ref[...]
    ids = lax.broadcasted_iota(I32, (N_EXPERTS, tm), 0)
    work = lg
    vals, idxs = [], []
    for _ in range(TOP_K):
        m = jnp.max(work, axis=0, keepdims=True)
        idx = jnp.min(jnp.where(work == m, ids, N_EXPERTS), axis=0, keepdims=True)
        vals.append(m)
        idxs.append(idx)
        work = jnp.where(ids == idx, -jnp.inf, work)
    exps = [jnp.exp(v - vals[0]) for v in vals]
    den = exps[0] + exps[1] + exps[2] + exps[3]
    gate_ref[...] = jnp.concatenate([e / den for e in exps], axis=0)
    topi_ref[...] = jnp.concatenate(idxs, axis=0)

    sel = jnp.zeros((N_EXPERTS, tm), F32)
    for idx in idxs:
        sel = sel + jnp.where(ids == idx, 1.0, 0.0)
    before = jnp.dot(sel.astype(BF16), tri_ref[...], preferred_element_type=F32)
    before = before + carry_ref[:, 0:1]
    ranks = [jnp.sum(jnp.where(ids == idx, before, 0.0), axis=0, keepdims=True) for idx in idxs]
    rank_ref[...] = jnp.concatenate(ranks, axis=0).astype(I32)
    total = carry_ref[...] + jnp.sum(sel, axis=1, keepdims=True)
    carry_ref[...] = total
    cnt_ref[...] = total


def _out_proj(x2, attn, ssm, attn_g, wo_a, wo_s, ffn_g, wrt, b_router, *, tm=512):
    t, d = x2.shape
    tri = jnp.triu(jnp.ones((tm, tm), F32), k=1).astype(BF16)
    kern = functools.partial(_outproj_kernel, tm=tm)
    row = lambda i: (i, 0)
    const = lambda i: (0, 0)
    colblk = lambda i: (0, i)
    return pl.pallas_call(
        kern,
        grid=(t // tm,),
        in_specs=[pl.BlockSpec((tm, d), row),
                  pl.BlockSpec((tm, ATTN_WIDTH), row),
                  pl.BlockSpec((tm, SSM_WIDTH), row),
                  pl.BlockSpec((1, ATTN_WIDTH), const),
                  pl.BlockSpec((ATTN_WIDTH, d), const),
                  pl.BlockSpec((SSM_WIDTH, d), const),
                  pl.BlockSpec((1, d), const),
                  pl.BlockSpec((N_EXPERTS, d), const),
                  pl.BlockSpec((N_EXPERTS, 1), const),
                  pl.BlockSpec((tm, tm), const)],
        out_specs=[pl.BlockSpec((tm, d), row),
                   pl.BlockSpec((tm, ROW_CHUNKS, LANES), lambda i: (i, 0, 0)),
                   pl.BlockSpec((TOP_K, tm), colblk),
                   pl.BlockSpec((TOP_K, tm), colblk),
                   pl.BlockSpec((TOP_K, tm), colblk),
                   pl.BlockSpec((N_EXPERTS, LANES), const)],
        out_shape=[jax.ShapeDtypeStruct((t, d), F32),
                   jax.ShapeDtypeStruct((t, ROW_CHUNKS, LANES), F32),
                   jax.ShapeDtypeStruct((TOP_K, t), I32),
                   jax.ShapeDtypeStruct((TOP_K, t), F32),
                   jax.ShapeDtypeStruct((TOP_K, t), I32),
                   jax.ShapeDtypeStruct((N_EXPERTS, LANES), F32)],
        scratch_shapes=[pltpu.VMEM((N_EXPERTS, LANES), F32)],
        compiler_params=pltpu.CompilerParams(
            dimension_semantics=("arbitrary",), vmem_limit_bytes=VMEM_LIMIT),
        name="out_proj",
    )(x2, attn, ssm, attn_g.reshape(1, -1), wo_a, wo_s, ffn_g.reshape(1, d), wrt,
      b_router.reshape(N_EXPERTS, 1), tri)


def _row_gather_start(src_hbm, idx_ref, buf, sem, slot, n_rows, unroll=16):
    def body(g, _):
        for k in range(unroll):
            r = g * unroll + k
            pltpu.make_async_copy(src_hbm.at[idx_ref[0, 0, r]], buf.at[slot, r],
                                  sem.at[slot]).start()
        return 0
    lax.fori_loop(0, n_rows // unroll, body, 0)


def _row_gather_wait(src_hbm, buf, sem, slot, n_rows):
    pltpu.make_async_copy(src_hbm.at[pl.ds(0, n_rows)], buf.at[slot], sem.at[slot]).wait()


def _rows_to_matrix(buf, slot):
    return jnp.concatenate([buf[slot, :, c, :] for c in range(ROW_CHUNKS)], axis=-1)


def _expert_kernel(blk_e_ref, nused_ref, tok_cur_ref, tok_nxt_ref, h2_hbm,
                   wgu_ref, bgu_ref, wdn_ref, bdn_ref, y_ref, xbuf, sem):
    del blk_e_ref
    i = pl.program_id(0)
    slot = i % 2
    nused = nused_ref[0]

    @pl.when(i == 0)
    def _():
        _row_gather_start(h2_hbm, tok_cur_ref, xbuf, sem, 0, MOE_ROWS)

    @pl.when(i + 1 < nused)
    def _():
        _row_gather_start(h2_hbm, tok_nxt_ref, xbuf, sem, 1 - slot, MOE_ROWS)

    @pl.when(i < nused)
    def _():
        _row_gather_wait(h2_hbm, xbuf, sem, slot, MOE_ROWS)
        x = _rows_to_matrix(xbuf, slot).astype(BF16)
        gu = jnp.dot(x, wgu_ref[...], preferred_element_type=F32) + bgu_ref[...]
        d_ff = gu.shape[1] // 2
        gate = jnp.minimum(gu[:, :d_ff], SWIGLU_LIMIT)
        up = jnp.clip(gu[:, d_ff:], -SWIGLU_LIMIT, SWIGLU_LIMIT)
        act = gate * _sigmoid(SWIGLU_ALPHA * gate) * (up + 1.0)
        y = jnp.dot(act.astype(BF16), wdn_ref[...], preferred_element_type=F32) + bdn_ref[...]
        for c in range(ROW_CHUNKS):
            y_ref[:, c, :] = y[:, c * LANES:(c + 1) * LANES]

    @pl.when(i >= nused)
    def _():
        y_ref[...] = jnp.zeros_like(y_ref)


def _experts(blk_expert, n_used, row_tok3, h2r, w_gu, b_gu, w_dn, b_dn):
    n_blocks = row_tok3.shape[0]
    e, d, f2 = w_gu.shape
    last = n_blocks - 1
    grid_spec = pltpu.PrefetchScalarGridSpec(
        num_scalar_prefetch=2,
        grid=(n_blocks,),
        in_specs=[
            pl.BlockSpec((1, 1, MOE_ROWS), lambda i, be, nu: (i, 0, 0),
                         memory_space=pltpu.SMEM),
            pl.BlockSpec((1, 1, MOE_ROWS), lambda i, be, nu: (jnp.minimum(i + 1, last), 0, 0),
                         memory_space=pltpu.SMEM),
            pl.BlockSpec(memory_space=pl.ANY),
            pl.BlockSpec((None, d, f2), lambda i, be, nu: (be[i], 0, 0)),
            pl.BlockSpec((None, 1, f2), lambda i, be, nu: (be[i], 0, 0)),
            pl.BlockSpec((None, f2 // 2, d), lambda i, be, nu: (be[i], 0, 0)),
            pl.BlockSpec((None, 1, d), lambda i, be, nu: (be[i], 0, 0)),
        ],
        out_specs=pl.BlockSpec((MOE_ROWS, ROW_CHUNKS, LANES), lambda i, be, nu: (i, 0, 0)),
        scratch_shapes=[pltpu.VMEM((2, MOE_ROWS, ROW_CHUNKS, LANES), F32),
                        pltpu.SemaphoreType.DMA((2,))],
    )
    return pl.pallas_call(
        _expert_kernel,
        grid_spec=grid_spec,
        out_shape=jax.ShapeDtypeStruct((n_blocks * MOE_ROWS, ROW_CHUNKS, LANES), F32),
        compiler_params=pltpu.CompilerParams(
            dimension_semantics=("arbitrary",), vmem_limit_bytes=VMEM_LIMIT),
        name="experts",
    )(blk_expert, n_used, row_tok3, row_tok3, h2r, w_gu, b_gu.reshape(e, 1, f2),
      w_dn, b_dn.reshape(e, 1, d))


def _combine_kernel(dst_cur_ref, dst_nxt_ref, y_hbm, x1_ref, gate_ref, p_ref, gp_ref,
                    wg_ref, wp_ref, gfin_ref, o_ref, ybuf, sem, *, tm, n_tiles):
    i = pl.program_id(0)
    slot = i % 2
    n_rows = TOP_K * tm

    @pl.when(i == 0)
    def _():
        _row_gather_start(y_hbm, dst_cur_ref, ybuf, sem, 0, n_rows)

    @pl.when(i + 1 < n_tiles)
    def _():
        _row_gather_start(y_hbm, dst_nxt_ref, ybuf, sem, 1 - slot, n_rows)

    _row_gather_wait(y_hbm, ybuf, sem, slot, n_rows)
    gates = gate_ref[...]
    x2 = x1_ref[...]
    for k in range(TOP_K):
        yk = jnp.concatenate(
            [ybuf[slot, pl.ds(k * tm, tm), c, :] for c in range(ROW_CHUNKS)], axis=-1)
        x2 = x2 + gates[:, k:k + 1] * yk
    hg = _rms(x2, gp_ref[...]).astype(BF16)
    gate = _sigmoid(jnp.dot(hg, wg_ref[...], preferred_element_type=F32))
    emb = jnp.dot(p_ref[...].astype(BF16), wp_ref[...], preferred_element_type=F32)
    x3 = x2 + gate * emb
    o_ref[...] = _rms(x3, gfin_ref[...])


def _combine(dest3, y, x1, gates_tk, p2, ple_g, w_gate, w_proj, fin_g, *, tm=256):
    t, d = x1.shape
    n_tiles = t // tm
    last = n_tiles - 1
    n_rows = TOP_K * tm
    ple = p2.shape[1]
    kern = functools.partial(_combine_kernel, tm=tm, n_tiles=n_tiles)
    row = lambda i: (i, 0)
    const = lambda i: (0, 0)
    return pl.pallas_call(
        kern,
        grid=(n_tiles,),
        in_specs=[pl.BlockSpec((1, 1, n_rows), lambda i: (i, 0, 0), memory_space=pltpu.SMEM),
                  pl.BlockSpec((1, 1, n_rows), lambda i: (jnp.minimum(i + 1, last), 0, 0),
                               memory_space=pltpu.SMEM),
                  pl.BlockSpec(memory_space=pl.ANY),
                  pl.BlockSpec((tm, d), row),
                  pl.BlockSpec((tm, TOP_K), row),
                  pl.BlockSpec((tm, ple), row),
                  pl.BlockSpec((1, d), const),
                  pl.BlockSpec((d, d), const),
                  pl.BlockSpec((ple, d), const),
                  pl.BlockSpec((1, d), const)],
        out_specs=pl.BlockSpec((tm, d), row),
        out_shape=jax.ShapeDtypeStruct((t, d), F32),
        scratch_shapes=[pltpu.VMEM((2, n_rows, ROW_CHUNKS, LANES), F32),
                        pltpu.SemaphoreType.DMA((2,))],
        compiler_params=pltpu.CompilerParams(
            dimension_semantics=("arbitrary",), vmem_limit_bytes=VMEM_LIMIT),
        name="combine",
    )(dest3, dest3, y, x1, gates_tk, p2, ple_g.reshape(1, d), w_gate, w_proj,
      fin_g.reshape(1, d))


def _routing_tables(topi, rank, counts, *, tm_combine):
    k, t = topi.shape
    n_blocks = (k * t) // MOE_ROWS + N_EXPERTS
    padded = (counts + MOE_ROWS - 1) // MOE_ROWS * MOE_ROWS
    pad_end = jnp.cumsum(padded)
    pad_start = pad_end - padded
    dest = pad_start[topi] + rank
    n_used = (pad_end[-1] // MOE_ROWS).astype(I32).reshape(1)
    blk_expert = jnp.minimum(
        jnp.searchsorted(pad_end, jnp.arange(n_blocks, dtype=I32) * MOE_ROWS, side='right'),
        N_EXPERTS - 1).astype(I32)
    tok = jnp.broadcast_to(jnp.arange(t, dtype=I32)[None, :], (k, t))
    row_tok = jnp.zeros((n_blocks * MOE_ROWS,), I32).at[dest.reshape(-1)].set(tok.reshape(-1))
    row_tok3 = row_tok.reshape(n_blocks, 1, MOE_ROWS)
    n_tiles = t // tm_combine
    dest3 = dest.reshape(k, n_tiles, tm_combine).transpose(1, 0, 2).reshape(
        n_tiles, 1, k * tm_combine)
    return blk_expert, n_used, row_tok3, dest3


def _layer(x2, p2, seq, norm_mix_g, w_in, b_f, lam_re, lam_im, log_dt, b_re, b_im, c_re, c_im,
           d_skip, w_glu, b_glu, attn_out_g, ssm_out_g, w_out, norm_ffn_g, w_router, b_router,
           w_gu, b_gu, w_dn, b_dn, norm_ple_g, w_ple_gate, w_ple_proj, final_g):
    t, d = x2.shape
    nb = t // seq
    aw = ATTN_WIDTH
    w_main = jnp.concatenate([w_in[:, :3 * aw], w_in[:, 3 * aw + N_HEADS:]], axis=1).astype(BF16)
    wft = w_in[:, 3 * aw:3 * aw + N_HEADS].T.astype(BF16)

    a_re, a_im, bbr, bbi = _ssm_prep(lam_re, lam_im, log_dt, b_re, b_im)
    bre = _block_diag(bbr, True).astype(BF16)
    bim = _block_diag(bbi, True).astype(BF16)
    cre = _block_diag(jnp.transpose(c_re, (0, 2, 1)), False).astype(BF16)
    cim = _block_diag(jnp.transpose(c_im, (0, 2, 1)), False).astype(BF16)

    q, k, v, u, c_t = _in_proj(x2, norm_mix_g, w_main, wft, b_f, seq=seq)
    shp = (nb, seq, aw)
    c3 = c_t.reshape(N_HEADS // 2, 2, t)
    attn = _attention(q.reshape(shp), k.reshape(shp), v.reshape(shp), c3)
    ssm = _ssm(u.reshape(nb, seq, SSM_WIDTH), bre, bim, cre, cim, a_re, a_im,
               d_skip, w_glu.astype(BF16), b_glu, ssm_out_g)

    x1, h2r, topi, gates, rank, cnt = _out_proj(
        x2, attn.reshape(t, aw), ssm.reshape(t, SSM_WIDTH), attn_out_g,
        w_out[:aw].astype(BF16), w_out[aw:].astype(BF16), norm_ffn_g,
        w_router.T.astype(BF16), b_router)

    tm_combine = 256
    counts = cnt[:, 0].astype(I32)
    blk_expert, n_used, row_tok3, dest3 = _routing_tables(topi, rank, counts,
                                                          tm_combine=tm_combine)
    y = _experts(blk_expert, n_used, row_tok3, h2r, w_gu.astype(BF16), b_gu,
                 w_dn.astype(BF16), b_dn)
    return _combine(dest3, y, x1, gates.T, p2, norm_ple_g, w_ple_gate.astype(BF16),
                    w_ple_proj.astype(BF16), final_g, tm=tm_combine)


def kernel(x, p, norm_mix_g, w_in, b_f, lam_re, lam_im, log_dt, b_re, b_im, c_re, c_im, d_skip, w_glu, b_glu, attn_out_g, ssm_out_g, w_out, norm_ffn_g, w_router, b_router, w_gu, b_gu, w_dn, b_dn, norm_ple_g, w_ple_gate, w_ple_proj, norm_final_g):
    bsz, seq, d = x.shape
    depth = w_in.shape[0]
    assert depth == 1, "one layer: the final rmsnorm is fused into the layer's last kernel"
    out = _layer(x.reshape(bsz * seq, d), p[0].reshape(bsz * seq, -1), seq,
                 norm_mix_g[0], w_in[0], b_f[0], lam_re[0], lam_im[0], log_dt[0],
                 b_re[0], b_im[0], c_re[0], c_im[0], d_skip[0], w_glu[0], b_glu[0],
                 attn_out_g[0], ssm_out_g[0], w_out[0], norm_ffn_g[0], w_router[0],
                 b_router[0], w_gu[0], b_gu[0], w_dn[0], b_dn[0], norm_ple_g[0],
                 w_ple_gate[0], w_ple_proj[0], norm_final_g)
    return out.reshape(bsz, seq, d)
```

```python
import functools
import math

import jax
import jax.numpy as jnp
from jax import lax
from jax.experimental import pallas as pl
from jax.experimental.pallas import tpu as pltpu

F32 = jnp.float32
BF16 = jnp.bfloat16
I32 = jnp.int32

NORM_EPS = 1e-5
HEAD_DIM = 64
N_HEADS = 8
ATTN_WIDTH = 512
SSM_WIDTH = 512
SSM_GROUP = 16
N_SSM_GROUPS = 32
SSM_STATE = 64
N_STATE = N_SSM_GROUPS * SSM_STATE
N_EXPERTS = 32
TOP_K = 4
SWIGLU_LIMIT = 7.0
SWIGLU_ALPHA = 1.702
LANES = 128
SUBLANES = 8
ROW_CHUNKS = 8
MOE_ROWS = 256
NEG_BIG = -1e30
VMEM_LIMIT = 56 * 1024 * 1024

_NT = (((1,), (1,)), ((), ()))


def _rms(xf, g):
    ms = jnp.mean(xf * xf, axis=-1, keepdims=True)
    return xf * lax.rsqrt(ms + NORM_EPS) * g


def _sigmoid(x):
    return 1.0 / (1.0 + jnp.exp(-x))


def _ssm_prep_kernel(lr_ref, li_ref, ldt_ref, brt_ref, bit_ref,
                     ar_ref, ai_ref, bbr_ref, bbi_ref):
    lr = lr_ref[...]
    li = li_ref[...]
    dt = jnp.exp(ldt_ref[...])
    mag = jnp.exp(lr * dt)
    ab_re = mag * jnp.cos(li * dt)
    ab_im = mag * jnp.sin(li * dt)
    den = lr * lr + li * li
    nr = ab_re - 1.0
    z_re = (nr * lr + ab_im * li) / den
    z_im = (ab_im * lr - nr * li) / den
    ar_ref[...] = ab_re
    ai_ref[...] = ab_im
    br = brt_ref[...]
    bi = bit_ref[...]
    bbr_ref[...] = z_re * br - z_im * bi
    bbi_ref[...] = z_re * bi + z_im * br


def _ssm_prep(lam_re, lam_im, log_dt, b_re, b_im):
    g, p, c = b_re.shape
    brt = jnp.transpose(b_re, (0, 2, 1))
    bit = jnp.transpose(b_im, (0, 2, 1))
    return pl.pallas_call(
        _ssm_prep_kernel,
        out_shape=(jax.ShapeDtypeStruct((g, 1, p), F32), jax.ShapeDtypeStruct((g, 1, p), F32),
                   jax.ShapeDtypeStruct((g, c, p), F32), jax.ShapeDtypeStruct((g, c, p), F32)),
        name="ssm_prep",
    )(lam_re.reshape(g, 1, p), lam_im.reshape(g, 1, p), log_dt.reshape(g, 1, 1), brt, bit)


def _block_diag(w, rows_first):
    g, a, b = w.shape
    half = g // 2
    eye = jnp.eye(half, dtype=w.dtype)
    w4 = w.reshape(2, half, a, b)
    out = jnp.einsum('hgab,gk->hgakb', w4, eye)
    del rows_first
    return out.reshape(2, half * a, half * b)


def _inproj_kernel(x_ref, g_ref, w_ref, wft_ref, bf_ref, tri_ref,
                   q_ref, k_ref, v_ref, u_ref, c_ref, carry_ref, *, tiles_per_seq, tm):
    i = pl.program_id(0)

    @pl.when(i % tiles_per_seq == 0)
    def _():
        carry_ref[...] = jnp.zeros_like(carry_ref)

    h = _rms(x_ref[...], g_ref[...]).astype(BF16)
    proj = jnp.dot(h, w_ref[...], preferred_element_type=F32)
    aw = ATTN_WIDTH
    q_ref[...] = (proj[:, 0:aw] * (HEAD_DIM ** -0.5)).astype(BF16)
    k_ref[...] = proj[:, aw:2 * aw].astype(BF16)
    v_ref[...] = proj[:, 2 * aw:3 * aw].astype(BF16)
    u_ref[...] = proj[:, 3 * aw:3 * aw + SSM_WIDTH].astype(BF16)

    fl = lax.dot_general(wft_ref[...], h, _NT, preferred_element_type=F32)
    z = fl + bf_ref[...]
    lf = jnp.minimum(z, 0.0) - jnp.log1p(jnp.exp(-jnp.abs(z)))
    hi = lf.astype(BF16)
    lo = (lf - hi.astype(F32)).astype(BF16)
    tri = tri_ref[...]
    cs = (jnp.dot(hi, tri, preferred_element_type=F32)
          + jnp.dot(lo, tri, preferred_element_type=F32))
    c = cs + carry_ref[:, 0:1]
    c_ref[...] = c
    carry_ref[...] = jnp.broadcast_to(c[:, tm - 1:tm], carry_ref.shape)


def _in_proj(x2, norm_g, w_main, wft, b_f, *, seq, tm=512):
    t, d = x2.shape
    n_main = w_main.shape[1]
    tri = jnp.triu(jnp.ones((tm, tm), F32)).astype(BF16)
    kern = functools.partial(_inproj_kernel, tiles_per_seq=seq // tm, tm=tm)
    row = lambda i: (i, 0)
    const = lambda i: (0, 0)
    act = jax.ShapeDtypeStruct((t, ATTN_WIDTH), BF16)
    return pl.pallas_call(
        kern,
        grid=(t // tm,),
        in_specs=[pl.BlockSpec((tm, d), row),
                  pl.BlockSpec((1, d), const),
                  pl.BlockSpec((d, n_main), const),
                  pl.BlockSpec((N_HEADS, d), const),
                  pl.BlockSpec((N_HEADS, 1), const),
                  pl.BlockSpec((tm, tm), const)],
        out_specs=[pl.BlockSpec((tm, ATTN_WIDTH), row)] * 4
        + [pl.BlockSpec((N_HEADS, tm), lambda i: (0, i))],
        out_shape=[act, act, act, act, jax.ShapeDtypeStruct((N_HEADS, t), F32)],
        scratch_shapes=[pltpu.VMEM((N_HEADS, LANES), F32)],
        compiler_params=pltpu.CompilerParams(
            dimension_semantics=("arbitrary",), vmem_limit_bytes=VMEM_LIMIT),
        name="in_proj",
    )(x2, norm_g.reshape(1, d), w_main, wft, b_f.reshape(N_HEADS, 1), tri)


def _attn_kernel(q_ref, k_ref, v_ref, c_ref, o_ref, *, tq):
    i = pl.program_id(2)
    q2 = q_ref[...]
    lane = lax.broadcasted_iota(I32, (1, LANES), 1)
    first = lane < HEAD_DIM
    zero = jnp.zeros_like(q2)
    q_heads = (jnp.where(first, q2, zero), jnp.where(first, zero, q2))

    def update(s, m, l, acc, vj):
        m_new = jnp.maximum(m, jnp.max(s, axis=-1, keepdims=True))
        alpha = jnp.exp(m - m_new)
        p = jnp.exp(s - m_new)
        l_new = alpha * l + jnp.sum(p, axis=-1, keepdims=True)
        acc_new = alpha * acc + jnp.dot(p.astype(BF16), vj, preferred_element_type=F32)
        return m_new, l_new, acc_new

    def block(j, carry, masked):
        off = pl.multiple_of(j * tq, tq)
        kj = k_ref[pl.ds(off, tq), :]
        vj = v_ref[pl.ds(off, tq), :]
        cj = c_ref[:, pl.ds(off, tq)]
        out = []
        for h in range(2):
            m, l, acc = carry[h]
            s = lax.dot_general(q_heads[h], kj, _NT, preferred_element_type=F32)
            s = s - cj[h:h + 1, :]
            if masked:
                r = lax.broadcasted_iota(I32, (tq, tq), 0)
                cc = lax.broadcasted_iota(I32, (tq, tq), 1)
                s = jnp.where(cc <= r, s, NEG_BIG)
            out.append(update(s, m, l, acc, vj))
        return tuple(out)

    init_one = (jnp.full((tq, 1), NEG_BIG, F32), jnp.zeros((tq, 1), F32),
                jnp.zeros((tq, LANES), F32))
    carry = lax.fori_loop(0, i, lambda j, c: block(j, c, False), (init_one, init_one))
    (_, l_a, acc_a), (_, l_b, acc_b) = block(i, carry, True)
    o = jnp.where(first, acc_a / l_a, acc_b / l_b)
    o_ref[...] = o.astype(BF16)


def _attention(q, k, v, c3, *, tq=512):
    b, s, w = q.shape
    n_pairs = w // LANES
    kern = functools.partial(_attn_kernel, tq=tq)
    return pl.pallas_call(
        kern,
        grid=(b, n_pairs, s // tq),
        in_specs=[pl.BlockSpec((None, tq, LANES), lambda bi, hp, i: (bi, i, hp)),
                  pl.BlockSpec((None, s, LANES), lambda bi, hp, i: (bi, 0, hp)),
                  pl.BlockSpec((None, s, LANES), lambda bi, hp, i: (bi, 0, hp)),
                  pl.BlockSpec((None, 2, s), lambda bi, hp, i: (hp, 0, bi))],
        out_specs=pl.BlockSpec((None, tq, LANES), lambda bi, hp, i: (bi, i, hp)),
        out_shape=jax.ShapeDtypeStruct((b, s, w), BF16),
        compiler_params=pltpu.CompilerParams(
            dimension_semantics=("arbitrary", "arbitrary", "arbitrary"),
            vmem_limit_bytes=VMEM_LIMIT),
        name="attention",
    )(q, k, v, c3)


def _ssm_kernel(u_ref, bre_ref, bim_ref, cre_ref, cim_ref, ar_ref, ai_ref, dsk_ref,
                wglu_ref, bglu_ref, g_ref, o_ref,
                us_ref, xr_ref, xi_ref, str_ref, sti_ref, res_ref, *, tt, nb):
    i = pl.program_id(0)

    @pl.when(i == 0)
    def _():
        str_ref[...] = jnp.zeros_like(str_ref)
        sti_ref[...] = jnp.zeros_like(sti_ref)

    n_chunks = SSM_WIDTH // LANES
    for b in range(nb):
        ub32 = u_ref[b].astype(F32)
        for c in range(n_chunks):
            us_ref[c, pl.ds(b, tt, stride=nb), :] = ub32[:, c * LANES:(c + 1) * LANES]
    uf = jnp.concatenate([us_ref[c] for c in range(n_chunks)], axis=-1)
    ub = uf.astype(BF16)
    half_in = SSM_WIDTH // 2
    half_st = N_STATE // 2
    for hf in range(2):
        uh = ub[:, hf * half_in:(hf + 1) * half_in]
        xr_ref[:, hf * half_st:(hf + 1) * half_st] = jnp.dot(
            uh, bre_ref[hf], preferred_element_type=F32)
        xi_ref[:, hf * half_st:(hf + 1) * half_st] = jnp.dot(
            uh, bim_ref[hf], preferred_element_type=F32)

    n_col_groups = 2
    wcol = N_STATE // n_col_groups
    unroll = 4
    for cg in range(n_col_groups):
        cols = slice(cg * wcol, (cg + 1) * wcol)
        ar = jnp.broadcast_to(ar_ref[:, cols], (nb, wcol))
        ai = jnp.broadcast_to(ai_ref[:, cols], (nb, wcol))

        def steps(tb, carry, cols=cols, ar=ar, ai=ai):
            sr, si = carry
            for k in range(unroll):
                r0 = pl.multiple_of((tb * unroll + k) * nb, nb)
                br = xr_ref[pl.ds(r0, nb), cols]
                bi = xi_ref[pl.ds(r0, nb), cols]
                nr = ar * sr - ai * si + br
                ni = ar * si + ai * sr + bi
                xr_ref[pl.ds(r0, nb), cols] = nr
                xi_ref[pl.ds(r0, nb), cols] = ni
                sr, si = nr, ni
            return sr, si

        sr, si = lax.fori_loop(0, tt // unroll, steps, (str_ref[:, cols], sti_ref[:, cols]))
        str_ref[:, cols] = sr
        sti_ref[:, cols] = si

    ys = []
    for hf in range(2):
        xr = xr_ref[:, hf * half_st:(hf + 1) * half_st].astype(BF16)
        xi = xi_ref[:, hf * half_st:(hf + 1) * half_st].astype(BF16)
        ys.append(jnp.dot(xr, cre_ref[hf], preferred_element_type=F32)
                  - jnp.dot(xi, cim_ref[hf], preferred_element_type=F32))
    y = jnp.concatenate(ys, axis=-1) + dsk_ref[...] * uf
    gl = 0.5 * y * (1.0 + jnp.tanh(math.sqrt(2.0 / math.pi) * (y + 0.044715 * (y * y * y))))
    zz = jnp.dot(gl.astype(BF16), wglu_ref[...], preferred_element_type=F32) + bglu_ref[...]
    out = gl * _sigmoid(zz)
    res = _rms(out, g_ref[...])
    for c in range(n_chunks):
        res_ref[c] = res[:, c * LANES:(c + 1) * LANES]
    for b in range(nb):
        o_ref[b] = jnp.concatenate(
            [res_ref[c, pl.ds(b, tt, stride=nb), :] for c in range(n_chunks)],
            axis=-1).astype(BF16)


def _ssm(u3, bre, bim, cre, cim, a_re, a_im, d_skip, w_glu, b_glu, out_g, *, tt=64):
    nb, s, w = u3.shape
    rows = tt * nb
    kern = functools.partial(_ssm_kernel, tt=tt, nb=nb)
    c3 = lambda i: (0, 0, 0)
    c2 = lambda i: (0, 0)
    return pl.pallas_call(
        kern,
        grid=(s // tt,),
        in_specs=[pl.BlockSpec((nb, tt, w), lambda i: (0, i, 0)),
                  pl.BlockSpec(bre.shape, c3), pl.BlockSpec(bim.shape, c3),
                  pl.BlockSpec(cre.shape, c3), pl.BlockSpec(cim.shape, c3),
                  pl.BlockSpec((1, N_STATE), c2), pl.BlockSpec((1, N_STATE), c2),
                  pl.BlockSpec((1, w), c2),
                  pl.BlockSpec((w, w), c2), pl.BlockSpec((1, w), c2), pl.BlockSpec((1, w), c2)],
        out_specs=pl.BlockSpec((nb, tt, w), lambda i: (0, i, 0)),
        out_shape=jax.ShapeDtypeStruct((nb, s, w), BF16),
        scratch_shapes=[pltpu.VMEM((w // LANES, rows, LANES), F32),
                        pltpu.VMEM((rows, N_STATE), F32), pltpu.VMEM((rows, N_STATE), F32),
                        pltpu.VMEM((nb, N_STATE), F32), pltpu.VMEM((nb, N_STATE), F32),
                        pltpu.VMEM((w // LANES, rows, LANES), F32)],
        compiler_params=pltpu.CompilerParams(
            dimension_semantics=("arbitrary",), vmem_limit_bytes=VMEM_LIMIT),
        name="ssm",
    )(u3, bre, bim, cre, cim, a_re.reshape(1, N_STATE), a_im.reshape(1, N_STATE),
      d_skip.reshape(1, w), w_glu, b_glu.reshape(1, w), out_g.reshape(1, w))


def _outproj_kernel(x_ref, a_ref, s_ref, ga_ref, woa_ref, wos_ref, gf_ref, wrt_ref, br_ref,
                    tri_ref, x1_ref, h2_ref, topi_ref, gate_ref, rank_ref, cnt_ref,
                    carry_ref, *, tm):
    i = pl.program_id(0)

    @pl.when(i == 0)
    def _():
        carry_ref[...] = jnp.zeros_like(carry_ref)

    a = _rms(a_ref[...].astype(F32), ga_ref[...]).astype(BF16)
    x1 = (x_ref[...] + jnp.dot(a, woa_ref[...], preferred_element_type=F32)
          + jnp.dot(s_ref[...], wos_ref[...], preferred_element_type=F32))
    x1_ref[...] = x1
    h2 = _rms(x1, gf_ref[...])
    for c in range(ROW_CHUNKS):
        h2_ref[:, c, :] = h2[:, c * LANES:(c + 1) * LANES]

    lg = lax.dot_general(wrt_ref[...], h2.astype(BF16), _NT,
                         preferred_element_type=F32) + br_ref[...]
    ids = lax.broadcasted_iota(I32, (N_EXPERTS, tm), 0)
    work = lg
    vals, idxs = [], []
    for _ in range(TOP_K):
        m = jnp.max(work, axis=0, keepdims=True)
        idx = jnp.min(jnp.where(work == m, ids, N_EXPERTS), axis=0, keepdims=True)
        vals.append(m)
        idxs.append(idx)
        work = jnp.where(ids == idx, -jnp.inf, work)
    exps = [jnp.exp(v - vals[0]) for v in vals]
    den = exps[0] + exps[1] + exps[2] + exps[3]
    gate_ref[...] = jnp.concatenate([e / den for e in exps], axis=0)
    topi_ref[...] = jnp.concatenate(idxs, axis=0)

    sel = jnp.zeros((N_EXPERTS, tm), F32)
    for idx in idxs:
        sel = sel + jnp.where(ids == idx, 1.0, 0.0)
    before = jnp.dot(sel.astype(BF16), tri_ref[...], preferred_element_type=F32)
    before = before + carry_ref[:, 0:1]
    ranks = [jnp.sum(jnp.where(ids == idx, before, 0.0), axis=0, keepdims=True) for idx in idxs]
    rank_ref[...] = jnp.concatenate(ranks, axis=0).astype(I32)
    total = carry_ref[...] + jnp.sum(sel, axis=1, keepdims=True)
    carry_ref[...] = total
    cnt_ref[...] = total


def _out_proj(x2, attn, ssm, attn_g, wo_a, wo_s, ffn_g, wrt, b_router, *, tm=512):
    t, d = x2.shape
    tri = jnp.triu(jnp.ones((tm, tm), F32), k=1).astype(BF16)
    kern = functools.partial(_outproj_kernel, tm=tm)
    row = lambda i: (i, 0)
    const = lambda i: (0, 0)
    colblk = lambda i: (0, i)
    return pl.pallas_call(
        kern,
        grid=(t // tm,),
        in_specs=[pl.BlockSpec((tm, d), row),
                  pl.BlockSpec((tm, ATTN_WIDTH), row),
                  pl.BlockSpec((tm, SSM_WIDTH), row),
                  pl.BlockSpec((1, ATTN_WIDTH), const),
                  pl.BlockSpec((ATTN_WIDTH, d), const),
                  pl.BlockSpec((SSM_WIDTH, d), const),
                  pl.BlockSpec((1, d), const),
                  pl.BlockSpec((N_EXPERTS, d), const),
                  pl.BlockSpec((N_EXPERTS, 1), const),
                  pl.BlockSpec((tm, tm), const)],
        out_specs=[pl.BlockSpec((tm, d), row),
                   pl.BlockSpec((tm, ROW_CHUNKS, LANES), lambda i: (i, 0, 0)),
                   pl.BlockSpec((TOP_K, tm), colblk),
                   pl.BlockSpec((TOP_K, tm), colblk),
                   pl.BlockSpec((TOP_K, tm), colblk),
                   pl.BlockSpec((N_EXPERTS, LANES), const)],
        out_shape=[jax.ShapeDtypeStruct((t, d), F32),
                   jax.ShapeDtypeStruct((t, ROW_CHUNKS, LANES), F32),
                   jax.ShapeDtypeStruct((TOP_K, t), I32),
                   jax.ShapeDtypeStruct((TOP_K, t), F32),
                   jax.ShapeDtypeStruct((TOP_K, t), I32),
                   jax.ShapeDtypeStruct((N_EXPERTS, LANES), F32)],
        scratch_shapes=[pltpu.VMEM((N_EXPERTS, LANES), F32)],
        compiler_params=pltpu.CompilerParams(
            dimension_semantics=("arbitrary",), vmem_limit_bytes=VMEM_LIMIT),
        name="out_proj",
    )(x2, attn, ssm, attn_g.reshape(1, -1), wo_a, wo_s, ffn_g.reshape(1, d), wrt,
      b_router.reshape(N_EXPERTS, 1), tri)


def _row_gather_start(src_hbm, idx_ref, buf, sem, slot, n_rows, unroll=16):
    def body(g, _):
        for k in range(unroll):
            r = g * unroll + k
            pltpu.make_async_copy(src_hbm.at[idx_ref[0, 0, r]], buf.at[slot, r],
                                  sem.at[slot]).start()
        return 0
    lax.fori_loop(0, n_rows // unroll, body, 0)


def _row_gather_wait(src_hbm, buf, sem, slot, n_rows):
    pltpu.make_async_copy(src_hbm.at[pl.ds(0, n_rows)], buf.at[slot], sem.at[slot]).wait()


def _rows_to_matrix(buf, slot):
    return jnp.concatenate([buf[slot, :, c, :] for c in range(ROW_CHUNKS)], axis=-1)


def _expert_kernel(blk_e_ref, nused_ref, tok_cur_ref, tok_nxt_ref, h2_hbm,
                   wgu_ref, bgu_ref, wdn_ref, bdn_ref, y_ref, xbuf, sem):
    del blk_e_ref
    i = pl.program_id(0)
    slot = i % 2
    nused = nused_ref[0]

    @pl.when(i == 0)
    def _():
        _row_gather_start(h2_hbm, tok_cur_ref, xbuf, sem, 0, MOE_ROWS)

    @pl.when(i + 1 < nused)
    def _():
        _row_gather_start(h2_hbm, tok_nxt_ref, xbuf, sem, 1 - slot, MOE_ROWS)

    @pl.when(i < nused)
    def _():
        _row_gather_wait(h2_hbm, xbuf, sem, slot, MOE_ROWS)
        x = _rows_to_matrix(xbuf, slot).astype(BF16)
        gu = jnp.dot(x, wgu_ref[...], preferred_element_type=F32) + bgu_ref[...]
        d_ff = gu.shape[1] // 2
        gate = jnp.minimum(gu[:, :d_ff], SWIGLU_LIMIT)
        up = jnp.clip(gu[:, d_ff:], -SWIGLU_LIMIT, SWIGLU_LIMIT)
        act = gate * _sigmoid(SWIGLU_ALPHA * gate) * (up + 1.0)
        y = jnp.dot(act.astype(BF16), wdn_ref[...], preferred_element_type=F32) + bdn_ref[...]
        for c in range(ROW_CHUNKS):
            y_ref[:, c, :] = y[:, c * LANES:(c + 1) * LANES]

    @pl.when(i >= nused)
    def _():
        y_ref[...] = jnp.zeros_like(y_ref)


def _experts(blk_expert, n_used, row_tok3, h2r, w_gu, b_gu, w_dn, b_dn):
    n_blocks = row_tok3.shape[0]
    e, d, f2 = w_gu.shape
    last = n_blocks - 1
    grid_spec = pltpu.PrefetchScalarGridSpec(
        num_scalar_prefetch=2,
        grid=(n_blocks,),
        in_specs=[
            pl.BlockSpec((1, 1, MOE_ROWS), lambda i, be, nu: (i, 0, 0),
                         memory_space=pltpu.SMEM),
            pl.BlockSpec((1, 1, MOE_ROWS), lambda i, be, nu: (jnp.minimum(i + 1, last), 0, 0),
                         memory_space=pltpu.SMEM),
            pl.BlockSpec(memory_space=pl.ANY),
            pl.BlockSpec((None, d, f2), lambda i, be, nu: (be[i], 0, 0)),
            pl.BlockSpec((None, 1, f2), lambda i, be, nu: (be[i], 0, 0)),
            pl.BlockSpec((None, f2 // 2, d), lambda i, be, nu: (be[i], 0, 0)),
            pl.BlockSpec((None, 1, d), lambda i, be, nu: (be[i], 0, 0)),
        ],
        out_specs=pl.BlockSpec((MOE_ROWS, ROW_CHUNKS, LANES), lambda i, be, nu: (i, 0, 0)),
        scratch_shapes=[pltpu.VMEM((2, MOE_ROWS, ROW_CHUNKS, LANES), F32),
                        pltpu.SemaphoreType.DMA((2,))],
    )
    return pl.pallas_call(
        _expert_kernel,
        grid_spec=grid_spec,
        out_shape=jax.ShapeDtypeStruct((n_blocks * MOE_ROWS, ROW_CHUNKS, LANES), F32),
        compiler_params=pltpu.CompilerParams(
            dimension_semantics=("arbitrary",), vmem_limit_bytes=VMEM_LIMIT),
        name="experts",
    )(blk_expert, n_used, row_tok3, row_tok3, h2r, w_gu, b_gu.reshape(e, 1, f2),
      w_dn, b_dn.reshape(e, 1, d))


def _combine_kernel(dst_cur_ref, dst_nxt_ref, y_hbm, x1_ref, gate_ref, p_ref, gp_ref,
                    wg_ref, wp_ref, gfin_ref, o_ref, ybuf, sem, *, tm, n_tiles):
    i = pl.program_id(0)
    slot = i % 2
    n_rows = TOP_K * tm

    @pl.when(i == 0)
    def _():
        _row_gather_start(y_hbm, dst_cur_ref, ybuf, sem, 0, n_rows)

    @pl.when(i + 1 < n_tiles)
    def _():
        _row_gather_start(y_hbm, dst_nxt_ref, ybuf, sem, 1 - slot, n_rows)

    _row_gather_wait(y_hbm, ybuf, sem, slot, n_rows)
    gates = gate_ref[...]
    x2 = x1_ref[...]
    for k in range(TOP_K):
        yk = jnp.concatenate(
            [ybuf[slot, pl.ds(k * tm, tm), c, :] for c in range(ROW_CHUNKS)], axis=-1)
        x2 = x2 + gates[:, k:k + 1] * yk
    hg = _rms(x2, gp_ref[...]).astype(BF16)
    gate = _sigmoid(jnp.dot(hg, wg_ref[...], preferred_element_type=F32))
    emb = jnp.dot(p_ref[...].astype(BF16), wp_ref[...], preferred_element_type=F32)
    x3 = x2 + gate * emb
    o_ref[...] = _rms(x3, gfin_ref[...])


def _combine(dest3, y, x1, gates_tk, p2, ple_g, w_gate, w_proj, fin_g, *, tm=256):
    t, d = x1.shape
    n_tiles = t // tm
    last = n_tiles - 1
    n_rows = TOP_K * tm
    ple = p2.shape[1]
    kern = functools.partial(_combine_kernel, tm=tm, n_tiles=n_tiles)
    row = lambda i: (i, 0)
    const = lambda i: (0, 0)
    return pl.pallas_call(
        kern,
        grid=(n_tiles,),
        in_specs=[pl.BlockSpec((1, 1, n_rows), lambda i: (i, 0, 0), memory_space=pltpu.SMEM),
                  pl.BlockSpec((1, 1, n_rows), lambda i: (jnp.minimum(i + 1, last), 0, 0),
                               memory_space=pltpu.SMEM),
                  pl.BlockSpec(memory_space=pl.ANY),
                  pl.BlockSpec((tm, d), row),
                  pl.BlockSpec((tm, TOP_K), row),
                  pl.BlockSpec((tm, ple), row),
                  pl.BlockSpec((1, d), const),
                  pl.BlockSpec((d, d), const),
                  pl.BlockSpec((ple, d), const),
                  pl.BlockSpec((1, d), const)],
        out_specs=pl.BlockSpec((tm, d), row),
        out_shape=jax.ShapeDtypeStruct((t, d), F32),
        scratch_shapes=[pltpu.VMEM((2, n_rows, ROW_CHUNKS, LANES), F32),
                        pltpu.SemaphoreType.DMA((2,))],
        compiler_params=pltpu.CompilerParams(
            dimension_semantics=("arbitrary",), vmem_limit_bytes=VMEM_LIMIT),
        name="combine",
    )(dest3, dest3, y, x1, gates_tk, p2, ple_g.reshape(1, d), w_gate, w_proj,
      fin_g.reshape(1, d))


def _routing_tables(topi, rank, counts, *, tm_combine):
    k, t = topi.shape
    n_blocks = (k * t) // MOE_ROWS + N_EXPERTS
    padded = (counts + MOE_ROWS - 1) // MOE_ROWS * MOE_ROWS
    pad_end = jnp.cumsum(padded)
    pad_start = pad_end - padded
    eids = jnp.arange(N_EXPERTS, dtype=I32)[:, None, None]
    dest = rank + jnp.sum(jnp.where(topi[None] == eids, pad_start[:, None, None], 0), axis=0)
    n_used = (pad_end[-1] // MOE_ROWS).astype(I32).reshape(1)
    blk_row0 = jnp.arange(n_blocks, dtype=I32) * MOE_ROWS
    blk_expert = jnp.minimum(
        jnp.sum((pad_end[None, :] <= blk_row0[:, None]).astype(I32), axis=1), N_EXPERTS - 1)
    tok = jnp.broadcast_to(jnp.arange(t, dtype=I32)[None, :], (k, t))
    row_tok = jnp.zeros((n_blocks * MOE_ROWS,), I32).at[dest.reshape(-1)].set(tok.reshape(-1))
    row_tok3 = row_tok.reshape(n_blocks, 1, MOE_ROWS)
    n_tiles = t // tm_combine
    dest3 = dest.reshape(k, n_tiles, tm_combine).transpose(1, 0, 2).reshape(
        n_tiles, 1, k * tm_combine)
    return blk_expert, n_used, row_tok3, dest3


def _layer(x2, p2, seq, norm_mix_g, w_in, b_f, lam_re, lam_im, log_dt, b_re, b_im, c_re, c_im,
           d_skip, w_glu, b_glu, attn_out_g, ssm_out_g, w_out, norm_ffn_g, w_router, b_router,
           w_gu, b_gu, w_dn, b_dn, norm_ple_g, w_ple_gate, w_ple_proj, final_g):
    t, d = x2.shape
    nb = t // seq
    aw = ATTN_WIDTH
    w_main = jnp.concatenate([w_in[:, :3 * aw], w_in[:, 3 * aw + N_HEADS:]], axis=1).astype(BF16)
    wft = w_in[:, 3 * aw:3 * aw + N_HEADS].T.astype(BF16)

    a_re, a_im, bbr, bbi = _ssm_prep(lam_re, lam_im, log_dt, b_re, b_im)
    bre = _block_diag(bbr, True).astype(BF16)
    bim = _block_diag(bbi, True).astype(BF16)
    cre = _block_diag(jnp.transpose(c_re, (0, 2, 1)), False).astype(BF16)
    cim = _block_diag(jnp.transpose(c_im, (0, 2, 1)), False).astype(BF16)

    q, k, v, u, c_t = _in_proj(x2, norm_mix_g, w_main, wft, b_f, seq=seq)
    shp = (nb, seq, aw)
    c3 = c_t.reshape(N_HEADS // 2, 2, t)
    attn = _attention(q.reshape(shp), k.reshape(shp), v.reshape(shp), c3)
    ssm = _ssm(u.reshape(nb, seq, SSM_WIDTH), bre, bim, cre, cim, a_re, a_im,
               d_skip, w_glu.astype(BF16), b_glu, ssm_out_g)

    x1, h2r, topi, gates, rank, cnt = _out_proj(
        x2, attn.reshape(t, aw), ssm.reshape(t, SSM_WIDTH), attn_out_g,
        w_out[:aw].astype(BF16), w_out[aw:].astype(BF16), norm_ffn_g,
        w_router.T.astype(BF16), b_router)

    tm_combine = 256
    counts = cnt[:, 0].astype(I32)
    blk_expert, n_used, row_tok3, dest3 = _routing_tables(topi, rank, counts,
                                                          tm_combine=tm_combine)
    y = _experts(blk_expert, n_used, row_tok3, h2r, w_gu.astype(BF16), b_gu,
                 w_dn.astype(BF16), b_dn)
    return _combine(dest3, y, x1, gates.T, p2, norm_ple_g, w_ple_gate.astype(BF16),
                    w_ple_proj.astype(BF16), final_g, tm=tm_combine)


def kernel(x, p, norm_mix_g, w_in, b_f, lam_re, lam_im, log_dt, b_re, b_im, c_re, c_im, d_skip, w_glu, b_glu, attn_out_g, ssm_out_g, w_out, norm_ffn_g, w_router, b_router, w_gu, b_gu, w_dn, b_dn, norm_ple_g, w_ple_gate, w_ple_proj, norm_final_g):
    bsz, seq, d = x.shape
    depth = w_in.shape[0]
    assert depth == 1, "one layer: the final rmsnorm is fused into the layer's last kernel"
    out = _layer(x.reshape(bsz * seq, d), p[0].reshape(bsz * seq, -1), seq,
                 norm_mix_g[0], w_in[0], b_f[0], lam_re[0], lam_im[0], log_dt[0],
                 b_re[0], b_im[0], c_re[0], c_im[0], d_skip[0], w_glu[0], b_glu[0],
                 attn_out_g[0], ssm_out_g[0], w_out[0], norm_ffn_g[0], w_router[0],
                 b_router[0], w_gu[0], b_gu[0], w_dn[0], b_dn[0], norm_ple_g[0],
                 w_ple_gate[0], w_ple_proj[0], norm_final_g)
    return out.reshape(bsz, seq, d)
```

```python
import functools
import math

import jax
import jax.numpy as jnp
from jax import lax
from jax.experimental import pallas as pl
from jax.experimental.pallas import tpu as pltpu

F32 = jnp.float32
BF16 = jnp.bfloat16
I32 = jnp.int32

NORM_EPS = 1e-5
HEAD_DIM = 64
N_HEADS = 8
ATTN_WIDTH = 512
SSM_WIDTH = 512
SSM_GROUP = 16
N_SSM_GROUPS = 32
SSM_STATE = 64
N_STATE = N_SSM_GROUPS * SSM_STATE
N_EXPERTS = 32
TOP_K = 4
SWIGLU_LIMIT = 7.0
SWIGLU_ALPHA = 1.702
LANES = 128
SUBLANES = 8
ROW_CHUNKS = 8
MOE_ROWS = 256
NEG_BIG = -1e30
LOG2E = math.log2(math.e)
VMEM_LIMIT = 56 * 1024 * 1024

_NT = (((1,), (1,)), ((), ()))


def _rms(xf, g):
    ms = jnp.mean(xf * xf, axis=-1, keepdims=True)
    return xf * lax.rsqrt(ms + NORM_EPS) * g


def _sigmoid(x):
    return 1.0 / (1.0 + jnp.exp(-x))


def _ssm_prep_kernel(lr_ref, li_ref, ldt_ref, brt_ref, bit_ref,
                     ar_ref, ai_ref, bbr_ref, bbi_ref):
    lr = lr_ref[...]
    li = li_ref[...]
    dt = jnp.exp(ldt_ref[...])
    mag = jnp.exp(lr * dt)
    ab_re = mag * jnp.cos(li * dt)
    ab_im = mag * jnp.sin(li * dt)
    den = lr * lr + li * li
    nr = ab_re - 1.0
    z_re = (nr * lr + ab_im * li) / den
    z_im = (ab_im * lr - nr * li) / den
    ar_ref[...] = ab_re
    ai_ref[...] = ab_im
    br = brt_ref[...]
    bi = bit_ref[...]
    bbr_ref[...] = z_re * br - z_im * bi
    bbi_ref[...] = z_re * bi + z_im * br


def _ssm_prep(lam_re, lam_im, log_dt, b_re, b_im):
    g, p, c = b_re.shape
    brt = jnp.transpose(b_re, (0, 2, 1))
    bit = jnp.transpose(b_im, (0, 2, 1))
    return pl.pallas_call(
        _ssm_prep_kernel,
        out_shape=(jax.ShapeDtypeStruct((g, 1, p), F32), jax.ShapeDtypeStruct((g, 1, p), F32),
                   jax.ShapeDtypeStruct((g, c, p), F32), jax.ShapeDtypeStruct((g, c, p), F32)),
        name="ssm_prep",
    )(lam_re.reshape(g, 1, p), lam_im.reshape(g, 1, p), log_dt.reshape(g, 1, 1), brt, bit)


def _block_diag(w, rows_first):
    g, a, b = w.shape
    half = g // 2
    eye = jnp.eye(half, dtype=w.dtype)
    w4 = w.reshape(2, half, a, b)
    out = jnp.einsum('hgab,gk->hgakb', w4, eye)
    del rows_first
    return out.reshape(2, half * a, half * b)


def _inproj_kernel(x_ref, g_ref, w_ref, wft_ref, bf_ref, tri_ref,
                   q_ref, k_ref, v_ref, u_ref, c_ref, carry_ref, *, tiles_per_seq, tm):
    i = pl.program_id(0)

    @pl.when(i % tiles_per_seq == 0)
    def _():
        carry_ref[...] = jnp.zeros_like(carry_ref)

    h = _rms(x_ref[...], g_ref[...]).astype(BF16)
    proj = jnp.dot(h, w_ref[...], preferred_element_type=F32)
    aw = ATTN_WIDTH
    q_ref[...] = (proj[:, 0:aw] * (LOG2E * HEAD_DIM ** -0.5)).astype(BF16)
    k_ref[...] = proj[:, aw:2 * aw].astype(BF16)
    v_ref[...] = proj[:, 2 * aw:3 * aw].astype(BF16)
    u_ref[...] = proj[:, 3 * aw:3 * aw + SSM_WIDTH].astype(BF16)

    fl = lax.dot_general(wft_ref[...], h, _NT, preferred_element_type=F32)
    z = fl + bf_ref[...]
    lf = jnp.minimum(z, 0.0) - jnp.log1p(jnp.exp(-jnp.abs(z)))
    hi = lf.astype(BF16)
    lo = (lf - hi.astype(F32)).astype(BF16)
    tri = tri_ref[...]
    cs = (jnp.dot(hi, tri, preferred_element_type=F32)
          + jnp.dot(lo, tri, preferred_element_type=F32))
    c = cs + carry_ref[:, 0:1]
    c_ref[...] = c * LOG2E
    carry_ref[...] = jnp.broadcast_to(c[:, tm - 1:tm], carry_ref.shape)


def _in_proj(x2, norm_g, w_main, wft, b_f, *, seq, tm=512):
    t, d = x2.shape
    n_main = w_main.shape[1]
    tri = jnp.triu(jnp.ones((tm, tm), F32)).astype(BF16)
    kern = functools.partial(_inproj_kernel, tiles_per_seq=seq // tm, tm=tm)
    row = lambda i: (i, 0)
    const = lambda i: (0, 0)
    act = jax.ShapeDtypeStruct((t, ATTN_WIDTH), BF16)
    return pl.pallas_call(
        kern,
        grid=(t // tm,),
        in_specs=[pl.BlockSpec((tm, d), row),
                  pl.BlockSpec((1, d), const),
                  pl.BlockSpec((d, n_main), const),
                  pl.BlockSpec((N_HEADS, d), const),
                  pl.BlockSpec((N_HEADS, 1), const),
                  pl.BlockSpec((tm, tm), const)],
        out_specs=[pl.BlockSpec((tm, ATTN_WIDTH), row)] * 4
        + [pl.BlockSpec((N_HEADS, tm), lambda i: (0, i))],
        out_shape=[act, act, act, act, jax.ShapeDtypeStruct((N_HEADS, t), F32)],
        scratch_shapes=[pltpu.VMEM((N_HEADS, LANES), F32)],
        compiler_params=pltpu.CompilerParams(
            dimension_semantics=("arbitrary",), vmem_limit_bytes=VMEM_LIMIT),
        name="in_proj",
    )(x2, norm_g.reshape(1, d), w_main, wft, b_f.reshape(N_HEADS, 1), tri)


def _attn_kernel(q_ref, k_ref, v_ref, c_ref, o_ref, *, tq):
    i = pl.program_id(2)
    q2 = q_ref[...]
    lane = lax.broadcasted_iota(I32, (1, LANES), 1)
    first = lane < HEAD_DIM
    zero = jnp.zeros_like(q2)
    q_heads = (jnp.where(first, q2, zero), jnp.where(first, zero, q2))

    def block(j, carry, masked):
        off = pl.multiple_of(j * tq, tq)
        kj = k_ref[pl.ds(off, tq), :]
        vj = v_ref[pl.ds(off, tq), :]
        cj = c_ref[:, pl.ds(off, tq)]
        one = jnp.ones_like(vj)
        v_heads = (jnp.where(first, vj, one), jnp.where(first, one, vj))
        out = []
        for h in range(2):
            m, acc = carry[h]
            s = (lax.dot_general(q_heads[h], kj, _NT, preferred_element_type=F32)
                 - cj[h:h + 1, :])
            if masked:
                rr = lax.broadcasted_iota(I32, (tq, tq), 0)
                cc = lax.broadcasted_iota(I32, (tq, tq), 1)
                s = jnp.where(cc <= rr, s, NEG_BIG)
            m_new = jnp.maximum(m, jnp.max(s, axis=-1, keepdims=True))
            alpha = jnp.exp2(m - m_new)
            p = jnp.exp2(s - m_new).astype(BF16)
            acc = alpha * acc + jnp.dot(p, v_heads[h], preferred_element_type=F32)
            out.append((m_new, acc))
        return tuple(out)

    init_one = (jnp.full((tq, 1), NEG_BIG, F32), jnp.zeros((tq, LANES), F32))
    carry = lax.fori_loop(0, i, lambda j, c: block(j, c, False), (init_one, init_one))
    (_, acc_a), (_, acc_b) = block(i, carry, True)
    o = jnp.where(first, acc_a / pltpu.roll(acc_a, HEAD_DIM, axis=1),
                  acc_b / pltpu.roll(acc_b, HEAD_DIM, axis=1))
    o_ref[...] = o.astype(BF16)


def _attention(q, k, v, c3, *, tq=512):
    b, s, w = q.shape
    n_pairs = w // LANES
    kern = functools.partial(_attn_kernel, tq=tq)
    return pl.pallas_call(
        kern,
        grid=(b, n_pairs, s // tq),
        in_specs=[pl.BlockSpec((None, tq, LANES), lambda bi, hp, i: (bi, i, hp)),
                  pl.BlockSpec((None, s, LANES), lambda bi, hp, i: (bi, 0, hp)),
                  pl.BlockSpec((None, s, LANES), lambda bi, hp, i: (bi, 0, hp)),
                  pl.BlockSpec((None, 2, s), lambda bi, hp, i: (hp, 0, bi))],
        out_specs=pl.BlockSpec((None, tq, LANES), lambda bi, hp, i: (bi, i, hp)),
        out_shape=jax.ShapeDtypeStruct((b, s, w), BF16),
        compiler_params=pltpu.CompilerParams(
            dimension_semantics=("arbitrary", "arbitrary", "arbitrary"),
            vmem_limit_bytes=VMEM_LIMIT),
        name="attention",
    )(q, k, v, c3)


def _ssm_kernel(u_ref, bre_ref, bim_ref, cre_ref, cim_ref, ar_ref, ai_ref, dsk_ref,
                wglu_ref, bglu_ref, g_ref, o_ref,
                us_ref, xr_ref, xi_ref, str_ref, sti_ref, res_ref, *, tt, nb):
    i = pl.program_id(0)

    @pl.when(i == 0)
    def _():
        str_ref[...] = jnp.zeros_like(str_ref)
        sti_ref[...] = jnp.zeros_like(sti_ref)

    n_chunks = SSM_WIDTH // LANES
    for b in range(nb):
        ub32 = u_ref[b].astype(F32)
        for c in range(n_chunks):
            us_ref[c, pl.ds(b, tt, stride=nb), :] = ub32[:, c * LANES:(c + 1) * LANES]
    uf = jnp.concatenate([us_ref[c] for c in range(n_chunks)], axis=-1)
    ub = uf.astype(BF16)
    half_in = SSM_WIDTH // 2
    half_st = N_STATE // 2
    for hf in range(2):
        uh = ub[:, hf * half_in:(hf + 1) * half_in]
        xr_ref[:, hf * half_st:(hf + 1) * half_st] = jnp.dot(
            uh, bre_ref[hf], preferred_element_type=F32)
        xi_ref[:, hf * half_st:(hf + 1) * half_st] = jnp.dot(
            uh, bim_ref[hf], preferred_element_type=F32)

    n_col_groups = 2
    wcol = N_STATE // n_col_groups
    unroll = 4
    for cg in range(n_col_groups):
        cols = slice(cg * wcol, (cg + 1) * wcol)
        ar = jnp.broadcast_to(ar_ref[:, cols], (nb, wcol))
        ai = jnp.broadcast_to(ai_ref[:, cols], (nb, wcol))

        def steps(tb, carry, cols=cols, ar=ar, ai=ai):
            sr, si = carry
            for k in range(unroll):
                r0 = pl.multiple_of((tb * unroll + k) * nb, nb)
                br = xr_ref[pl.ds(r0, nb), cols]
                bi = xi_ref[pl.ds(r0, nb), cols]
                nr = ar * sr - ai * si + br
                ni = ar * si + ai * sr + bi
                xr_ref[pl.ds(r0, nb), cols] = nr
                xi_ref[pl.ds(r0, nb), cols] = ni
                sr, si = nr, ni
            return sr, si

        sr, si = lax.fori_loop(0, tt // unroll, steps, (str_ref[:, cols], sti_ref[:, cols]))
        str_ref[:, cols] = sr
        sti_ref[:, cols] = si

    ys = []
    for hf in range(2):
        xr = xr_ref[:, hf * half_st:(hf + 1) * half_st].astype(BF16)
        xi = xi_ref[:, hf * half_st:(hf + 1) * half_st].astype(BF16)
        ys.append(jnp.dot(xr, cre_ref[hf], preferred_element_type=F32)
                  - jnp.dot(xi, cim_ref[hf], preferred_element_type=F32))
    y = jnp.concatenate(ys, axis=-1) + dsk_ref[...] * uf
    gl = 0.5 * y * (1.0 + jnp.tanh(math.sqrt(2.0 / math.pi) * (y + 0.044715 * (y * y * y))))
    zz = jnp.dot(gl.astype(BF16), wglu_ref[...], preferred_element_type=F32) + bglu_ref[...]
    out = gl * _sigmoid(zz)
    res = _rms(out, g_ref[...])
    for c in range(n_chunks):
        res_ref[c] = res[:, c * LANES:(c + 1) * LANES]
    for b in range(nb):
        o_ref[b] = jnp.concatenate(
            [res_ref[c, pl.ds(b, tt, stride=nb), :] for c in range(n_chunks)],
            axis=-1).astype(BF16)


def _ssm(u3, bre, bim, cre, cim, a_re, a_im, d_skip, w_glu, b_glu, out_g, *, tt=64):
    nb, s, w = u3.shape
    rows = tt * nb
    kern = functools.partial(_ssm_kernel, tt=tt, nb=nb)
    c3 = lambda i: (0, 0, 0)
    c2 = lambda i: (0, 0)
    return pl.pallas_call(
        kern,
        grid=(s // tt,),
        in_specs=[pl.BlockSpec((nb, tt, w), lambda i: (0, i, 0)),
                  pl.BlockSpec(bre.shape, c3), pl.BlockSpec(bim.shape, c3),
                  pl.BlockSpec(cre.shape, c3), pl.BlockSpec(cim.shape, c3),
                  pl.BlockSpec((1, N_STATE), c2), pl.BlockSpec((1, N_STATE), c2),
                  pl.BlockSpec((1, w), c2),
                  pl.BlockSpec((w, w), c2), pl.BlockSpec((1, w), c2), pl.BlockSpec((1, w), c2)],
        out_specs=pl.BlockSpec((nb, tt, w), lambda i: (0, i, 0)),
        out_shape=jax.ShapeDtypeStruct((nb, s, w), BF16),
        scratch_shapes=[pltpu.VMEM((w // LANES, rows, LANES), F32),
                        pltpu.VMEM((rows, N_STATE), F32), pltpu.VMEM((rows, N_STATE), F32),
                        pltpu.VMEM((nb, N_STATE), F32), pltpu.VMEM((nb, N_STATE), F32),
                        pltpu.VMEM((w // LANES, rows, LANES), F32)],
        compiler_params=pltpu.CompilerParams(
            dimension_semantics=("arbitrary",), vmem_limit_bytes=VMEM_LIMIT),
        name="ssm",
    )(u3, bre, bim, cre, cim, a_re.reshape(1, N_STATE), a_im.reshape(1, N_STATE),
      d_skip.reshape(1, w), w_glu, b_glu.reshape(1, w), out_g.reshape(1, w))


def _outproj_kernel(x_ref, a_ref, s_ref, ga_ref, woa_ref, wos_ref, gf_ref, wrt_ref, br_ref,
                    tri_ref, x1_ref, h2_ref, topi_ref, gate_ref, rank_ref, cnt_ref,
                    carry_ref, *, tm):
    i = pl.program_id(0)

    @pl.when(i == 0)
    def _():
        carry_ref[...] = jnp.zeros_like(carry_ref)

    a = _rms(a_ref[...].astype(F32), ga_ref[...]).astype(BF16)
    x1 = (x_ref[...] + jnp.dot(a, woa_ref[...], preferred_element_type=F32)
          + jnp.dot(s_ref[...], wos_ref[...], preferred_element_type=F32))
    x1_ref[...] = x1
    h2 = _rms(x1, gf_ref[...])
    for c in range(ROW_CHUNKS):
        h2_ref[:, c, :] = h2[:, c * LANES:(c + 1) * LANES]

    lg = lax.dot_general(wrt_ref[...], h2.astype(BF16), _NT,
                         preferred_element_type=F32) + br_ref[...]
    ids = lax.broadcasted_iota(I32, (N_EXPERTS, tm), 0)
    work = lg
    vals, idxs = [], []
    for _ in range(TOP_K):
        m = jnp.max(work, axis=0, keepdims=True)
        idx = jnp.min(jnp.where(work == m, ids, N_EXPERTS), axis=0, keepdims=True)
        vals.append(m)
        idxs.append(idx)
        work = jnp.where(ids == idx, -jnp.inf, work)
    exps = [jnp.exp(v - vals[0]) for v in vals]
    den = exps[0] + exps[1] + exps[2] + exps[3]
    gate_ref[...] = jnp.concatenate([e / den for e in exps], axis=0)
    topi_ref[...] = jnp.concatenate(idxs, axis=0)

    sel = jnp.zeros((N_EXPERTS, tm), F32)
    for idx in idxs:
        sel = sel + jnp.where(ids == idx, 1.0, 0.0)
    before = jnp.dot(sel.astype(BF16), tri_ref[...], preferred_element_type=F32)
    before = before + carry_ref[:, 0:1]
    ranks = [jnp.sum(jnp.where(ids == idx, before, 0.0), axis=0, keepdims=True) for idx in idxs]
    rank_ref[...] = jnp.concatenate(ranks, axis=0).astype(I32)
    total = carry_ref[...] + jnp.sum(sel, axis=1, keepdims=True)
    carry_ref[...] = total
    cnt_ref[...] = total


def _out_proj(x2, attn, ssm, attn_g, wo_a, wo_s, ffn_g, wrt, b_router, *, tm=512):
    t, d = x2.shape
    tri = jnp.triu(jnp.ones((tm, tm), F32), k=1).astype(BF16)
    kern = functools.partial(_outproj_kernel, tm=tm)
    row = lambda i: (i, 0)
    const = lambda i: (0, 0)
    colblk = lambda i: (0, i)
    return pl.pallas_call(
        kern,
        grid=(t // tm,),
        in_specs=[pl.BlockSpec((tm, d), row),
                  pl.BlockSpec((tm, ATTN_WIDTH), row),
                  pl.BlockSpec((tm, SSM_WIDTH), row),
                  pl.BlockSpec((1, ATTN_WIDTH), const),
                  pl.BlockSpec((ATTN_WIDTH, d), const),
                  pl.BlockSpec((SSM_WIDTH, d), const),
                  pl.BlockSpec((1, d), const),
                  pl.BlockSpec((N_EXPERTS, d), const),
                  pl.BlockSpec((N_EXPERTS, 1), const),
                  pl.BlockSpec((tm, tm), const)],
        out_specs=[pl.BlockSpec((tm, d), row),
                   pl.BlockSpec((tm, ROW_CHUNKS, LANES), lambda i: (i, 0, 0)),
                   pl.BlockSpec((TOP_K, tm), colblk),
                   pl.BlockSpec((TOP_K, tm), colblk),
                   pl.BlockSpec((TOP_K, tm), colblk),
                   pl.BlockSpec((N_EXPERTS, LANES), const)],
        out_shape=[jax.ShapeDtypeStruct((t, d), F32),
                   jax.ShapeDtypeStruct((t, ROW_CHUNKS, LANES), F32),
                   jax.ShapeDtypeStruct((TOP_K, t), I32),
                   jax.ShapeDtypeStruct((TOP_K, t), F32),
                   jax.ShapeDtypeStruct((TOP_K, t), I32),
                   jax.ShapeDtypeStruct((N_EXPERTS, LANES), F32)],
        scratch_shapes=[pltpu.VMEM((N_EXPERTS, LANES), F32)],
        compiler_params=pltpu.CompilerParams(
            dimension_semantics=("arbitrary",), vmem_limit_bytes=VMEM_LIMIT),
        name="out_proj",
    )(x2, attn, ssm, attn_g.reshape(1, -1), wo_a, wo_s, ffn_g.reshape(1, d), wrt,
      b_router.reshape(N_EXPERTS, 1), tri)


def _row_gather_start(src_hbm, idx_ref, buf, sem, slot, n_rows):
    for r in range(n_rows):
        pltpu.make_async_copy(
            src_hbm.at[idx_ref[0, 0, r]],
            buf.at[slot, pl.ds(r * ROW_CHUNKS, ROW_CHUNKS), :],
            sem.at[slot]).start(priority=r % 2)


def _row_gather_wait(buf, sem, slot):
    pltpu.make_async_copy(buf.at[slot], buf.at[slot], sem.at[slot]).wait()


def _rows_to_matrix(buf, slot, row0, n_rows):
    return jnp.concatenate(
        [buf[slot, pl.ds(row0 * ROW_CHUNKS + c, n_rows, stride=ROW_CHUNKS), :]
         for c in range(ROW_CHUNKS)], axis=-1)


def _expert_kernel(blk_e_ref, nused_ref, tok_cur_ref, tok_nxt_ref, h2_hbm,
                   wgu_ref, bgu_ref, wdn_ref, bdn_ref, y_ref, xbuf, sem):
    del blk_e_ref
    i = pl.program_id(0)
    slot = i % 2
    nused = nused_ref[0]

    @pl.when(i == 0)
    def _():
        _row_gather_start(h2_hbm, tok_cur_ref, xbuf, sem, 0, MOE_ROWS)

    @pl.when(i < nused)
    def _():
        _row_gather_start(h2_hbm, tok_nxt_ref, xbuf, sem, 1 - slot, MOE_ROWS)
        _row_gather_wait(xbuf, sem, slot)
        x = _rows_to_matrix(xbuf, slot, 0, MOE_ROWS).astype(BF16)
        gu = jnp.dot(x, wgu_ref[...], preferred_element_type=F32) + bgu_ref[...]
        d_ff = gu.shape[1] // 2
        gate = jnp.minimum(gu[:, :d_ff], SWIGLU_LIMIT)
        up = jnp.clip(gu[:, d_ff:], -SWIGLU_LIMIT, SWIGLU_LIMIT)
        act = gate * _sigmoid(SWIGLU_ALPHA * gate) * (up + 1.0)
        y = jnp.dot(act.astype(BF16), wdn_ref[...], preferred_element_type=F32) + bdn_ref[...]
        for c in range(ROW_CHUNKS):
            y_ref[pl.ds(c, MOE_ROWS, stride=ROW_CHUNKS), :] = y[:, c * LANES:(c + 1) * LANES]

        @pl.when(i == nused - 1)
        def _():
            _row_gather_wait(xbuf, sem, 1 - slot)

    @pl.when(i >= nused)
    def _():
        y_ref[...] = jnp.zeros_like(y_ref)


def _experts(blk_expert, n_used, row_tok3, h2r, w_gu, b_gu, w_dn, b_dn):
    n_blocks = row_tok3.shape[0]
    e, d, f2 = w_gu.shape
    blk_rows = MOE_ROWS * ROW_CHUNKS
    grid_spec = pltpu.PrefetchScalarGridSpec(
        num_scalar_prefetch=2,
        grid=(n_blocks,),
        in_specs=[
            pl.BlockSpec((1, 1, MOE_ROWS), lambda i, be, nu: (i, 0, 0),
                         memory_space=pltpu.SMEM),
            pl.BlockSpec((1, 1, MOE_ROWS),
                         lambda i, be, nu: (jnp.minimum(i + 1, nu[0] - 1), 0, 0),
                         memory_space=pltpu.SMEM),
            pl.BlockSpec(memory_space=pl.ANY),
            pl.BlockSpec((None, d, f2), lambda i, be, nu: (be[i], 0, 0)),
            pl.BlockSpec((None, 1, f2), lambda i, be, nu: (be[i], 0, 0)),
            pl.BlockSpec((None, f2 // 2, d), lambda i, be, nu: (be[i], 0, 0)),
            pl.BlockSpec((None, 1, d), lambda i, be, nu: (be[i], 0, 0)),
        ],
        out_specs=pl.BlockSpec((blk_rows, LANES), lambda i, be, nu: (i, 0)),
        scratch_shapes=[pltpu.VMEM((2, blk_rows, LANES), F32),
                        pltpu.SemaphoreType.DMA((2,))],
    )
    return pl.pallas_call(
        _expert_kernel,
        grid_spec=grid_spec,
        out_shape=jax.ShapeDtypeStruct((n_blocks * blk_rows, LANES), F32),
        compiler_params=pltpu.CompilerParams(
            dimension_semantics=("arbitrary",), vmem_limit_bytes=VMEM_LIMIT),
        name="experts",
    )(blk_expert, n_used, row_tok3, row_tok3, h2r, w_gu, b_gu.reshape(e, 1, f2),
      w_dn, b_dn.reshape(e, 1, d))


def _combine_kernel(dst_cur_ref, dst_nxt_ref, y_hbm, x1_ref, gate_ref, p_ref, gp_ref,
                    wg_ref, wp_ref, gfin_ref, o_ref, ybuf, sem, *, tm, n_tiles):
    i = pl.program_id(0)
    slot = i % 2
    n_rows = TOP_K * tm

    @pl.when(i == 0)
    def _():
        _row_gather_start(y_hbm, dst_cur_ref, ybuf, sem, 0, n_rows)

    _row_gather_start(y_hbm, dst_nxt_ref, ybuf, sem, 1 - slot, n_rows)
    _row_gather_wait(ybuf, sem, slot)
    gates = gate_ref[...]
    x2 = x1_ref[...]
    for k in range(TOP_K):
        x2 = x2 + gates[:, k:k + 1] * _rows_to_matrix(ybuf, slot, k * tm, tm)
    hg = _rms(x2, gp_ref[...]).astype(BF16)
    gate = _sigmoid(jnp.dot(hg, wg_ref[...], preferred_element_type=F32))
    emb = jnp.dot(p_ref[...].astype(BF16), wp_ref[...], preferred_element_type=F32)
    x3 = x2 + gate * emb
    o_ref[...] = _rms(x3, gfin_ref[...])

    @pl.when(i == n_tiles - 1)
    def _():
        _row_gather_wait(ybuf, sem, 1 - slot)


def _combine(dest3, y, x1, gates_tk, p2, ple_g, w_gate, w_proj, fin_g, *, tm=256):
    t, d = x1.shape
    n_tiles = t // tm
    last = n_tiles - 1
    n_rows = TOP_K * tm
    ple = p2.shape[1]
    kern = functools.partial(_combine_kernel, tm=tm, n_tiles=n_tiles)
    row = lambda i: (i, 0)
    const = lambda i: (0, 0)
    return pl.pallas_call(
        kern,
        grid=(n_tiles,),
        in_specs=[pl.BlockSpec((1, 1, n_rows), lambda i: (i, 0, 0), memory_space=pltpu.SMEM),
                  pl.BlockSpec((1, 1, n_rows), lambda i: (jnp.minimum(i + 1, last), 0, 0),
                               memory_space=pltpu.SMEM),
                  pl.BlockSpec(memory_space=pl.ANY),
                  pl.BlockSpec((tm, d), row),
                  pl.BlockSpec((tm, TOP_K), row),
                  pl.BlockSpec((tm, ple), row),
                  pl.BlockSpec((1, d), const),
                  pl.BlockSpec((d, d), const),
                  pl.BlockSpec((ple, d), const),
                  pl.BlockSpec((1, d), const)],
        out_specs=pl.BlockSpec((tm, d), row),
        out_shape=jax.ShapeDtypeStruct((t, d), F32),
        scratch_shapes=[pltpu.VMEM((2, n_rows * ROW_CHUNKS, LANES), F32),
                        pltpu.SemaphoreType.DMA((2,))],
        compiler_params=pltpu.CompilerParams(
            dimension_semantics=("arbitrary",), vmem_limit_bytes=VMEM_LIMIT),
        name="combine",
    )(dest3, dest3, y.reshape(-1, ROW_CHUNKS, LANES), x1, gates_tk, p2, ple_g.reshape(1, d),
      w_gate, w_proj, fin_g.reshape(1, d))


def _routing_tables(topi, rank, counts, *, tm_combine):
    k, t = topi.shape
    n_blocks = (k * t) // MOE_ROWS + N_EXPERTS
    padded = (counts + MOE_ROWS - 1) // MOE_ROWS * MOE_ROWS
    pad_end = jnp.cumsum(padded)
    pad_start = pad_end - padded
    eids = jnp.arange(N_EXPERTS, dtype=I32)[:, None, None]
    dest = rank + jnp.sum(jnp.where(topi[None] == eids, pad_start[:, None, None], 0), axis=0)
    n_used = (pad_end[-1] // MOE_ROWS).astype(I32).reshape(1)
    blk_row0 = jnp.arange(n_blocks, dtype=I32) * MOE_ROWS
    blk_expert = jnp.minimum(
        jnp.sum((pad_end[None, :] <= blk_row0[:, None]).astype(I32), axis=1), N_EXPERTS - 1)
    tok = jnp.broadcast_to(jnp.arange(t, dtype=I32)[None, :], (k, t))
    row_tok = jnp.zeros((n_blocks * MOE_ROWS,), I32).at[dest.reshape(-1)].set(tok.reshape(-1))
    row_tok3 = row_tok.reshape(n_blocks, 1, MOE_ROWS)
    n_tiles = t // tm_combine
    dest3 = dest.reshape(k, n_tiles, tm_combine).transpose(1, 0, 2).reshape(
        n_tiles, 1, k * tm_combine)
    return blk_expert, n_used, row_tok3, dest3


def _layer(x2, p2, seq, norm_mix_g, w_in, b_f, lam_re, lam_im, log_dt, b_re, b_im, c_re, c_im,
           d_skip, w_glu, b_glu, attn_out_g, ssm_out_g, w_out, norm_ffn_g, w_router, b_router,
           w_gu, b_gu, w_dn, b_dn, norm_ple_g, w_ple_gate, w_ple_proj, final_g):
    t, d = x2.shape
    nb = t // seq
    aw = ATTN_WIDTH
    w_main = jnp.concatenate([w_in[:, :3 * aw], w_in[:, 3 * aw + N_HEADS:]], axis=1).astype(BF16)
    wft = w_in[:, 3 * aw:3 * aw + N_HEADS].T.astype(BF16)

    a_re, a_im, bbr, bbi = _ssm_prep(lam_re, lam_im, log_dt, b_re, b_im)
    bre = _block_diag(bbr, True).astype(BF16)
    bim = _block_diag(bbi, True).astype(BF16)
    cre = _block_diag(jnp.transpose(c_re, (0, 2, 1)), False).astype(BF16)
    cim = _block_diag(jnp.transpose(c_im, (0, 2, 1)), False).astype(BF16)

    q, k, v, u, c_t = _in_proj(x2, norm_mix_g, w_main, wft, b_f, seq=seq)
    shp = (nb, seq, aw)
    c3 = c_t.reshape(N_HEADS // 2, 2, t)
    attn = _attention(q.reshape(shp), k.reshape(shp), v.reshape(shp), c3)
    ssm = _ssm(u.reshape(nb, seq, SSM_WIDTH), bre, bim, cre, cim, a_re, a_im,
               d_skip, w_glu.astype(BF16), b_glu, ssm_out_g)

    x1, h2r, topi, gates, rank, cnt = _out_proj(
        x2, attn.reshape(t, aw), ssm.reshape(t, SSM_WIDTH), attn_out_g,
        w_out[:aw].astype(BF16), w_out[aw:].astype(BF16), norm_ffn_g,
        w_router.T.astype(BF16), b_router)

    tm_combine = 256
    counts = cnt[:, 0].astype(I32)
    blk_expert, n_used, row_tok3, dest3 = _routing_tables(topi, rank, counts,
                                                          tm_combine=tm_combine)
    y = _experts(blk_expert, n_used, row_tok3, h2r, w_gu.astype(BF16), b_gu,
                 w_dn.astype(BF16), b_dn)
    return _combine(dest3, y, x1, gates.T, p2, norm_ple_g, w_ple_gate.astype(BF16),
                    w_ple_proj.astype(BF16), final_g, tm=tm_combine)


def kernel(x, p, norm_mix_g, w_in, b_f, lam_re, lam_im, log_dt, b_re, b_im, c_re, c_im, d_skip, w_glu, b_glu, attn_out_g, ssm_out_g, w_out, norm_ffn_g, w_router, b_router, w_gu, b_gu, w_dn, b_dn, norm_ple_g, w_ple_gate, w_ple_proj, norm_final_g):
    bsz, seq, d = x.shape
    depth = w_in.shape[0]
    assert depth == 1, "one layer: the final rmsnorm is fused into the layer's last kernel"
    out = _layer(x.reshape(bsz * seq, d), p[0].reshape(bsz * seq, -1), seq,
                 norm_mix_g[0], w_in[0], b_f[0], lam_re[0], lam_im[0], log_dt[0],
                 b_re[0], b_im[0], c_re[0], c_im[0], d_skip[0], w_glu[0], b_glu[0],
                 attn_out_g[0], ssm_out_g[0], w_out[0], norm_ffn_g[0], w_router[0],
                 b_router[0], w_gu[0], b_gu[0], w_dn[0], b_dn[0], norm_ple_g[0],
                 w_ple_gate[0], w_ple_proj[0], norm_final_g)
    return out.reshape(bsz, seq, d)
```

```python
import functools
import math

import jax
import jax.numpy as jnp
from jax import lax
from jax.experimental import pallas as pl
from jax.experimental.pallas import tpu as pltpu

F32 = jnp.float32
BF16 = jnp.bfloat16
I32 = jnp.int32

NORM_EPS = 1e-5
HEAD_DIM = 64
N_HEADS = 8
ATTN_WIDTH = 512
SSM_WIDTH = 512
SSM_GROUP = 16
N_SSM_GROUPS = 32
SSM_STATE = 64
N_STATE = N_SSM_GROUPS * SSM_STATE
N_EXPERTS = 32
TOP_K = 4
SWIGLU_LIMIT = 7.0
SWIGLU_ALPHA = 1.702
LANES = 128
SUBLANES = 8
ROW_CHUNKS = 8
MOE_ROWS = 256
NEG_BIG = -1e30
LOG2E = math.log2(math.e)
VMEM_LIMIT = 56 * 1024 * 1024

_NT = (((1,), (1,)), ((), ()))


def _rms(xf, g):
    ms = jnp.mean(xf * xf, axis=-1, keepdims=True)
    return xf * lax.rsqrt(ms + NORM_EPS) * g


def _sigmoid(x):
    return 1.0 / (1.0 + jnp.exp(-x))


def _ssm_prep_kernel(lr_ref, li_ref, ldt_ref, brt_ref, bit_ref,
                     ar_ref, ai_ref, bbr_ref, bbi_ref):
    lr = lr_ref[...]
    li = li_ref[...]
    dt = jnp.exp(ldt_ref[...])
    mag = jnp.exp(lr * dt)
    ab_re = mag * jnp.cos(li * dt)
    ab_im = mag * jnp.sin(li * dt)
    den = lr * lr + li * li
    nr = ab_re - 1.0
    z_re = (nr * lr + ab_im * li) / den
    z_im = (ab_im * lr - nr * li) / den
    ar_ref[...] = ab_re
    ai_ref[...] = ab_im
    br = brt_ref[...]
    bi = bit_ref[...]
    bbr_ref[...] = z_re * br - z_im * bi
    bbi_ref[...] = z_re * bi + z_im * br


def _ssm_prep(lam_re, lam_im, log_dt, b_re, b_im):
    g, p, c = b_re.shape
    brt = jnp.transpose(b_re, (0, 2, 1))
    bit = jnp.transpose(b_im, (0, 2, 1))
    return pl.pallas_call(
        _ssm_prep_kernel,
        out_shape=(jax.ShapeDtypeStruct((g, 1, p), F32), jax.ShapeDtypeStruct((g, 1, p), F32),
                   jax.ShapeDtypeStruct((g, c, p), F32), jax.ShapeDtypeStruct((g, c, p), F32)),
        name="ssm_prep",
    )(lam_re.reshape(g, 1, p), lam_im.reshape(g, 1, p), log_dt.reshape(g, 1, 1), brt, bit)


def _block_diag(w, rows_first):
    g, a, b = w.shape
    half = g // 2
    eye = jnp.eye(half, dtype=w.dtype)
    w4 = w.reshape(2, half, a, b)
    out = jnp.einsum('hgab,gk->hgakb', w4, eye)
    del rows_first
    return out.reshape(2, half * a, half * b)


def _inproj_kernel(x_ref, g_ref, w_ref, wft_ref, bf_ref, tri_ref,
                   q_ref, k_ref, v_ref, u_ref, c_ref, carry_ref, *, tiles_per_seq, tm):
    i = pl.program_id(0)

    @pl.when(i % tiles_per_seq == 0)
    def _():
        carry_ref[...] = jnp.zeros_like(carry_ref)

    h = _rms(x_ref[...], g_ref[...]).astype(BF16)
    proj = jnp.dot(h, w_ref[...], preferred_element_type=F32)
    aw = ATTN_WIDTH
    q_ref[...] = (proj[:, 0:aw] * (LOG2E * HEAD_DIM ** -0.5)).astype(BF16)
    k_ref[...] = proj[:, aw:2 * aw].astype(BF16)
    v_ref[...] = proj[:, 2 * aw:3 * aw].astype(BF16)
    u_ref[...] = proj[:, 3 * aw:3 * aw + SSM_WIDTH].astype(BF16)

    fl = lax.dot_general(wft_ref[...], h, _NT, preferred_element_type=F32)
    z = fl + bf_ref[...]
    lf = jnp.minimum(z, 0.0) - jnp.log1p(jnp.exp(-jnp.abs(z)))
    hi = lf.astype(BF16)
    lo = (lf - hi.astype(F32)).astype(BF16)
    tri = tri_ref[...]
    cs = (jnp.dot(hi, tri, preferred_element_type=F32)
          + jnp.dot(lo, tri, preferred_element_type=F32))
    c = cs + carry_ref[:, 0:1]
    c_ref[...] = c * LOG2E
    carry_ref[...] = jnp.broadcast_to(c[:, tm - 1:tm], carry_ref.shape)


def _in_proj(x2, norm_g, w_main, wft, b_f, *, seq, tm=512):
    t, d = x2.shape
    n_main = w_main.shape[1]
    tri = jnp.triu(jnp.ones((tm, tm), F32)).astype(BF16)
    kern = functools.partial(_inproj_kernel, tiles_per_seq=seq // tm, tm=tm)
    row = lambda i: (i, 0)
    const = lambda i: (0, 0)
    act = jax.ShapeDtypeStruct((t, ATTN_WIDTH), BF16)
    return pl.pallas_call(
        kern,
        grid=(t // tm,),
        in_specs=[pl.BlockSpec((tm, d), row),
                  pl.BlockSpec((1, d), const),
                  pl.BlockSpec((d, n_main), const),
                  pl.BlockSpec((N_HEADS, d), const),
                  pl.BlockSpec((N_HEADS, 1), const),
                  pl.BlockSpec((tm, tm), const)],
        out_specs=[pl.BlockSpec((tm, ATTN_WIDTH), row)] * 4
        + [pl.BlockSpec((N_HEADS, tm), lambda i: (0, i))],
        out_shape=[act, act, act, act, jax.ShapeDtypeStruct((N_HEADS, t), F32)],
        scratch_shapes=[pltpu.VMEM((N_HEADS, LANES), F32)],
        compiler_params=pltpu.CompilerParams(
            dimension_semantics=("arbitrary",), vmem_limit_bytes=VMEM_LIMIT),
        name="in_proj",
    )(x2, norm_g.reshape(1, d), w_main, wft, b_f.reshape(N_HEADS, 1), tri)


def _attn_kernel(q_ref, k_ref, v_ref, c_ref, o_ref, *, tq):
    i = pl.program_id(2)
    q2 = q_ref[...]
    lane = lax.broadcasted_iota(I32, (1, LANES), 1)
    first = lane < HEAD_DIM
    zero = jnp.zeros_like(q2)
    q_heads = (jnp.where(first, q2, zero), jnp.where(first, zero, q2))

    def block(j, carry, masked):
        off = pl.multiple_of(j * tq, tq)
        kj = k_ref[pl.ds(off, tq), :]
        vj = v_ref[pl.ds(off, tq), :]
        cj = c_ref[:, pl.ds(off, tq)]
        one = jnp.ones_like(vj)
        v_heads = (jnp.where(first, vj, one), jnp.where(first, one, vj))
        out = []
        for h in range(2):
            m, acc = carry[h]
            s = (lax.dot_general(q_heads[h], kj, _NT, preferred_element_type=F32)
                 - cj[h:h + 1, :])
            if masked:
                rr = lax.broadcasted_iota(I32, (tq, tq), 0)
                cc = lax.broadcasted_iota(I32, (tq, tq), 1)
                s = jnp.where(cc <= rr, s, NEG_BIG)
            m_new = jnp.maximum(m, jnp.max(s, axis=-1, keepdims=True))
            alpha = jnp.exp2(m - m_new)
            p = jnp.exp2(s - m_new).astype(BF16)
            acc = alpha * acc + jnp.dot(p, v_heads[h], preferred_element_type=F32)
            out.append((m_new, acc))
        return tuple(out)

    init_one = (jnp.full((tq, 1), NEG_BIG, F32), jnp.zeros((tq, LANES), F32))
    carry = lax.fori_loop(0, i, lambda j, c: block(j, c, False), (init_one, init_one))
    (_, acc_a), (_, acc_b) = block(i, carry, True)
    o = jnp.where(first, acc_a / pltpu.roll(acc_a, HEAD_DIM, axis=1),
                  acc_b / pltpu.roll(acc_b, HEAD_DIM, axis=1))
    o_ref[...] = o.astype(BF16)


def _attention(q, k, v, c3, *, tq=512):
    b, s, w = q.shape
    n_pairs = w // LANES
    kern = functools.partial(_attn_kernel, tq=tq)
    return pl.pallas_call(
        kern,
        grid=(b, n_pairs, s // tq),
        in_specs=[pl.BlockSpec((None, tq, LANES), lambda bi, hp, i: (bi, i, hp)),
                  pl.BlockSpec((None, s, LANES), lambda bi, hp, i: (bi, 0, hp)),
                  pl.BlockSpec((None, s, LANES), lambda bi, hp, i: (bi, 0, hp)),
                  pl.BlockSpec((None, 2, s), lambda bi, hp, i: (hp, 0, bi))],
        out_specs=pl.BlockSpec((None, tq, LANES), lambda bi, hp, i: (bi, i, hp)),
        out_shape=jax.ShapeDtypeStruct((b, s, w), BF16),
        compiler_params=pltpu.CompilerParams(
            dimension_semantics=("arbitrary", "arbitrary", "arbitrary"),
            vmem_limit_bytes=VMEM_LIMIT),
        name="attention",
    )(q, k, v, c3)


def _ssm_kernel(u_ref, bre_ref, bim_ref, cre_ref, cim_ref, ar_ref, ai_ref, dsk_ref,
                wglu_ref, bglu_ref, g_ref, o_ref,
                us_ref, xr_ref, xi_ref, str_ref, sti_ref, res_ref, *, tt, nb):
    i = pl.program_id(0)

    @pl.when(i == 0)
    def _():
        str_ref[...] = jnp.zeros_like(str_ref)
        sti_ref[...] = jnp.zeros_like(sti_ref)

    n_chunks = SSM_WIDTH // LANES
    for b in range(nb):
        ub32 = u_ref[b].astype(F32)
        for c in range(n_chunks):
            us_ref[c, pl.ds(b, tt, stride=nb), :] = ub32[:, c * LANES:(c + 1) * LANES]
    uf = jnp.concatenate([us_ref[c] for c in range(n_chunks)], axis=-1)
    ub = uf.astype(BF16)
    half_in = SSM_WIDTH // 2
    half_st = N_STATE // 2
    for hf in range(2):
        uh = ub[:, hf * half_in:(hf + 1) * half_in]
        xr_ref[:, hf * half_st:(hf + 1) * half_st] = jnp.dot(
            uh, bre_ref[hf], preferred_element_type=F32)
        xi_ref[:, hf * half_st:(hf + 1) * half_st] = jnp.dot(
            uh, bim_ref[hf], preferred_element_type=F32)

    n_col_groups = 2
    wcol = N_STATE // n_col_groups
    unroll = 4
    for cg in range(n_col_groups):
        cols = slice(cg * wcol, (cg + 1) * wcol)
        ar = jnp.broadcast_to(ar_ref[:, cols], (nb, wcol))
        ai = jnp.broadcast_to(ai_ref[:, cols], (nb, wcol))

        def steps(tb, carry, cols=cols, ar=ar, ai=ai):
            sr, si = carry
            for k in range(unroll):
                r0 = pl.multiple_of((tb * unroll + k) * nb, nb)
                br = xr_ref[pl.ds(r0, nb), cols]
                bi = xi_ref[pl.ds(r0, nb), cols]
                nr = ar * sr - ai * si + br
                ni = ar * si + ai * sr + bi
                xr_ref[pl.ds(r0, nb), cols] = nr
                xi_ref[pl.ds(r0, nb), cols] = ni
                sr, si = nr, ni
            return sr, si

        sr, si = lax.fori_loop(0, tt // unroll, steps, (str_ref[:, cols], sti_ref[:, cols]))
        str_ref[:, cols] = sr
        sti_ref[:, cols] = si

    ys = []
    for hf in range(2):
        xr = xr_ref[:, hf * half_st:(hf + 1) * half_st].astype(BF16)
        xi = xi_ref[:, hf * half_st:(hf + 1) * half_st].astype(BF16)
        ys.append(jnp.dot(xr, cre_ref[hf], preferred_element_type=F32)
                  - jnp.dot(xi, cim_ref[hf], preferred_element_type=F32))
    y = jnp.concatenate(ys, axis=-1) + dsk_ref[...] * uf
    gl = 0.5 * y * (1.0 + jnp.tanh(math.sqrt(2.0 / math.pi) * (y + 0.044715 * (y * y * y))))
    zz = jnp.dot(gl.astype(BF16), wglu_ref[...], preferred_element_type=F32) + bglu_ref[...]
    out = gl * _sigmoid(zz)
    res = _rms(out, g_ref[...])
    for c in range(n_chunks):
        res_ref[c] = res[:, c * LANES:(c + 1) * LANES]
    for b in range(nb):
        o_ref[b] = jnp.concatenate(
            [res_ref[c, pl.ds(b, tt, stride=nb), :] for c in range(n_chunks)],
            axis=-1).astype(BF16)


def _ssm(u3, bre, bim, cre, cim, a_re, a_im, d_skip, w_glu, b_glu, out_g, *, tt=64):
    nb, s, w = u3.shape
    rows = tt * nb
    kern = functools.partial(_ssm_kernel, tt=tt, nb=nb)
    c3 = lambda i: (0, 0, 0)
    c2 = lambda i: (0, 0)
    return pl.pallas_call(
        kern,
        grid=(s // tt,),
        in_specs=[pl.BlockSpec((nb, tt, w), lambda i: (0, i, 0)),
                  pl.BlockSpec(bre.shape, c3), pl.BlockSpec(bim.shape, c3),
                  pl.BlockSpec(cre.shape, c3), pl.BlockSpec(cim.shape, c3),
                  pl.BlockSpec((1, N_STATE), c2), pl.BlockSpec((1, N_STATE), c2),
                  pl.BlockSpec((1, w), c2),
                  pl.BlockSpec((w, w), c2), pl.BlockSpec((1, w), c2), pl.BlockSpec((1, w), c2)],
        out_specs=pl.BlockSpec((nb, tt, w), lambda i: (0, i, 0)),
        out_shape=jax.ShapeDtypeStruct((nb, s, w), BF16),
        scratch_shapes=[pltpu.VMEM((w // LANES, rows, LANES), F32),
                        pltpu.VMEM((rows, N_STATE), F32), pltpu.VMEM((rows, N_STATE), F32),
                        pltpu.VMEM((nb, N_STATE), F32), pltpu.VMEM((nb, N_STATE), F32),
                        pltpu.VMEM((w // LANES, rows, LANES), F32)],
        compiler_params=pltpu.CompilerParams(
            dimension_semantics=("arbitrary",), vmem_limit_bytes=VMEM_LIMIT),
        name="ssm",
    )(u3, bre, bim, cre, cim, a_re.reshape(1, N_STATE), a_im.reshape(1, N_STATE),
      d_skip.reshape(1, w), w_glu, b_glu.reshape(1, w), out_g.reshape(1, w))


def _outproj_kernel(x_ref, a_ref, s_ref, ga_ref, woa_ref, wos_ref, gf_ref, wrt_ref, br_ref,
                    tri_ref, x1_ref, h2_ref, topi_ref, gate_ref, rank_ref, cnt_ref,
                    carry_ref, *, tm):
    i = pl.program_id(0)

    @pl.when(i == 0)
    def _():
        carry_ref[...] = jnp.zeros_like(carry_ref)

    a = _rms(a_ref[...].astype(F32), ga_ref[...]).astype(BF16)
    x1 = (x_ref[...] + jnp.dot(a, woa_ref[...], preferred_element_type=F32)
          + jnp.dot(s_ref[...], wos_ref[...], preferred_element_type=F32))
    x1_ref[...] = x1
    h2 = _rms(x1, gf_ref[...])
    for c in range(ROW_CHUNKS):
        h2_ref[:, c, :] = h2[:, c * LANES:(c + 1) * LANES]

    lg = lax.dot_general(wrt_ref[...], h2.astype(BF16), _NT,
                         preferred_element_type=F32) + br_ref[...]
    ids = lax.broadcasted_iota(I32, (N_EXPERTS, tm), 0)
    work = lg
    vals, idxs = [], []
    for _ in range(TOP_K):
        m = jnp.max(work, axis=0, keepdims=True)
        idx = jnp.min(jnp.where(work == m, ids, N_EXPERTS), axis=0, keepdims=True)
        vals.append(m)
        idxs.append(idx)
        work = jnp.where(ids == idx, -jnp.inf, work)
    exps = [jnp.exp(v - vals[0]) for v in vals]
    den = exps[0] + exps[1] + exps[2] + exps[3]
    gate_ref[...] = jnp.concatenate([e / den for e in exps], axis=0)
    topi_ref[...] = jnp.concatenate(idxs, axis=0)

    sel = jnp.zeros((N_EXPERTS, tm), F32)
    for idx in idxs:
        sel = sel + jnp.where(ids == idx, 1.0, 0.0)
    before = jnp.dot(sel.astype(BF16), tri_ref[...], preferred_element_type=F32)
    before = before + carry_ref[:, 0:1]
    ranks = [jnp.sum(jnp.where(ids == idx, before, 0.0), axis=0, keepdims=True) for idx in idxs]
    rank_ref[...] = jnp.concatenate(ranks, axis=0).astype(I32)
    total = carry_ref[...] + jnp.sum(sel, axis=1, keepdims=True)
    carry_ref[...] = total
    cnt_ref[...] = total


def _out_proj(x2, attn, ssm, attn_g, wo_a, wo_s, ffn_g, wrt, b_router, *, tm=512):
    t, d = x2.shape
    tri = jnp.triu(jnp.ones((tm, tm), F32), k=1).astype(BF16)
    kern = functools.partial(_outproj_kernel, tm=tm)
    row = lambda i: (i, 0)
    const = lambda i: (0, 0)
    colblk = lambda i: (0, i)
    return pl.pallas_call(
        kern,
        grid=(t // tm,),
        in_specs=[pl.BlockSpec((tm, d), row),
                  pl.BlockSpec((tm, ATTN_WIDTH), row),
                  pl.BlockSpec((tm, SSM_WIDTH), row),
                  pl.BlockSpec((1, ATTN_WIDTH), const),
                  pl.BlockSpec((ATTN_WIDTH, d), const),
                  pl.BlockSpec((SSM_WIDTH, d), const),
                  pl.BlockSpec((1, d), const),
                  pl.BlockSpec((N_EXPERTS, d), const),
                  pl.BlockSpec((N_EXPERTS, 1), const),
                  pl.BlockSpec((tm, tm), const)],
        out_specs=[pl.BlockSpec((tm, d), row),
                   pl.BlockSpec((tm, ROW_CHUNKS, LANES), lambda i: (i, 0, 0)),
                   pl.BlockSpec((TOP_K, tm), colblk),
                   pl.BlockSpec((TOP_K, tm), colblk),
                   pl.BlockSpec((TOP_K, tm), colblk),
                   pl.BlockSpec((N_EXPERTS, LANES), const)],
        out_shape=[jax.ShapeDtypeStruct((t, d), F32),
                   jax.ShapeDtypeStruct((t, ROW_CHUNKS, LANES), F32),
                   jax.ShapeDtypeStruct((TOP_K, t), I32),
                   jax.ShapeDtypeStruct((TOP_K, t), F32),
                   jax.ShapeDtypeStruct((TOP_K, t), I32),
                   jax.ShapeDtypeStruct((N_EXPERTS, LANES), F32)],
        scratch_shapes=[pltpu.VMEM((N_EXPERTS, LANES), F32)],
        compiler_params=pltpu.CompilerParams(
            dimension_semantics=("arbitrary",), vmem_limit_bytes=VMEM_LIMIT),
        name="out_proj",
    )(x2, attn, ssm, attn_g.reshape(1, -1), wo_a, wo_s, ffn_g.reshape(1, d), wrt,
      b_router.reshape(N_EXPERTS, 1), tri)


def _row_gather_start(src_hbm, idx_ref, buf, sem, slot, rows):
    for r in rows:
        pltpu.make_async_copy(
            src_hbm.at[idx_ref[0, 0, r]],
            buf.at[slot, pl.ds(r * ROW_CHUNKS, ROW_CHUNKS), :],
            sem.at[slot]).start(priority=r % 2)


def _row_gather_wait(buf, sem, slot):
    pltpu.make_async_copy(buf.at[slot], buf.at[slot], sem.at[slot]).wait()


def _issue_anchor(buf, slot):
    return buf[slot, 0:1, 0:1] * 0.0


def _rows_to_matrix(buf, slot, row0, n_rows):
    return jnp.concatenate(
        [buf[slot, pl.ds(row0 * ROW_CHUNKS + c, n_rows, stride=ROW_CHUNKS), :]
         for c in range(ROW_CHUNKS)], axis=-1)


def _expert_kernel(blk_e_ref, nused_ref, tok_cur_ref, tok_nxt_ref, h2_hbm,
                   wgu_ref, bgu_ref, wdn_ref, bdn_ref, y_ref, xbuf, sem):
    del blk_e_ref
    i = pl.program_id(0)
    slot = i % 2
    nused = nused_ref[0]
    d_ff = wdn_ref.shape[0]
    n_col = 4
    wc = d_ff // n_col
    per = MOE_ROWS // (2 * n_col)

    @pl.when(i == 0)
    def _():
        _row_gather_start(h2_hbm, tok_cur_ref, xbuf, sem, 0, range(MOE_ROWS))

    @pl.when(i < nused)
    def _():
        _row_gather_wait(xbuf, sem, slot)
        x = _rows_to_matrix(xbuf, slot, 0, MOE_ROWS).astype(BF16)
        acts = []
        for c in range(n_col):
            g = (jnp.dot(x, wgu_ref[:, c * wc:(c + 1) * wc], preferred_element_type=F32)
                 + bgu_ref[:, c * wc:(c + 1) * wc])
            _row_gather_start(h2_hbm, tok_nxt_ref, xbuf, sem, 1 - slot,
                              range(2 * c * per, (2 * c + 1) * per))
            g = g + _issue_anchor(xbuf, slot)
            u = (jnp.dot(x, wgu_ref[:, d_ff + c * wc:d_ff + (c + 1) * wc],
                         preferred_element_type=F32)
                 + bgu_ref[:, d_ff + c * wc:d_ff + (c + 1) * wc])
            _row_gather_start(h2_hbm, tok_nxt_ref, xbuf, sem, 1 - slot,
                              range((2 * c + 1) * per, (2 * c + 2) * per))
            u = u + _issue_anchor(xbuf, slot)
            gate = jnp.minimum(g, SWIGLU_LIMIT)
            up = jnp.clip(u, -SWIGLU_LIMIT, SWIGLU_LIMIT)
            acts.append((gate * _sigmoid(SWIGLU_ALPHA * gate) * (up + 1.0)).astype(BF16))
        act = jnp.concatenate(acts, axis=-1)
        y = jnp.dot(act, wdn_ref[...], preferred_element_type=F32) + bdn_ref[...]
        for c in range(ROW_CHUNKS):
            y_ref[pl.ds(c, MOE_ROWS, stride=ROW_CHUNKS), :] = y[:, c * LANES:(c + 1) * LANES]

        @pl.when(i == nused - 1)
        def _():
            _row_gather_wait(xbuf, sem, 1 - slot)

    @pl.when(i >= nused)
    def _():
        y_ref[...] = jnp.zeros_like(y_ref)


def _experts(blk_expert, n_used, row_tok3, h2r, w_gu, b_gu, w_dn, b_dn):
    n_blocks = row_tok3.shape[0]
    e, d, f2 = w_gu.shape
    blk_rows = MOE_ROWS * ROW_CHUNKS
    grid_spec = pltpu.PrefetchScalarGridSpec(
        num_scalar_prefetch=2,
        grid=(n_blocks,),
        in_specs=[
            pl.BlockSpec((1, 1, MOE_ROWS), lambda i, be, nu: (i, 0, 0),
                         memory_space=pltpu.SMEM),
            pl.BlockSpec((1, 1, MOE_ROWS),
                         lambda i, be, nu: (jnp.minimum(i + 1, nu[0] - 1), 0, 0),
                         memory_space=pltpu.SMEM),
            pl.BlockSpec(memory_space=pl.ANY),
            pl.BlockSpec((None, d, f2), lambda i, be, nu: (be[i], 0, 0)),
            pl.BlockSpec((None, 1, f2), lambda i, be, nu: (be[i], 0, 0)),
            pl.BlockSpec((None, f2 // 2, d), lambda i, be, nu: (be[i], 0, 0)),
            pl.BlockSpec((None, 1, d), lambda i, be, nu: (be[i], 0, 0)),
        ],
        out_specs=pl.BlockSpec((blk_rows, LANES), lambda i, be, nu: (i, 0)),
        scratch_shapes=[pltpu.VMEM((2, blk_rows, LANES), F32),
                        pltpu.SemaphoreType.DMA((2,))],
    )
    return pl.pallas_call(
        _expert_kernel,
        grid_spec=grid_spec,
        out_shape=jax.ShapeDtypeStruct((n_blocks * blk_rows, LANES), F32),
        compiler_params=pltpu.CompilerParams(
            dimension_semantics=("arbitrary",), vmem_limit_bytes=VMEM_LIMIT),
        name="experts",
    )(blk_expert, n_used, row_tok3, row_tok3, h2r, w_gu, b_gu.reshape(e, 1, f2),
      w_dn, b_dn.reshape(e, 1, d))


def _combine_kernel(dst_cur_ref, dst_nxt_ref, y_hbm, x1_ref, gate_ref, p_ref, gp_ref,
                    wg_ref, wp_ref, gfin_ref, o_ref, ybuf, sem, *, tm, n_tiles):
    i = pl.program_id(0)
    slot = i % 2
    n_rows = TOP_K * tm
    d = x1_ref.shape[1]
    n_col = 4
    wc = d // n_col
    per = n_rows // (TOP_K + n_col)

    def start_next(part):
        _row_gather_start(y_hbm, dst_nxt_ref, ybuf, sem, 1 - slot,
                          range(part * per, (part + 1) * per))

    @pl.when(i == 0)
    def _():
        _row_gather_start(y_hbm, dst_cur_ref, ybuf, sem, 0, range(n_rows))

    _row_gather_wait(ybuf, sem, slot)
    gates = gate_ref[...]
    x2 = x1_ref[...]
    for k in range(TOP_K):
        x2 = x2 + gates[:, k:k + 1] * _rows_to_matrix(ybuf, slot, k * tm, tm)
        start_next(k)
    hg = _rms(x2, gp_ref[...]).astype(BF16)
    pb = p_ref[...].astype(BF16)
    x3 = []
    for c in range(n_col):
        cols = slice(c * wc, (c + 1) * wc)
        gate = _sigmoid(jnp.dot(hg, wg_ref[:, cols], preferred_element_type=F32))
        emb = jnp.dot(pb, wp_ref[:, cols], preferred_element_type=F32)
        start_next(TOP_K + c)
        x3.append(x2[:, cols] + gate * emb + _issue_anchor(ybuf, slot))
    o_ref[...] = _rms(jnp.concatenate(x3, axis=-1), gfin_ref[...])

    @pl.when(i == n_tiles - 1)
    def _():
        _row_gather_wait(ybuf, sem, 1 - slot)


def _combine(dest3, y, x1, gates_tk, p2, ple_g, w_gate, w_proj, fin_g, *, tm=256):
    t, d = x1.shape
    n_tiles = t // tm
    last = n_tiles - 1
    n_rows = TOP_K * tm
    ple = p2.shape[1]
    kern = functools.partial(_combine_kernel, tm=tm, n_tiles=n_tiles)
    row = lambda i: (i, 0)
    const = lambda i: (0, 0)
    return pl.pallas_call(
        kern,
        grid=(n_tiles,),
        in_specs=[pl.BlockSpec((1, 1, n_rows), lambda i: (i, 0, 0), memory_space=pltpu.SMEM),
                  pl.BlockSpec((1, 1, n_rows), lambda i: (jnp.minimum(i + 1, last), 0, 0),
                               memory_space=pltpu.SMEM),
                  pl.BlockSpec(memory_space=pl.ANY),
                  pl.BlockSpec((tm, d), row),
                  pl.BlockSpec((tm, TOP_K), row),
                  pl.BlockSpec((tm, ple), row),
                  pl.BlockSpec((1, d), const),
                  pl.BlockSpec((d, d), const),
                  pl.BlockSpec((ple, d), const),
                  pl.BlockSpec((1, d), const)],
        out_specs=pl.BlockSpec((tm, d), row),
        out_shape=jax.ShapeDtypeStruct((t, d), F32),
        scratch_shapes=[pltpu.VMEM((2, n_rows * ROW_CHUNKS, LANES), F32),
                        pltpu.SemaphoreType.DMA((2,))],
        compiler_params=pltpu.CompilerParams(
            dimension_semantics=("arbitrary",), vmem_limit_bytes=VMEM_LIMIT),
        name="combine",
    )(dest3, dest3, y.reshape(-1, ROW_CHUNKS, LANES), x1, gates_tk, p2, ple_g.reshape(1, d),
      w_gate, w_proj, fin_g.reshape(1, d))


def _routing_tables(topi, rank, counts, *, tm_combine):
    k, t = topi.shape
    n_blocks = (k * t) // MOE_ROWS + N_EXPERTS
    padded = (counts + MOE_ROWS - 1) // MOE_ROWS * MOE_ROWS
    pad_end = jnp.cumsum(padded)
    pad_start = pad_end - padded
    eids = jnp.arange(N_EXPERTS, dtype=I32)[:, None, None]
    dest = rank + jnp.sum(jnp.where(topi[None] == eids, pad_start[:, None, None], 0), axis=0)
    n_used = (pad_end[-1] // MOE_ROWS).astype(I32).reshape(1)
    blk_row0 = jnp.arange(n_blocks, dtype=I32) * MOE_ROWS
    blk_expert = jnp.minimum(
        jnp.sum((pad_end[None, :] <= blk_row0[:, None]).astype(I32), axis=1), N_EXPERTS - 1)
    tok = jnp.broadcast_to(jnp.arange(t, dtype=I32)[None, :], (k, t))
    row_tok = jnp.zeros((n_blocks * MOE_ROWS,), I32).at[dest.reshape(-1)].set(tok.reshape(-1))
    row_tok3 = row_tok.reshape(n_blocks, 1, MOE_ROWS)
    n_tiles = t // tm_combine
    dest3 = dest.reshape(k, n_tiles, tm_combine).transpose(1, 0, 2).reshape(
        n_tiles, 1, k * tm_combine)
    return blk_expert, n_used, row_tok3, dest3


def _layer(x2, p2, seq, norm_mix_g, w_in, b_f, lam_re, lam_im, log_dt, b_re, b_im, c_re, c_im,
           d_skip, w_glu, b_glu, attn_out_g, ssm_out_g, w_out, norm_ffn_g, w_router, b_router,
           w_gu, b_gu, w_dn, b_dn, norm_ple_g, w_ple_gate, w_ple_proj, final_g):
    t, d = x2.shape
    nb = t // seq
    aw = ATTN_WIDTH
    w_main = jnp.concatenate([w_in[:, :3 * aw], w_in[:, 3 * aw + N_HEADS:]], axis=1).astype(BF16)
    wft = w_in[:, 3 * aw:3 * aw + N_HEADS].T.astype(BF16)

    a_re, a_im, bbr, bbi = _ssm_prep(lam_re, lam_im, log_dt, b_re, b_im)
    bre = _block_diag(bbr, True).astype(BF16)
    bim = _block_diag(bbi, True).astype(BF16)
    cre = _block_diag(jnp.transpose(c_re, (0, 2, 1)), False).astype(BF16)
    cim = _block_diag(jnp.transpose(c_im, (0, 2, 1)), False).astype(BF16)

    q, k, v, u, c_t = _in_proj(x2, norm_mix_g, w_main, wft, b_f, seq=seq)
    shp = (nb, seq, aw)
    c3 = c_t.reshape(N_HEADS // 2, 2, t)
    attn = _attention(q.reshape(shp), k.reshape(shp), v.reshape(shp), c3)
    ssm = _ssm(u.reshape(nb, seq, SSM_WIDTH), bre, bim, cre, cim, a_re, a_im,
               d_skip, w_glu.astype(BF16), b_glu, ssm_out_g)

    x1, h2r, topi, gates, rank, cnt = _out_proj(
        x2, attn.reshape(t, aw), ssm.reshape(t, SSM_WIDTH), attn_out_g,
        w_out[:aw].astype(BF16), w_out[aw:].astype(BF16), norm_ffn_g,
        w_router.T.astype(BF16), b_router)

    tm_combine = 256
    counts = cnt[:, 0].astype(I32)
    blk_expert, n_used, row_tok3, dest3 = _routing_tables(topi, rank, counts,
                                                          tm_combine=tm_combine)
    y = _experts(blk_expert, n_used, row_tok3, h2r, w_gu.astype(BF16), b_gu,
                 w_dn.astype(BF16), b_dn)
    return _combine(dest3, y, x1, gates.T, p2, norm_ple_g, w_ple_gate.astype(BF16),
                    w_ple_proj.astype(BF16), final_g, tm=tm_combine)


def kernel(x, p, norm_mix_g, w_in, b_f, lam_re, lam_im, log_dt, b_re, b_im, c_re, c_im, d_skip, w_glu, b_glu, attn_out_g, ssm_out_g, w_out, norm_ffn_g, w_router, b_router, w_gu, b_gu, w_dn, b_dn, norm_ple_g, w_ple_gate, w_ple_proj, norm_final_g):
    bsz, seq, d = x.shape
    depth = w_in.shape[0]
    assert depth == 1, "one layer: the final rmsnorm is fused into the layer's last kernel"
    out = _layer(x.reshape(bsz * seq, d), p[0].reshape(bsz * seq, -1), seq,
                 norm_mix_g[0], w_in[0], b_f[0], lam_re[0], lam_im[0], log_dt[0],
                 b_re[0], b_im[0], c_re[0], c_im[0], d_skip[0], w_glu[0], b_glu[0],
                 attn_out_g[0], ssm_out_g[0], w_out[0], norm_ffn_g[0], w_router[0],
                 b_router[0], w_gu[0], b_gu[0], w_dn[0], b_dn[0], norm_ple_g[0],
                 w_ple_gate[0], w_ple_proj[0], norm_final_g)
    return out.reshape(bsz, seq, d)
```

```python
import functools
import math

import jax
import jax.numpy as jnp
from jax import lax
from jax.experimental import pallas as pl
from jax.experimental.pallas import tpu as pltpu

F32 = jnp.float32
BF16 = jnp.bfloat16
I32 = jnp.int32

NORM_EPS = 1e-5
HEAD_DIM = 64
N_HEADS = 8
ATTN_WIDTH = 512
SSM_WIDTH = 512
SSM_GROUP = 16
N_SSM_GROUPS = 32
SSM_STATE = 64
N_STATE = N_SSM_GROUPS * SSM_STATE
N_EXPERTS = 32
TOP_K = 4
SWIGLU_LIMIT = 7.0
SWIGLU_ALPHA = 1.702
LANES = 128
SUBLANES = 8
ROW_CHUNKS = 8
MOE_ROWS = 256
NEG_BIG = -1e30
LOG2E = math.log2(math.e)
VMEM_LIMIT = 56 * 1024 * 1024

_NT = (((1,), (1,)), ((), ()))


def _rms(xf, g):
    ms = jnp.mean(xf * xf, axis=-1, keepdims=True)
    return xf * lax.rsqrt(ms + NORM_EPS) * g


def _sigmoid(x):
    return 1.0 / (1.0 + jnp.exp(-x))


def _ssm_prep_kernel(lr_ref, li_ref, ldt_ref, brt_ref, bit_ref,
                     ar_ref, ai_ref, bbr_ref, bbi_ref):
    lr = lr_ref[...]
    li = li_ref[...]
    dt = jnp.exp(ldt_ref[...])
    mag = jnp.exp(lr * dt)
    ab_re = mag * jnp.cos(li * dt)
    ab_im = mag * jnp.sin(li * dt)
    den = lr * lr + li * li
    nr = ab_re - 1.0
    z_re = (nr * lr + ab_im * li) / den
    z_im = (ab_im * lr - nr * li) / den
    ar_ref[...] = ab_re
    ai_ref[...] = ab_im
    br = brt_ref[...]
    bi = bit_ref[...]
    bbr_ref[...] = z_re * br - z_im * bi
    bbi_ref[...] = z_re * bi + z_im * br


def _ssm_prep(lam_re, lam_im, log_dt, b_re, b_im):
    g, p, c = b_re.shape
    brt = jnp.transpose(b_re, (0, 2, 1))
    bit = jnp.transpose(b_im, (0, 2, 1))
    return pl.pallas_call(
        _ssm_prep_kernel,
        out_shape=(jax.ShapeDtypeStruct((g, 1, p), F32), jax.ShapeDtypeStruct((g, 1, p), F32),
                   jax.ShapeDtypeStruct((g, c, p), F32), jax.ShapeDtypeStruct((g, c, p), F32)),
        name="ssm_prep",
    )(lam_re.reshape(g, 1, p), lam_im.reshape(g, 1, p), log_dt.reshape(g, 1, 1), brt, bit)


def _block_diag(w, rows_first):
    g, a, b = w.shape
    half = g // 2
    eye = jnp.eye(half, dtype=w.dtype)
    w4 = w.reshape(2, half, a, b)
    out = jnp.einsum('hgab,gk->hgakb', w4, eye)
    del rows_first
    return out.reshape(2, half * a, half * b)


def _inproj_kernel(x_ref, g_ref, w_ref, wft_ref, bf_ref, tri_ref,
                   q_ref, k_ref, v_ref, u_ref, c_ref, carry_ref, *, tiles_per_seq, tm):
    i = pl.program_id(0)

    @pl.when(i % tiles_per_seq == 0)
    def _():
        carry_ref[...] = jnp.zeros_like(carry_ref)

    h = _rms(x_ref[...], g_ref[...]).astype(BF16)
    proj = jnp.dot(h, w_ref[...], preferred_element_type=F32)
    aw = ATTN_WIDTH
    q_ref[...] = (proj[:, 0:aw] * (LOG2E * HEAD_DIM ** -0.5)).astype(BF16)
    k_ref[...] = proj[:, aw:2 * aw].astype(BF16)
    v_ref[...] = proj[:, 2 * aw:3 * aw].astype(BF16)
    u_ref[...] = proj[:, 3 * aw:3 * aw + SSM_WIDTH].astype(BF16)

    fl = lax.dot_general(wft_ref[...], h, _NT, preferred_element_type=F32)
    z = fl + bf_ref[...]
    lf = jnp.minimum(z, 0.0) - jnp.log1p(jnp.exp(-jnp.abs(z)))
    hi = lf.astype(BF16)
    lo = (lf - hi.astype(F32)).astype(BF16)
    tri = tri_ref[...]
    cs = (jnp.dot(hi, tri, preferred_element_type=F32)
          + jnp.dot(lo, tri, preferred_element_type=F32))
    c = cs + carry_ref[:, 0:1]
    c_ref[...] = c * LOG2E
    carry_ref[...] = jnp.broadcast_to(c[:, tm - 1:tm], carry_ref.shape)


def _in_proj(x2, norm_g, w_main, wft, b_f, *, seq, tm=512):
    t, d = x2.shape
    n_main = w_main.shape[1]
    tri = jnp.triu(jnp.ones((tm, tm), F32)).astype(BF16)
    kern = functools.partial(_inproj_kernel, tiles_per_seq=seq // tm, tm=tm)
    row = lambda i: (i, 0)
    const = lambda i: (0, 0)
    act = jax.ShapeDtypeStruct((t, ATTN_WIDTH), BF16)
    return pl.pallas_call(
        kern,
        grid=(t // tm,),
        in_specs=[pl.BlockSpec((tm, d), row),
                  pl.BlockSpec((1, d), const),
                  pl.BlockSpec((d, n_main), const),
                  pl.BlockSpec((N_HEADS, d), const),
                  pl.BlockSpec((N_HEADS, 1), const),
                  pl.BlockSpec((tm, tm), const)],
        out_specs=[pl.BlockSpec((tm, ATTN_WIDTH), row)] * 4
        + [pl.BlockSpec((N_HEADS, tm), lambda i: (0, i))],
        out_shape=[act, act, act, act, jax.ShapeDtypeStruct((N_HEADS, t), F32)],
        scratch_shapes=[pltpu.VMEM((N_HEADS, LANES), F32)],
        compiler_params=pltpu.CompilerParams(
            dimension_semantics=("arbitrary",), vmem_limit_bytes=VMEM_LIMIT),
        name="in_proj",
    )(x2, norm_g.reshape(1, d), w_main, wft, b_f.reshape(N_HEADS, 1), tri)


def _attn_kernel(q_ref, k_ref, v_ref, c_ref, o_ref, *, tq):
    i = pl.program_id(2)
    q2 = q_ref[...]
    lane = lax.broadcasted_iota(I32, (1, LANES), 1)
    first = lane < HEAD_DIM
    zero = jnp.zeros_like(q2)
    q_heads = (jnp.where(first, q2, zero), jnp.where(first, zero, q2))

    def block(j, carry, masked):
        off = pl.multiple_of(j * tq, tq)
        kj = k_ref[pl.ds(off, tq), :]
        vj = v_ref[pl.ds(off, tq), :]
        cj = c_ref[:, pl.ds(off, tq)]
        one = jnp.ones_like(vj)
        v_heads = (jnp.where(first, vj, one), jnp.where(first, one, vj))
        out = []
        for h in range(2):
            m, acc = carry[h]
            s = (lax.dot_general(q_heads[h], kj, _NT, preferred_element_type=F32)
                 - cj[h:h + 1, :])
            if masked:
                rr = lax.broadcasted_iota(I32, (tq, tq), 0)
                cc = lax.broadcasted_iota(I32, (tq, tq), 1)
                s = jnp.where(cc <= rr, s, NEG_BIG)
            m_new = jnp.maximum(m, jnp.max(s, axis=-1, keepdims=True))
            alpha = jnp.exp2(m - m_new)
            p = jnp.exp2(s - m_new).astype(BF16)
            acc = alpha * acc + jnp.dot(p, v_heads[h], preferred_element_type=F32)
            out.append((m_new, acc))
        return tuple(out)

    init_one = (jnp.full((tq, 1), NEG_BIG, F32), jnp.zeros((tq, LANES), F32))
    carry = lax.fori_loop(0, i, lambda j, c: block(j, c, False), (init_one, init_one))
    (_, acc_a), (_, acc_b) = block(i, carry, True)
    o = jnp.where(first, acc_a / pltpu.roll(acc_a, HEAD_DIM, axis=1),
                  acc_b / pltpu.roll(acc_b, HEAD_DIM, axis=1))
    o_ref[...] = o.astype(BF16)


def _attention(q, k, v, c3, *, tq=1024):
    b, s, w = q.shape
    n_pairs = w // LANES
    kern = functools.partial(_attn_kernel, tq=tq)
    return pl.pallas_call(
        kern,
        grid=(b, n_pairs, s // tq),
        in_specs=[pl.BlockSpec((None, tq, LANES), lambda bi, hp, i: (bi, i, hp)),
                  pl.BlockSpec((None, s, LANES), lambda bi, hp, i: (bi, 0, hp)),
                  pl.BlockSpec((None, s, LANES), lambda bi, hp, i: (bi, 0, hp)),
                  pl.BlockSpec((None, 2, s), lambda bi, hp, i: (hp, 0, bi))],
        out_specs=pl.BlockSpec((None, tq, LANES), lambda bi, hp, i: (bi, i, hp)),
        out_shape=jax.ShapeDtypeStruct((b, s, w), BF16),
        compiler_params=pltpu.CompilerParams(
            dimension_semantics=("arbitrary", "arbitrary", "arbitrary"),
            vmem_limit_bytes=VMEM_LIMIT),
        name="attention",
    )(q, k, v, c3)


def _ssm_kernel(u_ref, bre_ref, bim_ref, cre_ref, cim_ref, ar_ref, ai_ref, dsk_ref,
                wglu_ref, bglu_ref, g_ref, o_ref,
                us_ref, xr_ref, xi_ref, str_ref, sti_ref, res_ref, *, tt, nb):
    i = pl.program_id(0)

    @pl.when(i == 0)
    def _():
        str_ref[...] = jnp.zeros_like(str_ref)
        sti_ref[...] = jnp.zeros_like(sti_ref)

    n_chunks = SSM_WIDTH // LANES
    for b in range(nb):
        ub32 = u_ref[b].astype(F32)
        for c in range(n_chunks):
            us_ref[c, pl.ds(b, tt, stride=nb), :] = ub32[:, c * LANES:(c + 1) * LANES]
    uf = jnp.concatenate([us_ref[c] for c in range(n_chunks)], axis=-1)
    ub = uf.astype(BF16)
    half_in = SSM_WIDTH // 2
    half_st = N_STATE // 2
    for hf in range(2):
        uh = ub[:, hf * half_in:(hf + 1) * half_in]
        xr_ref[:, hf * half_st:(hf + 1) * half_st] = jnp.dot(
            uh, bre_ref[hf], preferred_element_type=F32)
        xi_ref[:, hf * half_st:(hf + 1) * half_st] = jnp.dot(
            uh, bim_ref[hf], preferred_element_type=F32)

    n_col_groups = 2
    wcol = N_STATE // n_col_groups
    unroll = 4
    for cg in range(n_col_groups):
        cols = slice(cg * wcol, (cg + 1) * wcol)
        ar = jnp.broadcast_to(ar_ref[:, cols], (nb, wcol))
        ai = jnp.broadcast_to(ai_ref[:, cols], (nb, wcol))

        def steps(tb, carry, cols=cols, ar=ar, ai=ai):
            sr, si = carry
            for k in range(unroll):
                r0 = pl.multiple_of((tb * unroll + k) * nb, nb)
                br = xr_ref[pl.ds(r0, nb), cols]
                bi = xi_ref[pl.ds(r0, nb), cols]
                nr = ar * sr - ai * si + br
                ni = ar * si + ai * sr + bi
                xr_ref[pl.ds(r0, nb), cols] = nr
                xi_ref[pl.ds(r0, nb), cols] = ni
                sr, si = nr, ni
            return sr, si

        sr, si = lax.fori_loop(0, tt // unroll, steps, (str_ref[:, cols], sti_ref[:, cols]))
        str_ref[:, cols] = sr
        sti_ref[:, cols] = si

    ys = []
    for hf in range(2):
        xr = xr_ref[:, hf * half_st:(hf + 1) * half_st].astype(BF16)
        xi = xi_ref[:, hf * half_st:(hf + 1) * half_st].astype(BF16)
        ys.append(jnp.dot(xr, cre_ref[hf], preferred_element_type=F32)
                  - jnp.dot(xi, cim_ref[hf], preferred_element_type=F32))
    y = jnp.concatenate(ys, axis=-1) + dsk_ref[...] * uf
    gl = 0.5 * y * (1.0 + jnp.tanh(math.sqrt(2.0 / math.pi) * (y + 0.044715 * (y * y * y))))
    zz = jnp.dot(gl.astype(BF16), wglu_ref[...], preferred_element_type=F32) + bglu_ref[...]
    out = gl * _sigmoid(zz)
    res = _rms(out, g_ref[...])
    for c in range(n_chunks):
        res_ref[c] = res[:, c * LANES:(c + 1) * LANES]
    for b in range(nb):
        o_ref[b] = jnp.concatenate(
            [res_ref[c, pl.ds(b, tt, stride=nb), :] for c in range(n_chunks)],
            axis=-1).astype(BF16)


def _ssm(u3, bre, bim, cre, cim, a_re, a_im, d_skip, w_glu, b_glu, out_g, *, tt=64):
    nb, s, w = u3.shape
    rows = tt * nb
    kern = functools.partial(_ssm_kernel, tt=tt, nb=nb)
    c3 = lambda i: (0, 0, 0)
    c2 = lambda i: (0, 0)
    return pl.pallas_call(
        kern,
        grid=(s // tt,),
        in_specs=[pl.BlockSpec((nb, tt, w), lambda i: (0, i, 0)),
                  pl.BlockSpec(bre.shape, c3), pl.BlockSpec(bim.shape, c3),
                  pl.BlockSpec(cre.shape, c3), pl.BlockSpec(cim.shape, c3),
                  pl.BlockSpec((1, N_STATE), c2), pl.BlockSpec((1, N_STATE), c2),
                  pl.BlockSpec((1, w), c2),
                  pl.BlockSpec((w, w), c2), pl.BlockSpec((1, w), c2), pl.BlockSpec((1, w), c2)],
        out_specs=pl.BlockSpec((nb, tt, w), lambda i: (0, i, 0)),
        out_shape=jax.ShapeDtypeStruct((nb, s, w), BF16),
        scratch_shapes=[pltpu.VMEM((w // LANES, rows, LANES), F32),
                        pltpu.VMEM((rows, N_STATE), F32), pltpu.VMEM((rows, N_STATE), F32),
                        pltpu.VMEM((nb, N_STATE), F32), pltpu.VMEM((nb, N_STATE), F32),
                        pltpu.VMEM((w // LANES, rows, LANES), F32)],
        compiler_params=pltpu.CompilerParams(
            dimension_semantics=("arbitrary",), vmem_limit_bytes=VMEM_LIMIT),
        name="ssm",
    )(u3, bre, bim, cre, cim, a_re.reshape(1, N_STATE), a_im.reshape(1, N_STATE),
      d_skip.reshape(1, w), w_glu, b_glu.reshape(1, w), out_g.reshape(1, w))


def _outproj_kernel(x_ref, a_ref, s_ref, ga_ref, woa_ref, wos_ref, gf_ref, wrt_ref, br_ref,
                    tri_ref, x1_ref, h2_ref, topi_ref, gate_ref, rank_ref, cnt_ref, *, tm):
    a = _rms(a_ref[...].astype(F32), ga_ref[...]).astype(BF16)
    x1 = (x_ref[...] + jnp.dot(a, woa_ref[...], preferred_element_type=F32)
          + jnp.dot(s_ref[...], wos_ref[...], preferred_element_type=F32))
    x1_ref[...] = x1
    h2 = _rms(x1, gf_ref[...]).astype(BF16)
    h2_ref[...] = h2

    lg = lax.dot_general(wrt_ref[...], h2, _NT,
                         preferred_element_type=F32) + br_ref[...]
    ids = lax.broadcasted_iota(I32, (N_EXPERTS, tm), 0)
    work = lg
    vals, idxs = [], []
    for _ in range(TOP_K):
        m = jnp.max(work, axis=0, keepdims=True)
        idx = jnp.min(jnp.where(work == m, ids, N_EXPERTS), axis=0, keepdims=True)
        vals.append(m)
        idxs.append(idx)
        work = jnp.where(ids == idx, -jnp.inf, work)
    exps = [jnp.exp(v - vals[0]) for v in vals]
    den = exps[0] + exps[1] + exps[2] + exps[3]
    gate_ref[...] = jnp.concatenate([e / den for e in exps], axis=0)
    topi_ref[...] = jnp.concatenate(idxs, axis=0)

    sel = jnp.zeros((N_EXPERTS, tm), F32)
    for idx in idxs:
        sel = sel + jnp.where(ids == idx, 1.0, 0.0)
    before = jnp.dot(sel.astype(BF16), tri_ref[...], preferred_element_type=F32)
    ranks = [jnp.sum(jnp.where(ids == idx, before, 0.0), axis=0, keepdims=True) for idx in idxs]
    rank_ref[...] = jnp.concatenate(ranks, axis=0).astype(I32)
    cnt_ref[0] = jnp.broadcast_to(jnp.sum(sel, axis=1, keepdims=True), (N_EXPERTS, LANES))


def _out_proj(x2, attn, ssm, attn_g, wo_a, wo_s, ffn_g, wrt, b_router, *, tm=512):
    t, d = x2.shape
    tri = jnp.triu(jnp.ones((tm, tm), F32), k=1).astype(BF16)
    kern = functools.partial(_outproj_kernel, tm=tm)
    row = lambda i: (i, 0)
    const = lambda i: (0, 0)
    colblk = lambda i: (0, i)
    return pl.pallas_call(
        kern,
        grid=(t // tm,),
        in_specs=[pl.BlockSpec((tm, d), row),
                  pl.BlockSpec((tm, ATTN_WIDTH), row),
                  pl.BlockSpec((tm, SSM_WIDTH), row),
                  pl.BlockSpec((1, ATTN_WIDTH), const),
                  pl.BlockSpec((ATTN_WIDTH, d), const),
                  pl.BlockSpec((SSM_WIDTH, d), const),
                  pl.BlockSpec((1, d), const),
                  pl.BlockSpec((N_EXPERTS, d), const),
                  pl.BlockSpec((N_EXPERTS, 1), const),
                  pl.BlockSpec((tm, tm), const)],
        out_specs=[pl.BlockSpec((tm, d), row),
                   pl.BlockSpec((tm, d), row),
                   pl.BlockSpec((TOP_K, tm), colblk),
                   pl.BlockSpec((TOP_K, tm), colblk),
                   pl.BlockSpec((TOP_K, tm), colblk),
                   pl.BlockSpec((1, N_EXPERTS, LANES), lambda i: (i, 0, 0))],
        out_shape=[jax.ShapeDtypeStruct((t, d), F32),
                   jax.ShapeDtypeStruct((t, d), BF16),
                   jax.ShapeDtypeStruct((TOP_K, t), I32),
                   jax.ShapeDtypeStruct((TOP_K, t), F32),
                   jax.ShapeDtypeStruct((TOP_K, t), I32),
                   jax.ShapeDtypeStruct((t // tm, N_EXPERTS, LANES), F32)],
        compiler_params=pltpu.CompilerParams(
            dimension_semantics=("arbitrary",), vmem_limit_bytes=VMEM_LIMIT),
        name="out_proj",
    )(x2, attn, ssm, attn_g.reshape(1, -1), wo_a, wo_s, ffn_g.reshape(1, d), wrt,
      b_router.reshape(N_EXPERTS, 1), tri)


def _row_gather_start(src_hbm, idx_ref, buf, sem, slot, rows):
    for r in rows:
        if r % 16:
            continue
        t0 = jnp.minimum(idx_ref[0, 0, r], src_hbm.shape[0] - 16)
        pltpu.make_async_copy(
            src_hbm.at[pl.ds(t0, 16)].reshape(16 * ROW_CHUNKS, LANES),
            buf.at[slot, pl.ds(r * ROW_CHUNKS, 16 * ROW_CHUNKS), :],
            sem.at[slot]).start(priority=(r // 16) % 2)


def _row_gather_wait(buf, sem, slot):
    pltpu.make_async_copy(buf.at[slot], buf.at[slot], sem.at[slot]).wait()


def _issue_anchor(buf, slot):
    return buf[slot, 0:1, 0:1] * 0.0


def _rows_to_matrix(buf, slot, row0, n_rows):
    return jnp.concatenate(
        [buf[slot, pl.ds(row0 * ROW_CHUNKS + c, n_rows, stride=ROW_CHUNKS), :]
         for c in range(ROW_CHUNKS)], axis=-1)


def _expert_kernel(blk_e_ref, nused_ref, tok_cur_ref, tok_nxt_ref, h2_hbm,
                   wgu_ref, bgu_ref, wdn_ref, bdn_ref, y_ref, xbuf, sem):
    del blk_e_ref
    i = pl.program_id(0)
    slot = i % 2
    nused = nused_ref[0]
    d_ff = wdn_ref.shape[0]
    n_col = 4
    wc = d_ff // n_col
    per = MOE_ROWS // (2 * n_col)

    @pl.when(i == 0)
    def _():
        _row_gather_start(h2_hbm, tok_cur_ref, xbuf, sem, 0, range(MOE_ROWS))

    @pl.when(i < nused)
    def _():
        _row_gather_wait(xbuf, sem, slot)
        x = _rows_to_matrix(xbuf, slot, 0, MOE_ROWS).astype(BF16)
        acts = []
        for c in range(n_col):
            g = (jnp.dot(x, wgu_ref[:, c * wc:(c + 1) * wc], preferred_element_type=F32)
                 + bgu_ref[:, c * wc:(c + 1) * wc])
            _row_gather_start(h2_hbm, tok_nxt_ref, xbuf, sem, 1 - slot,
                              range(2 * c * per, (2 * c + 1) * per))
            g = g + _issue_anchor(xbuf, slot)
            u = (jnp.dot(x, wgu_ref[:, d_ff + c * wc:d_ff + (c + 1) * wc],
                         preferred_element_type=F32)
                 + bgu_ref[:, d_ff + c * wc:d_ff + (c + 1) * wc])
            _row_gather_start(h2_hbm, tok_nxt_ref, xbuf, sem, 1 - slot,
                              range((2 * c + 1) * per, (2 * c + 2) * per))
            u = u + _issue_anchor(xbuf, slot)
            gate = jnp.minimum(g, SWIGLU_LIMIT)
            up = jnp.clip(u, -SWIGLU_LIMIT, SWIGLU_LIMIT)
            acts.append((gate * _sigmoid(SWIGLU_ALPHA * gate) * (up + 1.0)).astype(BF16))
        act = jnp.concatenate(acts, axis=-1)
        y = jnp.dot(act, wdn_ref[...], preferred_element_type=F32) + bdn_ref[...]
        for c in range(ROW_CHUNKS):
            y_ref[pl.ds(c, MOE_ROWS, stride=ROW_CHUNKS), :] = y[:, c * LANES:(c + 1) * LANES]

        @pl.when(i == nused - 1)
        def _():
            _row_gather_wait(xbuf, sem, 1 - slot)

    @pl.when(i >= nused)
    def _():
        y_ref[...] = jnp.zeros_like(y_ref)


def _experts(blk_expert, n_used, row_tok3, h2r, w_gu, b_gu, w_dn, b_dn):
    n_blocks = row_tok3.shape[0]
    e, d, f2 = w_gu.shape
    blk_rows = MOE_ROWS * ROW_CHUNKS
    grid_spec = pltpu.PrefetchScalarGridSpec(
        num_scalar_prefetch=2,
        grid=(n_blocks,),
        in_specs=[
            pl.BlockSpec((1, 1, MOE_ROWS), lambda i, be, nu: (i, 0, 0),
                         memory_space=pltpu.SMEM),
            pl.BlockSpec((1, 1, MOE_ROWS),
                         lambda i, be, nu: (jnp.minimum(i + 1, nu[0] - 1), 0, 0),
                         memory_space=pltpu.SMEM),
            pl.BlockSpec(memory_space=pl.ANY),
            pl.BlockSpec((None, d, f2), lambda i, be, nu: (be[i], 0, 0)),
            pl.BlockSpec((None, 1, f2), lambda i, be, nu: (be[i], 0, 0)),
            pl.BlockSpec((None, f2 // 2, d), lambda i, be, nu: (be[i], 0, 0)),
            pl.BlockSpec((None, 1, d), lambda i, be, nu: (be[i], 0, 0)),
        ],
        out_specs=pl.BlockSpec((blk_rows, LANES), lambda i, be, nu: (i, 0)),
        scratch_shapes=[pltpu.VMEM((2, blk_rows, LANES), F32),
                        pltpu.SemaphoreType.DMA((2,))],
    )
    return pl.pallas_call(
        _expert_kernel,
        grid_spec=grid_spec,
        out_shape=jax.ShapeDtypeStruct((n_blocks * blk_rows, LANES), F32),
        compiler_params=pltpu.CompilerParams(
            dimension_semantics=("arbitrary",), vmem_limit_bytes=VMEM_LIMIT),
        name="experts",
    )(blk_expert, n_used, row_tok3, row_tok3, h2r, w_gu, b_gu.reshape(e, 1, f2),
      w_dn, b_dn.reshape(e, 1, d))


def _combine_kernel(dst_cur_ref, dst_nxt_ref, y_hbm, x1_ref, gate_ref, p_ref, gp_ref,
                    wg_ref, wp_ref, gfin_ref, o_ref, ybuf, sem, *, tm, n_tiles):
    i = pl.program_id(0)
    slot = i % 2
    n_rows = TOP_K * tm
    d = x1_ref.shape[1]
    n_col = 4
    wc = d // n_col
    per = n_rows // (TOP_K + n_col)

    def start_next(part):
        _row_gather_start(y_hbm, dst_nxt_ref, ybuf, sem, 1 - slot,
                          range(part * per, (part + 1) * per))

    @pl.when(i == 0)
    def _():
        _row_gather_start(y_hbm, dst_cur_ref, ybuf, sem, 0, range(n_rows))

    _row_gather_wait(ybuf, sem, slot)
    gates = gate_ref[...]
    x2 = x1_ref[...]
    for k in range(TOP_K):
        x2 = x2 + gates[:, k:k + 1] * _rows_to_matrix(ybuf, slot, k * tm, tm)
        start_next(k)
    hg = _rms(x2, gp_ref[...]).astype(BF16)
    pb = p_ref[...].astype(BF16)
    x3 = []
    for c in range(n_col):
        cols = slice(c * wc, (c + 1) * wc)
        gate = _sigmoid(jnp.dot(hg, wg_ref[:, cols], preferred_element_type=F32))
        emb = jnp.dot(pb, wp_ref[:, cols], preferred_element_type=F32)
        start_next(TOP_K + c)
        x3.append(x2[:, cols] + gate * emb + _issue_anchor(ybuf, slot))
    o_ref[...] = _rms(jnp.concatenate(x3, axis=-1), gfin_ref[...])

    @pl.when(i == n_tiles - 1)
    def _():
        _row_gather_wait(ybuf, sem, 1 - slot)


def _combine(dest3, y, x1, gates_tk, p2, ple_g, w_gate, w_proj, fin_g, *, tm=256):
    t, d = x1.shape
    n_tiles = t // tm
    last = n_tiles - 1
    n_rows = TOP_K * tm
    ple = p2.shape[1]
    kern = functools.partial(_combine_kernel, tm=tm, n_tiles=n_tiles)
    row = lambda i: (i, 0)
    const = lambda i: (0, 0)
    return pl.pallas_call(
        kern,
        grid=(n_tiles,),
        in_specs=[pl.BlockSpec((1, 1, n_rows), lambda i: (i, 0, 0), memory_space=pltpu.SMEM),
                  pl.BlockSpec((1, 1, n_rows), lambda i: (jnp.minimum(i + 1, last), 0, 0),
                               memory_space=pltpu.SMEM),
                  pl.BlockSpec(memory_space=pl.ANY),
                  pl.BlockSpec((tm, d), row),
                  pl.BlockSpec((tm, TOP_K), row),
                  pl.BlockSpec((tm, ple), row),
                  pl.BlockSpec((1, d), const),
                  pl.BlockSpec((d, d), const),
                  pl.BlockSpec((ple, d), const),
                  pl.BlockSpec((1, d), const)],
        out_specs=pl.BlockSpec((tm, d), row),
        out_shape=jax.ShapeDtypeStruct((t, d), F32),
        scratch_shapes=[pltpu.VMEM((2, n_rows * ROW_CHUNKS, LANES), F32),
                        pltpu.SemaphoreType.DMA((2,))],
        compiler_params=pltpu.CompilerParams(
            dimension_semantics=("arbitrary",), vmem_limit_bytes=VMEM_LIMIT),
        name="combine",
    )(dest3, dest3, y.reshape(-1, ROW_CHUNKS, LANES), x1, gates_tk, p2, ple_g.reshape(1, d),
      w_gate, w_proj, fin_g.reshape(1, d))


def _routing_tables(topi, rank, counts, *, tm_combine):
    k, t = topi.shape
    n_blocks = (k * t) // MOE_ROWS + N_EXPERTS
    padded = (counts + MOE_ROWS - 1) // MOE_ROWS * MOE_ROWS
    pad_end = jnp.cumsum(padded)
    pad_start = pad_end - padded
    eids = jnp.arange(N_EXPERTS, dtype=I32)[:, None, None]
    dest = rank + jnp.sum(jnp.where(topi[None] == eids, pad_start[:, None, None], 0), axis=0)
    n_used = (pad_end[-1] // MOE_ROWS).astype(I32).reshape(1)
    blk_row0 = jnp.arange(n_blocks, dtype=I32) * MOE_ROWS
    blk_expert = jnp.minimum(
        jnp.sum((pad_end[None, :] <= blk_row0[:, None]).astype(I32), axis=1), N_EXPERTS - 1)
    tok = jnp.broadcast_to(jnp.arange(t, dtype=I32)[None, :], (k, t))
    row_tok = jnp.zeros((n_blocks * MOE_ROWS,), I32).at[dest.reshape(-1)].set(tok.reshape(-1))
    row_tok3 = row_tok.reshape(n_blocks, 1, MOE_ROWS)
    n_tiles = t // tm_combine
    dest3 = dest.reshape(k, n_tiles, tm_combine).transpose(1, 0, 2).reshape(
        n_tiles, 1, k * tm_combine)
    return blk_expert, n_used, row_tok3, dest3


SEG_ALIGN = 16
MOE_TILE = 512
LOCAL_ROWS = TOP_K * MOE_TILE + N_EXPERTS * SEG_ALIGN
MAX_CHUNKS = LOCAL_ROWS // SEG_ALIGN
CHUNK_TABLE = 256


def _chunk_copy(src, dst, sem):
    return pltpu.make_async_copy(src, dst, sem)


def _chunk_wait_n(local_buf, hbm, sem, slot, n, to_hbm):
    loc = local_buf.at[slot, pl.ds(0, SEG_ALIGN), :]
    far = hbm.at[pl.ds(0, SEG_ALIGN), :]

    def body(j, carry):
        (_chunk_copy(loc, far, sem.at[slot]) if to_hbm
         else _chunk_copy(far, loc, sem.at[slot])).wait()
        return carry

    lax.fori_loop(0, n, body, 0)


def _chunk_start_n(local_buf, hbm, tab_ref, sem, slot, n, to_hbm):
    def body(j, carry):
        loc = local_buf.at[slot, pl.ds(pl.multiple_of(j * SEG_ALIGN, SEG_ALIGN), SEG_ALIGN), :]
        far = hbm.at[pl.ds(pl.multiple_of(tab_ref[0, 0, j] * SEG_ALIGN, SEG_ALIGN), SEG_ALIGN), :]
        (_chunk_copy(loc, far, sem.at[slot]) if to_hbm
         else _chunk_copy(far, loc, sem.at[slot])).start()
        return carry

    lax.fori_loop(0, n, body, 0)


def _dispatch_kernel(nch_ref, tab_ref, lp_ref, h2_ref, xs_hbm, buf, sem, *, n_tiles):
    i = pl.program_id(0)
    slot = i % 2

    @pl.when(i >= 2)
    def _():
        _chunk_wait_n(buf, xs_hbm, sem, slot, nch_ref[jnp.maximum(i - 2, 0)], True)

    lp = lp_ref[...]
    tm = lp.shape[1]
    rows = lax.broadcasted_iota(I32, (LOCAL_ROWS, tm), 0)
    onehot = jnp.zeros((LOCAL_ROWS, tm), F32)
    for k in range(TOP_K):
        onehot = jnp.where(rows == lp[k:k + 1, :], 1.0, onehot)
    buf[slot] = jnp.dot(onehot.astype(BF16), h2_ref[...],
                        preferred_element_type=F32).astype(BF16)
    _chunk_start_n(buf, xs_hbm, tab_ref, sem, slot, nch_ref[i], True)

    @pl.when(i == n_tiles - 1)
    def _():
        _chunk_wait_n(buf, xs_hbm, sem, slot, nch_ref[i], True)
        if n_tiles > 1:
            _chunk_wait_n(buf, xs_hbm, sem, 1 - slot, nch_ref[jnp.maximum(i - 1, 0)], True)


def _dispatch(n_chunks, chunk_tab, lp, h2, n_rows_out):
    k, t = lp.shape
    d = h2.shape[1]
    n_tiles = t // MOE_TILE
    grid_spec = pltpu.PrefetchScalarGridSpec(
        num_scalar_prefetch=1,
        grid=(n_tiles,),
        in_specs=[pl.BlockSpec((1, 1, CHUNK_TABLE), lambda i, nc: (i, 0, 0),
                               memory_space=pltpu.SMEM),
                  pl.BlockSpec((k, MOE_TILE), lambda i, nc: (0, i)),
                  pl.BlockSpec((MOE_TILE, d), lambda i, nc: (i, 0))],
        out_specs=pl.BlockSpec(memory_space=pl.ANY),
        scratch_shapes=[pltpu.VMEM((2, LOCAL_ROWS, d), BF16), pltpu.SemaphoreType.DMA((2,))],
    )
    return pl.pallas_call(
        functools.partial(_dispatch_kernel, n_tiles=n_tiles),
        grid_spec=grid_spec,
        out_shape=jax.ShapeDtypeStruct((n_rows_out, d), BF16),
        compiler_params=pltpu.CompilerParams(
            dimension_semantics=("arbitrary",), vmem_limit_bytes=VMEM_LIMIT),
        name="dispatch",
    )(n_chunks, chunk_tab, lp, h2)


def _expert_mlp_kernel(blk_e_ref, nused_ref, x_ref, wgu_ref, bgu_ref, wdn_ref, bdn_ref, y_ref):
    del blk_e_ref
    i = pl.program_id(0)
    nused = nused_ref[0]

    @pl.when(i < nused)
    def _():
        x = x_ref[...]
        gu = jnp.dot(x, wgu_ref[...], preferred_element_type=F32) + bgu_ref[...]
        d_ff = gu.shape[1] // 2
        gate = jnp.minimum(gu[:, :d_ff], SWIGLU_LIMIT)
        up = jnp.clip(gu[:, d_ff:], -SWIGLU_LIMIT, SWIGLU_LIMIT)
        act = gate * _sigmoid(SWIGLU_ALPHA * gate) * (up + 1.0)
        y = jnp.dot(act.astype(BF16), wdn_ref[...], preferred_element_type=F32) + bdn_ref[...]
        y_ref[...] = y.astype(BF16)

    @pl.when(i >= nused)
    def _():
        y_ref[...] = jnp.zeros_like(y_ref)


def _expert_mlp(blk_expert, n_used, xs, w_gu, b_gu, w_dn, b_dn):
    n_rows, d = xs.shape
    n_blocks = n_rows // MOE_ROWS
    e, _, f2 = w_gu.shape
    grid_spec = pltpu.PrefetchScalarGridSpec(
        num_scalar_prefetch=2,
        grid=(n_blocks,),
        in_specs=[
            pl.BlockSpec((MOE_ROWS, d), lambda i, be, nu: (jnp.minimum(i, nu[0] - 1), 0)),
            pl.BlockSpec((None, d, f2), lambda i, be, nu: (be[i], 0, 0)),
            pl.BlockSpec((None, 1, f2), lambda i, be, nu: (be[i], 0, 0)),
            pl.BlockSpec((None, f2 // 2, d), lambda i, be, nu: (be[i], 0, 0)),
            pl.BlockSpec((None, 1, d), lambda i, be, nu: (be[i], 0, 0)),
        ],
        out_specs=pl.BlockSpec((MOE_ROWS, d), lambda i, be, nu: (i, 0)),
    )
    return pl.pallas_call(
        _expert_mlp_kernel,
        grid_spec=grid_spec,
        out_shape=jax.ShapeDtypeStruct((n_rows, d), BF16),
        compiler_params=pltpu.CompilerParams(
            dimension_semantics=("arbitrary",), vmem_limit_bytes=VMEM_LIMIT),
        name="experts",
    )(blk_expert, n_used, xs, w_gu, b_gu.reshape(e, 1, f2), w_dn, b_dn.reshape(e, 1, d))


def _moe_combine_kernel(nch_ref, tab_cur_ref, tab_nxt_ref, y_hbm, lpt_ref, gate_ref, x1_ref,
                        p_ref, gp_ref, wg_ref, wp_ref, gfin_ref, o_ref, buf, sem, *, n_tiles):
    i = pl.program_id(0)
    slot = i % 2

    @pl.when(i == 0)
    def _():
        buf[...] = jnp.zeros_like(buf)
        _chunk_start_n(buf, y_hbm, tab_cur_ref, sem, 0, nch_ref[0], False)

    @pl.when(i + 1 < n_tiles)
    def _():
        _chunk_start_n(buf, y_hbm, tab_nxt_ref, sem, 1 - slot,
                       nch_ref[jnp.minimum(i + 1, n_tiles - 1)], False)

    _chunk_wait_n(buf, y_hbm, sem, slot, nch_ref[i], False)
    lpt = lpt_ref[...]
    gates = gate_ref[...]
    tm = lpt.shape[0]
    cols = lax.broadcasted_iota(I32, (tm, LOCAL_ROWS), 1)
    weights = jnp.zeros((tm, LOCAL_ROWS), F32)
    for k in range(TOP_K):
        weights = jnp.where(cols == lpt[:, k:k + 1], gates[:, k:k + 1], weights)
    x2 = x1_ref[...] + jnp.dot(weights.astype(BF16), buf[slot], preferred_element_type=F32)
    hg = _rms(x2, gp_ref[...]).astype(BF16)
    gate = _sigmoid(jnp.dot(hg, wg_ref[...], preferred_element_type=F32))
    emb = jnp.dot(p_ref[...].astype(BF16), wp_ref[...], preferred_element_type=F32)
    x3 = x2 + gate * emb
    o_ref[...] = _rms(x3, gfin_ref[...])


def _moe_combine(n_chunks, chunk_tab, y, lpt, gates_tk, x1, p2, ple_g, w_gate, w_proj, fin_g):
    t, d = x1.shape
    n_tiles = t // MOE_TILE
    last = n_tiles - 1
    ple = p2.shape[1]
    row = lambda i, nc: (i, 0)
    const = lambda i, nc: (0, 0)
    grid_spec = pltpu.PrefetchScalarGridSpec(
        num_scalar_prefetch=1,
        grid=(n_tiles,),
        in_specs=[pl.BlockSpec((1, 1, CHUNK_TABLE), lambda i, nc: (i, 0, 0),
                               memory_space=pltpu.SMEM),
                  pl.BlockSpec((1, 1, CHUNK_TABLE),
                               lambda i, nc: (jnp.minimum(i + 1, last), 0, 0),
                               memory_space=pltpu.SMEM),
                  pl.BlockSpec(memory_space=pl.ANY),
                  pl.BlockSpec((MOE_TILE, TOP_K), row),
                  pl.BlockSpec((MOE_TILE, TOP_K), row),
                  pl.BlockSpec((MOE_TILE, d), row),
                  pl.BlockSpec((MOE_TILE, ple), row),
                  pl.BlockSpec((1, d), const),
                  pl.BlockSpec((d, d), const),
                  pl.BlockSpec((ple, d), const),
                  pl.BlockSpec((1, d), const)],
        out_specs=pl.BlockSpec((MOE_TILE, d), row),
        scratch_shapes=[pltpu.VMEM((2, LOCAL_ROWS, d), BF16), pltpu.SemaphoreType.DMA((2,))],
    )
    return pl.pallas_call(
        functools.partial(_moe_combine_kernel, n_tiles=n_tiles),
        grid_spec=grid_spec,
        out_shape=jax.ShapeDtypeStruct((t, d), F32),
        compiler_params=pltpu.CompilerParams(
            dimension_semantics=("arbitrary",), vmem_limit_bytes=VMEM_LIMIT),
        name="combine",
    )(n_chunks, chunk_tab, chunk_tab, y, lpt, gates_tk, x1, p2, ple_g.reshape(1, d),
      w_gate, w_proj, fin_g.reshape(1, d))


def _moe_tables(topi, lrank, cnt):
    k, t = topi.shape
    n_tiles = t // MOE_TILE
    cnt = cnt.astype(I32)
    pc = (cnt + SEG_ALIGN - 1) // SEG_ALIGN * SEG_ALIGN
    l_end = jnp.cumsum(pc, axis=1)
    l_start = l_end - pc
    tot = jnp.sum(pc, axis=0)
    e_pad = (tot + MOE_ROWS - 1) // MOE_ROWS * MOE_ROWS
    e_end = jnp.cumsum(e_pad)
    seg_row0 = (e_end - e_pad)[None, :] + jnp.cumsum(pc, axis=0) - pc
    eids = jnp.arange(N_EXPERTS, dtype=I32)[:, None, None, None]
    topi3 = topi.reshape(k, n_tiles, MOE_TILE)
    base = jnp.sum(jnp.where(topi3[None] == eids, l_start.T[:, None, :, None], 0), axis=0)
    lp = (lrank.reshape(k, n_tiles, MOE_TILE) + base).reshape(k, t)
    j16 = jnp.arange(CHUNK_TABLE, dtype=I32)[None, :] * SEG_ALIGN
    ce = jnp.minimum(jnp.sum((l_end[:, :, None] <= j16[:, None, :]).astype(I32), axis=1),
                     N_EXPERTS - 1)
    seg0 = jnp.take_along_axis(seg_row0, ce, axis=1)
    loc0 = jnp.take_along_axis(l_start, ce, axis=1)
    n_chunks = l_end[:, -1] // SEG_ALIGN
    valid = jnp.arange(CHUNK_TABLE, dtype=I32)[None, :] < n_chunks[:, None]
    tab = jnp.where(valid, (seg0 + j16 - loc0) // SEG_ALIGN, 0).astype(I32)
    n_rows_max = k * t + n_tiles * N_EXPERTS * (SEG_ALIGN - 1) + N_EXPERTS * (MOE_ROWS - 1)
    n_blocks = (n_rows_max + MOE_ROWS - 1) // MOE_ROWS
    n_used = (e_end[-1] // MOE_ROWS).astype(I32).reshape(1)
    blk_row0 = jnp.arange(n_blocks, dtype=I32) * MOE_ROWS
    blk_expert = jnp.minimum(
        jnp.sum((e_end[None, :] <= blk_row0[:, None]).astype(I32), axis=1), N_EXPERTS - 1)
    return (lp, n_chunks.astype(I32), tab.reshape(n_tiles, 1, CHUNK_TABLE), blk_expert, n_used,
            n_blocks * MOE_ROWS)


def _layer(x2, p2, seq, norm_mix_g, w_in, b_f, lam_re, lam_im, log_dt, b_re, b_im, c_re, c_im,
           d_skip, w_glu, b_glu, attn_out_g, ssm_out_g, w_out, norm_ffn_g, w_router, b_router,
           w_gu, b_gu, w_dn, b_dn, norm_ple_g, w_ple_gate, w_ple_proj, final_g):
    t, d = x2.shape
    nb = t // seq
    aw = ATTN_WIDTH
    w_main = jnp.concatenate([w_in[:, :3 * aw], w_in[:, 3 * aw + N_HEADS:]], axis=1).astype(BF16)
    wft = w_in[:, 3 * aw:3 * aw + N_HEADS].T.astype(BF16)

    a_re, a_im, bbr, bbi = _ssm_prep(lam_re, lam_im, log_dt, b_re, b_im)
    bre = _block_diag(bbr, True).astype(BF16)
    bim = _block_diag(bbi, True).astype(BF16)
    cre = _block_diag(jnp.transpose(c_re, (0, 2, 1)), False).astype(BF16)
    cim = _block_diag(jnp.transpose(c_im, (0, 2, 1)), False).astype(BF16)

    q, k, v, u, c_t = _in_proj(x2, norm_mix_g, w_main, wft, b_f, seq=seq)
    shp = (nb, seq, aw)
    c3 = c_t.reshape(N_HEADS // 2, 2, t)
    attn = _attention(q.reshape(shp), k.reshape(shp), v.reshape(shp), c3)
    ssm = _ssm(u.reshape(nb, seq, SSM_WIDTH), bre, bim, cre, cim, a_re, a_im,
               d_skip, w_glu.astype(BF16), b_glu, ssm_out_g)

    x1, h2, topi, gates, lrank, cnt = _out_proj(
        x2, attn.reshape(t, aw), ssm.reshape(t, SSM_WIDTH), attn_out_g,
        w_out[:aw].astype(BF16), w_out[aw:].astype(BF16), norm_ffn_g,
        w_router.T.astype(BF16), b_router, tm=MOE_TILE)

    lp, n_chunks, chunk_tab, blk_expert, n_used, n_rows = _moe_tables(topi, lrank, cnt[:, :, 0])
    xs = _dispatch(n_chunks, chunk_tab, lp, h2, n_rows)
    y = _expert_mlp(blk_expert, n_used, xs, w_gu.astype(BF16), b_gu, w_dn.astype(BF16), b_dn)
    return _moe_combine(n_chunks, chunk_tab, y, lp.T, gates.T, x1, p2, norm_ple_g,
                        w_ple_gate.astype(BF16), w_ple_proj.astype(BF16), final_g)


def kernel(x, p, norm_mix_g, w_in, b_f, lam_re, lam_im, log_dt, b_re, b_im, c_re, c_im, d_skip, w_glu, b_glu, attn_out_g, ssm_out_g, w_out, norm_ffn_g, w_router, b_router, w_gu, b_gu, w_dn, b_dn, norm_ple_g, w_ple_gate, w_ple_proj, norm_final_g):
    bsz, seq, d = x.shape
    depth = w_in.shape[0]
    assert depth == 1, "one layer: the final rmsnorm is fused into the layer's last kernel"
    out = _layer(x.reshape(bsz * seq, d), p[0].reshape(bsz * seq, -1), seq,
                 norm_mix_g[0], w_in[0], b_f[0], lam_re[0], lam_im[0], log_dt[0],
                 b_re[0], b_im[0], c_re[0], c_im[0], d_skip[0], w_glu[0], b_glu[0],
                 attn_out_g[0], ssm_out_g[0], w_out[0], norm_ffn_g[0], w_router[0],
                 b_router[0], w_gu[0], b_gu[0], w_dn[0], b_dn[0], norm_ple_g[0],
                 w_ple_gate[0], w_ple_proj[0], norm_final_g)
    return out.reshape(bsz, seq, d)
```

```python
import functools
import math

import jax
import jax.numpy as jnp
from jax import lax
from jax.experimental import pallas as pl
from jax.experimental.pallas import tpu as pltpu

F32 = jnp.float32
BF16 = jnp.bfloat16
I32 = jnp.int32

NORM_EPS = 1e-5
HEAD_DIM = 64
N_HEADS = 8
ATTN_WIDTH = 512
SSM_WIDTH = 512
SSM_GROUP = 16
N_SSM_GROUPS = 32
SSM_STATE = 64
N_STATE = N_SSM_GROUPS * SSM_STATE
N_EXPERTS = 32
TOP_K = 4
SWIGLU_LIMIT = 7.0
SWIGLU_ALPHA = 1.702
LANES = 128
SUBLANES = 8
ROW_CHUNKS = 8
MOE_ROWS = 512
NEG_BIG = -1e30
LOG2E = math.log2(math.e)
VMEM_LIMIT = 56 * 1024 * 1024

_NT = (((1,), (1,)), ((), ()))


def _rms(xf, g):
    ms = jnp.mean(xf * xf, axis=-1, keepdims=True)
    return xf * lax.rsqrt(ms + NORM_EPS) * g


def _sigmoid(x):
    return 1.0 / (1.0 + jnp.exp(-x))


def _ssm_prep_kernel(lr_ref, li_ref, ldt_ref, brt_ref, bit_ref,
                     ar_ref, ai_ref, bbr_ref, bbi_ref):
    lr = lr_ref[...]
    li = li_ref[...]
    dt = jnp.exp(ldt_ref[...])
    mag = jnp.exp(lr * dt)
    ab_re = mag * jnp.cos(li * dt)
    ab_im = mag * jnp.sin(li * dt)
    den = lr * lr + li * li
    nr = ab_re - 1.0
    z_re = (nr * lr + ab_im * li) / den
    z_im = (ab_im * lr - nr * li) / den
    ar_ref[...] = ab_re
    ai_ref[...] = ab_im
    br = brt_ref[...]
    bi = bit_ref[...]
    bbr_ref[...] = z_re * br - z_im * bi
    bbi_ref[...] = z_re * bi + z_im * br


def _ssm_prep(lam_re, lam_im, log_dt, b_re, b_im):
    g, p, c = b_re.shape
    brt = jnp.transpose(b_re, (0, 2, 1))
    bit = jnp.transpose(b_im, (0, 2, 1))
    return pl.pallas_call(
        _ssm_prep_kernel,
        out_shape=(jax.ShapeDtypeStruct((g, 1, p), F32), jax.ShapeDtypeStruct((g, 1, p), F32),
                   jax.ShapeDtypeStruct((g, c, p), F32), jax.ShapeDtypeStruct((g, c, p), F32)),
        name="ssm_prep",
    )(lam_re.reshape(g, 1, p), lam_im.reshape(g, 1, p), log_dt.reshape(g, 1, 1), brt, bit)


def _block_diag(w, rows_first):
    g, a, b = w.shape
    half = g // 2
    eye = jnp.eye(half, dtype=w.dtype)
    w4 = w.reshape(2, half, a, b)
    out = jnp.einsum('hgab,gk->hgakb', w4, eye)
    del rows_first
    return out.reshape(2, half * a, half * b)


def _inproj_kernel(x_ref, g_ref, w_ref, wft_ref, bf_ref, tri_ref,
                   q_ref, k_ref, v_ref, u_ref, c_ref, carry_ref, *, tiles_per_seq, tm):
    i = pl.program_id(0)

    @pl.when(i % tiles_per_seq == 0)
    def _():
        carry_ref[...] = jnp.zeros_like(carry_ref)

    h = _rms(x_ref[...], g_ref[...]).astype(BF16)
    proj = jnp.dot(h, w_ref[...], preferred_element_type=F32)
    aw = ATTN_WIDTH
    q_ref[...] = (proj[:, 0:aw] * (LOG2E * HEAD_DIM ** -0.5)).astype(BF16)
    k_ref[...] = proj[:, aw:2 * aw].astype(BF16)
    v_ref[...] = proj[:, 2 * aw:3 * aw].astype(BF16)
    u_ref[...] = proj[:, 3 * aw:3 * aw + SSM_WIDTH].astype(BF16)

    fl = lax.dot_general(wft_ref[...], h, _NT, preferred_element_type=F32)
    z = fl + bf_ref[...]
    lf = jnp.minimum(z, 0.0) - jnp.log1p(jnp.exp(-jnp.abs(z)))
    hi = lf.astype(BF16)
    lo = (lf - hi.astype(F32)).astype(BF16)
    tri = tri_ref[...]
    cs = (jnp.dot(hi, tri, preferred_element_type=F32)
          + jnp.dot(lo, tri, preferred_element_type=F32))
    c = cs + carry_ref[:, 0:1]
    c_ref[...] = c * LOG2E
    carry_ref[...] = jnp.broadcast_to(c[:, tm - 1:tm], carry_ref.shape)


def _in_proj(x2, norm_g, w_main, wft, b_f, *, seq, tm=512):
    t, d = x2.shape
    n_main = w_main.shape[1]
    tri = jnp.triu(jnp.ones((tm, tm), F32)).astype(BF16)
    kern = functools.partial(_inproj_kernel, tiles_per_seq=seq // tm, tm=tm)
    row = lambda i: (i, 0)
    const = lambda i: (0, 0)
    act = jax.ShapeDtypeStruct((t, ATTN_WIDTH), BF16)
    return pl.pallas_call(
        kern,
        grid=(t // tm,),
        in_specs=[pl.BlockSpec((tm, d), row),
                  pl.BlockSpec((1, d), const),
                  pl.BlockSpec((d, n_main), const),
                  pl.BlockSpec((N_HEADS, d), const),
                  pl.BlockSpec((N_HEADS, 1), const),
                  pl.BlockSpec((tm, tm), const)],
        out_specs=[pl.BlockSpec((tm, ATTN_WIDTH), row)] * 4
        + [pl.BlockSpec((N_HEADS, tm), lambda i: (0, i))],
        out_shape=[act, act, act, act, jax.ShapeDtypeStruct((N_HEADS, t), F32)],
        scratch_shapes=[pltpu.VMEM((N_HEADS, LANES), F32)],
        compiler_params=pltpu.CompilerParams(
            dimension_semantics=("arbitrary",), vmem_limit_bytes=VMEM_LIMIT),
        name="in_proj",
    )(x2, norm_g.reshape(1, d), w_main, wft, b_f.reshape(N_HEADS, 1), tri)


def _attn_kernel(q_ref, k_ref, v_ref, c_ref, o_ref, *, tq):
    i = pl.program_id(2)
    q2 = q_ref[...]
    lane = lax.broadcasted_iota(I32, (1, LANES), 1)
    first = lane < HEAD_DIM
    zero = jnp.zeros_like(q2)
    q_heads = (jnp.where(first, q2, zero), jnp.where(first, zero, q2))

    def block(j, carry, masked):
        off = pl.multiple_of(j * tq, tq)
        kj = k_ref[pl.ds(off, tq), :]
        vj = v_ref[pl.ds(off, tq), :]
        cj = c_ref[:, pl.ds(off, tq)]
        one = jnp.ones_like(vj)
        v_heads = (jnp.where(first, vj, one), jnp.where(first, one, vj))
        out = []
        for h in range(2):
            m, acc = carry[h]
            s = (lax.dot_general(q_heads[h], kj, _NT, preferred_element_type=F32)
                 - cj[h:h + 1, :])
            if masked:
                rr = lax.broadcasted_iota(I32, (tq, tq), 0)
                cc = lax.broadcasted_iota(I32, (tq, tq), 1)
                s = jnp.where(cc <= rr, s, NEG_BIG)
            m_new = jnp.maximum(m, jnp.max(s, axis=-1, keepdims=True))
            alpha = jnp.exp2(m - m_new)
            p = jnp.exp2(s - m_new).astype(BF16)
            acc = alpha * acc + jnp.dot(p, v_heads[h], preferred_element_type=F32)
            out.append((m_new, acc))
        return tuple(out)

    init_one = (jnp.full((tq, 1), NEG_BIG, F32), jnp.zeros((tq, LANES), F32))
    carry = lax.fori_loop(0, i, lambda j, c: block(j, c, False), (init_one, init_one))
    (_, acc_a), (_, acc_b) = block(i, carry, True)
    o = jnp.where(first, acc_a / pltpu.roll(acc_a, HEAD_DIM, axis=1),
                  acc_b / pltpu.roll(acc_b, HEAD_DIM, axis=1))
    o_ref[...] = o.astype(BF16)


def _attention(q, k, v, c3, *, tq=1024):
    b, s, w = q.shape
    n_pairs = w // LANES
    kern = functools.partial(_attn_kernel, tq=tq)
    return pl.pallas_call(
        kern,
        grid=(b, n_pairs, s // tq),
        in_specs=[pl.BlockSpec((None, tq, LANES), lambda bi, hp, i: (bi, i, hp)),
                  pl.BlockSpec((None, s, LANES), lambda bi, hp, i: (bi, 0, hp)),
                  pl.BlockSpec((None, s, LANES), lambda bi, hp, i: (bi, 0, hp)),
                  pl.BlockSpec((None, 2, s), lambda bi, hp, i: (hp, 0, bi))],
        out_specs=pl.BlockSpec((None, tq, LANES), lambda bi, hp, i: (bi, i, hp)),
        out_shape=jax.ShapeDtypeStruct((b, s, w), BF16),
        compiler_params=pltpu.CompilerParams(
            dimension_semantics=("arbitrary", "arbitrary", "arbitrary"),
            vmem_limit_bytes=VMEM_LIMIT),
        name="attention",
    )(q, k, v, c3)


def _ssm_kernel(u_ref, bre_ref, bim_ref, cre_ref, cim_ref, ar_ref, ai_ref, dsk_ref,
                wglu_ref, bglu_ref, g_ref, o_ref,
                us_ref, xr_ref, xi_ref, str_ref, sti_ref, res_ref, *, tt, nb):
    i = pl.program_id(0)

    @pl.when(i == 0)
    def _():
        str_ref[...] = jnp.zeros_like(str_ref)
        sti_ref[...] = jnp.zeros_like(sti_ref)

    n_chunks = SSM_WIDTH // LANES
    for b in range(nb):
        ub32 = u_ref[b].astype(F32)
        for c in range(n_chunks):
            us_ref[c, pl.ds(b, tt, stride=nb), :] = ub32[:, c * LANES:(c + 1) * LANES]
    uf = jnp.concatenate([us_ref[c] for c in range(n_chunks)], axis=-1)
    ub = uf.astype(BF16)
    half_in = SSM_WIDTH // 2
    half_st = N_STATE // 2
    for hf in range(2):
        uh = ub[:, hf * half_in:(hf + 1) * half_in]
        xr_ref[:, hf * half_st:(hf + 1) * half_st] = jnp.dot(
            uh, bre_ref[hf], preferred_element_type=F32)
        xi_ref[:, hf * half_st:(hf + 1) * half_st] = jnp.dot(
            uh, bim_ref[hf], preferred_element_type=F32)

    n_col_groups = 2
    wcol = N_STATE // n_col_groups
    unroll = 4
    for cg in range(n_col_groups):
        cols = slice(cg * wcol, (cg + 1) * wcol)
        ar = jnp.broadcast_to(ar_ref[:, cols], (nb, wcol))
        ai = jnp.broadcast_to(ai_ref[:, cols], (nb, wcol))

        def steps(tb, carry, cols=cols, ar=ar, ai=ai):
            sr, si = carry
            for k in range(unroll):
                r0 = pl.multiple_of((tb * unroll + k) * nb, nb)
                br = xr_ref[pl.ds(r0, nb), cols]
                bi = xi_ref[pl.ds(r0, nb), cols]
                nr = ar * sr - ai * si + br
                ni = ar * si + ai * sr + bi
                xr_ref[pl.ds(r0, nb), cols] = nr
                xi_ref[pl.ds(r0, nb), cols] = ni
                sr, si = nr, ni
            return sr, si

        sr, si = lax.fori_loop(0, tt // unroll, steps, (str_ref[:, cols], sti_ref[:, cols]))
        str_ref[:, cols] = sr
        sti_ref[:, cols] = si

    ys = []
    for hf in range(2):
        xr = xr_ref[:, hf * half_st:(hf + 1) * half_st].astype(BF16)
        xi = xi_ref[:, hf * half_st:(hf + 1) * half_st].astype(BF16)
        ys.append(jnp.dot(xr, cre_ref[hf], preferred_element_type=F32)
                  - jnp.dot(xi, cim_ref[hf], preferred_element_type=F32))
    y = jnp.concatenate(ys, axis=-1) + dsk_ref[...] * uf
    gl = 0.5 * y * (1.0 + jnp.tanh(math.sqrt(2.0 / math.pi) * (y + 0.044715 * (y * y * y))))
    zz = jnp.dot(gl.astype(BF16), wglu_ref[...], preferred_element_type=F32) + bglu_ref[...]
    out = gl * _sigmoid(zz)
    res = _rms(out, g_ref[...])
    for c in range(n_chunks):
        res_ref[c] = res[:, c * LANES:(c + 1) * LANES]
    for b in range(nb):
        o_ref[b] = jnp.concatenate(
            [res_ref[c, pl.ds(b, tt, stride=nb), :] for c in range(n_chunks)],
            axis=-1).astype(BF16)


def _ssm(u3, bre, bim, cre, cim, a_re, a_im, d_skip, w_glu, b_glu, out_g, *, tt=64):
    nb, s, w = u3.shape
    rows = tt * nb
    kern = functools.partial(_ssm_kernel, tt=tt, nb=nb)
    c3 = lambda i: (0, 0, 0)
    c2 = lambda i: (0, 0)
    return pl.pallas_call(
        kern,
        grid=(s // tt,),
        in_specs=[pl.BlockSpec((nb, tt, w), lambda i: (0, i, 0)),
                  pl.BlockSpec(bre.shape, c3), pl.BlockSpec(bim.shape, c3),
                  pl.BlockSpec(cre.shape, c3), pl.BlockSpec(cim.shape, c3),
                  pl.BlockSpec((1, N_STATE), c2), pl.BlockSpec((1, N_STATE), c2),
                  pl.BlockSpec((1, w), c2),
                  pl.BlockSpec((w, w), c2), pl.BlockSpec((1, w), c2), pl.BlockSpec((1, w), c2)],
        out_specs=pl.BlockSpec((nb, tt, w), lambda i: (0, i, 0)),
        out_shape=jax.ShapeDtypeStruct((nb, s, w), BF16),
        scratch_shapes=[pltpu.VMEM((w // LANES, rows, LANES), F32),
                        pltpu.VMEM((rows, N_STATE), F32), pltpu.VMEM((rows, N_STATE), F32),
                        pltpu.VMEM((nb, N_STATE), F32), pltpu.VMEM((nb, N_STATE), F32),
                        pltpu.VMEM((w // LANES, rows, LANES), F32)],
        compiler_params=pltpu.CompilerParams(
            dimension_semantics=("arbitrary",), vmem_limit_bytes=VMEM_LIMIT),
        name="ssm",
    )(u3, bre, bim, cre, cim, a_re.reshape(1, N_STATE), a_im.reshape(1, N_STATE),
      d_skip.reshape(1, w), w_glu, b_glu.reshape(1, w), out_g.reshape(1, w))


def _outproj_kernel(x_ref, a_ref, s_ref, ga_ref, woa_ref, wos_ref, gf_ref, wrt_ref, br_ref,
                    tri_ref, x1_ref, h2_ref, topi_ref, gate_ref, rank_ref, cnt_ref, *, tm):
    a = _rms(a_ref[...].astype(F32), ga_ref[...]).astype(BF16)
    x1 = (x_ref[...] + jnp.dot(a, woa_ref[...], preferred_element_type=F32)
          + jnp.dot(s_ref[...], wos_ref[...], preferred_element_type=F32))
    x1_ref[...] = x1
    h2 = _rms(x1, gf_ref[...]).astype(BF16)
    h2_ref[...] = h2

    lg = lax.dot_general(wrt_ref[...], h2, _NT,
                         preferred_element_type=F32) + br_ref[...]
    ids = lax.broadcasted_iota(I32, (N_EXPERTS, tm), 0)
    work = lg
    vals, idxs = [], []
    for _ in range(TOP_K):
        m = jnp.max(work, axis=0, keepdims=True)
        idx = jnp.min(jnp.where(work == m, ids, N_EXPERTS), axis=0, keepdims=True)
        vals.append(m)
        idxs.append(idx)
        work = jnp.where(ids == idx, -jnp.inf, work)
    exps = [jnp.exp(v - vals[0]) for v in vals]
    den = exps[0] + exps[1] + exps[2] + exps[3]
    gate_ref[...] = jnp.concatenate([e / den for e in exps], axis=0)
    topi_ref[...] = jnp.concatenate(idxs, axis=0)

    sel = jnp.zeros((N_EXPERTS, tm), F32)
    for idx in idxs:
        sel = sel + jnp.where(ids == idx, 1.0, 0.0)
    before = jnp.dot(sel.astype(BF16), tri_ref[...], preferred_element_type=F32)
    ranks = [jnp.sum(jnp.where(ids == idx, before, 0.0), axis=0, keepdims=True) for idx in idxs]
    rank_ref[...] = jnp.concatenate(ranks, axis=0).astype(I32)
    cnt_ref[0] = jnp.broadcast_to(jnp.sum(sel, axis=1, keepdims=True), (N_EXPERTS, LANES))


def _out_proj(x2, attn, ssm, attn_g, wo_a, wo_s, ffn_g, wrt, b_router, *, tm=512):
    t, d = x2.shape
    tri = jnp.triu(jnp.ones((tm, tm), F32), k=1).astype(BF16)
    kern = functools.partial(_outproj_kernel, tm=tm)
    row = lambda i: (i, 0)
    const = lambda i: (0, 0)
    colblk = lambda i: (0, i)
    return pl.pallas_call(
        kern,
        grid=(t // tm,),
        in_specs=[pl.BlockSpec((tm, d), row),
                  pl.BlockSpec((tm, ATTN_WIDTH), row),
                  pl.BlockSpec((tm, SSM_WIDTH), row),
                  pl.BlockSpec((1, ATTN_WIDTH), const),
                  pl.BlockSpec((ATTN_WIDTH, d), const),
                  pl.BlockSpec((SSM_WIDTH, d), const),
                  pl.BlockSpec((1, d), const),
                  pl.BlockSpec((N_EXPERTS, d), const),
                  pl.BlockSpec((N_EXPERTS, 1), const),
                  pl.BlockSpec((tm, tm), const)],
        out_specs=[pl.BlockSpec((tm, d), row),
                   pl.BlockSpec((tm, d), row),
                   pl.BlockSpec((TOP_K, tm), colblk),
                   pl.BlockSpec((TOP_K, tm), colblk),
                   pl.BlockSpec((TOP_K, tm), colblk),
                   pl.BlockSpec((1, N_EXPERTS, LANES), lambda i: (i, 0, 0))],
        out_shape=[jax.ShapeDtypeStruct((t, d), F32),
                   jax.ShapeDtypeStruct((t, d), BF16),
                   jax.ShapeDtypeStruct((TOP_K, t), I32),
                   jax.ShapeDtypeStruct((TOP_K, t), F32),
                   jax.ShapeDtypeStruct((TOP_K, t), I32),
                   jax.ShapeDtypeStruct((t // tm, N_EXPERTS, LANES), F32)],
        compiler_params=pltpu.CompilerParams(
            dimension_semantics=("arbitrary",), vmem_limit_bytes=VMEM_LIMIT),
        name="out_proj",
    )(x2, attn, ssm, attn_g.reshape(1, -1), wo_a, wo_s, ffn_g.reshape(1, d), wrt,
      b_router.reshape(N_EXPERTS, 1), tri)


def _row_gather_start(src_hbm, idx_ref, buf, sem, slot, rows):
    for r in rows:
        if r % 16:
            continue
        t0 = jnp.minimum(idx_ref[0, 0, r], src_hbm.shape[0] - 16)
        pltpu.make_async_copy(
            src_hbm.at[pl.ds(t0, 16)].reshape(16 * ROW_CHUNKS, LANES),
            buf.at[slot, pl.ds(r * ROW_CHUNKS, 16 * ROW_CHUNKS), :],
            sem.at[slot]).start(priority=(r // 16) % 2)


def _row_gather_wait(buf, sem, slot):
    pltpu.make_async_copy(buf.at[slot], buf.at[slot], sem.at[slot]).wait()


def _issue_anchor(buf, slot):
    return buf[slot, 0:1, 0:1] * 0.0


def _rows_to_matrix(buf, slot, row0, n_rows):
    return jnp.concatenate(
        [buf[slot, pl.ds(row0 * ROW_CHUNKS + c, n_rows, stride=ROW_CHUNKS), :]
         for c in range(ROW_CHUNKS)], axis=-1)


def _expert_kernel(blk_e_ref, nused_ref, tok_cur_ref, tok_nxt_ref, h2_hbm,
                   wgu_ref, bgu_ref, wdn_ref, bdn_ref, y_ref, xbuf, sem):
    del blk_e_ref
    i = pl.program_id(0)
    slot = i % 2
    nused = nused_ref[0]
    d_ff = wdn_ref.shape[0]
    n_col = 4
    wc = d_ff // n_col
    per = MOE_ROWS // (2 * n_col)

    @pl.when(i == 0)
    def _():
        _row_gather_start(h2_hbm, tok_cur_ref, xbuf, sem, 0, range(MOE_ROWS))

    @pl.when(i < nused)
    def _():
        _row_gather_wait(xbuf, sem, slot)
        x = _rows_to_matrix(xbuf, slot, 0, MOE_ROWS).astype(BF16)
        acts = []
        for c in range(n_col):
            g = (jnp.dot(x, wgu_ref[:, c * wc:(c + 1) * wc], preferred_element_type=F32)
                 + bgu_ref[:, c * wc:(c + 1) * wc])
            _row_gather_start(h2_hbm, tok_nxt_ref, xbuf, sem, 1 - slot,
                              range(2 * c * per, (2 * c + 1) * per))
            g = g + _issue_anchor(xbuf, slot)
            u = (jnp.dot(x, wgu_ref[:, d_ff + c * wc:d_ff + (c + 1) * wc],
                         preferred_element_type=F32)
                 + bgu_ref[:, d_ff + c * wc:d_ff + (c + 1) * wc])
            _row_gather_start(h2_hbm, tok_nxt_ref, xbuf, sem, 1 - slot,
                              range((2 * c + 1) * per, (2 * c + 2) * per))
            u = u + _issue_anchor(xbuf, slot)
            gate = jnp.minimum(g, SWIGLU_LIMIT)
            up = jnp.clip(u, -SWIGLU_LIMIT, SWIGLU_LIMIT)
            acts.append((gate * _sigmoid(SWIGLU_ALPHA * gate) * (up + 1.0)).astype(BF16))
        act = jnp.concatenate(acts, axis=-1)
        y = jnp.dot(act, wdn_ref[...], preferred_element_type=F32) + bdn_ref[...]
        for c in range(ROW_CHUNKS):
            y_ref[pl.ds(c, MOE_ROWS, stride=ROW_CHUNKS), :] = y[:, c * LANES:(c + 1) * LANES]

        @pl.when(i == nused - 1)
        def _():
            _row_gather_wait(xbuf, sem, 1 - slot)

    @pl.when(i >= nused)
    def _():
        y_ref[...] = jnp.zeros_like(y_ref)


def _experts(blk_expert, n_used, row_tok3, h2r, w_gu, b_gu, w_dn, b_dn):
    n_blocks = row_tok3.shape[0]
    e, d, f2 = w_gu.shape
    blk_rows = MOE_ROWS * ROW_CHUNKS
    grid_spec = pltpu.PrefetchScalarGridSpec(
        num_scalar_prefetch=2,
        grid=(n_blocks,),
        in_specs=[
            pl.BlockSpec((1, 1, MOE_ROWS), lambda i, be, nu: (i, 0, 0),
                         memory_space=pltpu.SMEM),
            pl.BlockSpec((1, 1, MOE_ROWS),
                         lambda i, be, nu: (jnp.minimum(i + 1, nu[0] - 1), 0, 0),
                         memory_space=pltpu.SMEM),
            pl.BlockSpec(memory_space=pl.ANY),
            pl.BlockSpec((None, d, f2), lambda i, be, nu: (be[i], 0, 0)),
            pl.BlockSpec((None, 1, f2), lambda i, be, nu: (be[i], 0, 0)),
            pl.BlockSpec((None, f2 // 2, d), lambda i, be, nu: (be[i], 0, 0)),
            pl.BlockSpec((None, 1, d), lambda i, be, nu: (be[i], 0, 0)),
        ],
        out_specs=pl.BlockSpec((blk_rows, LANES), lambda i, be, nu: (i, 0)),
        scratch_shapes=[pltpu.VMEM((2, blk_rows, LANES), F32),
                        pltpu.SemaphoreType.DMA((2,))],
    )
    return pl.pallas_call(
        _expert_kernel,
        grid_spec=grid_spec,
        out_shape=jax.ShapeDtypeStruct((n_blocks * blk_rows, LANES), F32),
        compiler_params=pltpu.CompilerParams(
            dimension_semantics=("arbitrary",), vmem_limit_bytes=VMEM_LIMIT),
        name="experts",
    )(blk_expert, n_used, row_tok3, row_tok3, h2r, w_gu, b_gu.reshape(e, 1, f2),
      w_dn, b_dn.reshape(e, 1, d))


def _combine_kernel(dst_cur_ref, dst_nxt_ref, y_hbm, x1_ref, gate_ref, p_ref, gp_ref,
                    wg_ref, wp_ref, gfin_ref, o_ref, ybuf, sem, *, tm, n_tiles):
    i = pl.program_id(0)
    slot = i % 2
    n_rows = TOP_K * tm
    d = x1_ref.shape[1]
    n_col = 4
    wc = d // n_col
    per = n_rows // (TOP_K + n_col)

    def start_next(part):
        _row_gather_start(y_hbm, dst_nxt_ref, ybuf, sem, 1 - slot,
                          range(part * per, (part + 1) * per))

    @pl.when(i == 0)
    def _():
        _row_gather_start(y_hbm, dst_cur_ref, ybuf, sem, 0, range(n_rows))

    _row_gather_wait(ybuf, sem, slot)
    gates = gate_ref[...]
    x2 = x1_ref[...]
    for k in range(TOP_K):
        x2 = x2 + gates[:, k:k + 1] * _rows_to_matrix(ybuf, slot, k * tm, tm)
        start_next(k)
    hg = _rms(x2, gp_ref[...]).astype(BF16)
    pb = p_ref[...].astype(BF16)
    x3 = []
    for c in range(n_col):
        cols = slice(c * wc, (c + 1) * wc)
        gate = _sigmoid(jnp.dot(hg, wg_ref[:, cols], preferred_element_type=F32))
        emb = jnp.dot(pb, wp_ref[:, cols], preferred_element_type=F32)
        start_next(TOP_K + c)
        x3.append(x2[:, cols] + gate * emb + _issue_anchor(ybuf, slot))
    o_ref[...] = _rms(jnp.concatenate(x3, axis=-1), gfin_ref[...])

    @pl.when(i == n_tiles - 1)
    def _():
        _row_gather_wait(ybuf, sem, 1 - slot)


def _combine(dest3, y, x1, gates_tk, p2, ple_g, w_gate, w_proj, fin_g, *, tm=256):
    t, d = x1.shape
    n_tiles = t // tm
    last = n_tiles - 1
    n_rows = TOP_K * tm
    ple = p2.shape[1]
    kern = functools.partial(_combine_kernel, tm=tm, n_tiles=n_tiles)
    row = lambda i: (i, 0)
    const = lambda i: (0, 0)
    return pl.pallas_call(
        kern,
        grid=(n_tiles,),
        in_specs=[pl.BlockSpec((1, 1, n_rows), lambda i: (i, 0, 0), memory_space=pltpu.SMEM),
                  pl.BlockSpec((1, 1, n_rows), lambda i: (jnp.minimum(i + 1, last), 0, 0),
                               memory_space=pltpu.SMEM),
                  pl.BlockSpec(memory_space=pl.ANY),
                  pl.BlockSpec((tm, d), row),
                  pl.BlockSpec((tm, TOP_K), row),
                  pl.BlockSpec((tm, ple), row),
                  pl.BlockSpec((1, d), const),
                  pl.BlockSpec((d, d), const),
                  pl.BlockSpec((ple, d), const),
                  pl.BlockSpec((1, d), const)],
        out_specs=pl.BlockSpec((tm, d), row),
        out_shape=jax.ShapeDtypeStruct((t, d), F32),
        scratch_shapes=[pltpu.VMEM((2, n_rows * ROW_CHUNKS, LANES), F32),
                        pltpu.SemaphoreType.DMA((2,))],
        compiler_params=pltpu.CompilerParams(
            dimension_semantics=("arbitrary",), vmem_limit_bytes=VMEM_LIMIT),
        name="combine",
    )(dest3, dest3, y.reshape(-1, ROW_CHUNKS, LANES), x1, gates_tk, p2, ple_g.reshape(1, d),
      w_gate, w_proj, fin_g.reshape(1, d))


def _routing_tables(topi, rank, counts, *, tm_combine):
    k, t = topi.shape
    n_blocks = (k * t) // MOE_ROWS + N_EXPERTS
    padded = (counts + MOE_ROWS - 1) // MOE_ROWS * MOE_ROWS
    pad_end = jnp.cumsum(padded)
    pad_start = pad_end - padded
    eids = jnp.arange(N_EXPERTS, dtype=I32)[:, None, None]
    dest = rank + jnp.sum(jnp.where(topi[None] == eids, pad_start[:, None, None], 0), axis=0)
    n_used = (pad_end[-1] // MOE_ROWS).astype(I32).reshape(1)
    blk_row0 = jnp.arange(n_blocks, dtype=I32) * MOE_ROWS
    blk_expert = jnp.minimum(
        jnp.sum((pad_end[None, :] <= blk_row0[:, None]).astype(I32), axis=1), N_EXPERTS - 1)
    tok = jnp.broadcast_to(jnp.arange(t, dtype=I32)[None, :], (k, t))
    row_tok = jnp.zeros((n_blocks * MOE_ROWS,), I32).at[dest.reshape(-1)].set(tok.reshape(-1))
    row_tok3 = row_tok.reshape(n_blocks, 1, MOE_ROWS)
    n_tiles = t // tm_combine
    dest3 = dest.reshape(k, n_tiles, tm_combine).transpose(1, 0, 2).reshape(
        n_tiles, 1, k * tm_combine)
    return blk_expert, n_used, row_tok3, dest3


SEG_ALIGN = 16
MOE_TILE = 512
LOCAL_ROWS = TOP_K * MOE_TILE + N_EXPERTS * SEG_ALIGN
MAX_CHUNKS = LOCAL_ROWS // SEG_ALIGN
CHUNK_TABLE = 256
TAIL_TABLE = N_EXPERTS * (MOE_ROWS // SEG_ALIGN)


def _chunk_copy(src, dst, sem):
    return pltpu.make_async_copy(src, dst, sem)


def _chunk_wait_n(local_buf, hbm, sem, slot, n, to_hbm):
    loc = local_buf.at[slot, pl.ds(0, SEG_ALIGN), :]
    far = hbm.at[pl.ds(0, SEG_ALIGN), :]

    def body(j, carry):
        (_chunk_copy(loc, far, sem.at[slot]) if to_hbm
         else _chunk_copy(far, loc, sem.at[slot])).wait()
        return carry

    lax.fori_loop(0, n, body, 0)


def _chunk_start_n(local_buf, hbm, tab_ref, sem, slot, n, to_hbm):
    def body(j, carry):
        loc = local_buf.at[slot, pl.ds(pl.multiple_of(j * SEG_ALIGN, SEG_ALIGN), SEG_ALIGN), :]
        far = hbm.at[pl.ds(pl.multiple_of(tab_ref[0, 0, j] * SEG_ALIGN, SEG_ALIGN), SEG_ALIGN), :]
        (_chunk_copy(loc, far, sem.at[slot]) if to_hbm
         else _chunk_copy(far, loc, sem.at[slot])).start()
        return carry

    lax.fori_loop(0, n, body, 0)


def _dispatch_kernel(nch_ref, nused_ref, tab_ref, tail_ref, lp_ref, h2_ref, xs_hbm,
                     buf, zbuf, sem, zsem, *, n_tiles, n_blocks):
    i = pl.program_id(0)
    slot = i % 2

    @pl.when(i >= 2)
    def _():
        _chunk_wait_n(buf, xs_hbm, sem, slot, nch_ref[jnp.maximum(i - 2, 0)], True)

    lp = lp_ref[...]
    tm = lp.shape[1]
    rows = lax.broadcasted_iota(I32, (LOCAL_ROWS, tm), 0)
    onehot = jnp.zeros((LOCAL_ROWS, tm), F32)
    for k in range(TOP_K):
        onehot = jnp.where(rows == lp[k:k + 1, :], 1.0, onehot)
    buf[slot] = jnp.dot(onehot.astype(BF16), h2_ref[...],
                        preferred_element_type=F32).astype(BF16)
    _chunk_start_n(buf, xs_hbm, tab_ref, sem, slot, nch_ref[i], True)

    @pl.when(i == n_tiles - 1)
    def _():
        zbuf[...] = jnp.zeros_like(zbuf)
        zrow = zbuf.at[pl.ds(0, SEG_ALIGN), :]

        def fill(j, carry):
            @pl.when(tail_ref[0, 0, j] >= 0)
            def _():
                row0 = pl.multiple_of(tail_ref[0, 0, j] * SEG_ALIGN, SEG_ALIGN)
                _chunk_copy(zrow, xs_hbm.at[pl.ds(row0, SEG_ALIGN), :], zsem).start()
            return carry

        def drain(j, carry):
            @pl.when(tail_ref[0, 0, j] >= 0)
            def _():
                _chunk_copy(zrow, xs_hbm.at[pl.ds(0, SEG_ALIGN), :], zsem).wait()
            return carry

        def fill_block(b, carry):
            row0 = pl.multiple_of(b * MOE_ROWS, MOE_ROWS)
            _chunk_copy(zbuf, xs_hbm.at[pl.ds(row0, MOE_ROWS), :], zsem).start()
            return carry

        def drain_block(b, carry):
            _chunk_copy(zbuf, xs_hbm.at[pl.ds(0, MOE_ROWS), :], zsem).wait()
            return carry

        lax.fori_loop(0, TAIL_TABLE, fill, 0)
        lax.fori_loop(0, TAIL_TABLE, drain, 0)
        lax.fori_loop(nused_ref[0], n_blocks, fill_block, 0)
        lax.fori_loop(nused_ref[0], n_blocks, drain_block, 0)
        _chunk_wait_n(buf, xs_hbm, sem, slot, nch_ref[i], True)
        if n_tiles > 1:
            _chunk_wait_n(buf, xs_hbm, sem, 1 - slot, nch_ref[jnp.maximum(i - 1, 0)], True)


def _dispatch(n_chunks, n_used, chunk_tab, tail_tab, lp, h2, n_blocks):
    k, t = lp.shape
    d = h2.shape[1]
    n_tiles = t // MOE_TILE
    grid_spec = pltpu.PrefetchScalarGridSpec(
        num_scalar_prefetch=2,
        grid=(n_tiles,),
        in_specs=[pl.BlockSpec((1, 1, CHUNK_TABLE), lambda i, nc, nu: (i, 0, 0),
                               memory_space=pltpu.SMEM),
                  pl.BlockSpec((1, 1, TAIL_TABLE), lambda i, nc, nu: (0, 0, 0),
                               memory_space=pltpu.SMEM),
                  pl.BlockSpec((k, MOE_TILE), lambda i, nc, nu: (0, i)),
                  pl.BlockSpec((MOE_TILE, d), lambda i, nc, nu: (i, 0))],
        out_specs=pl.BlockSpec(memory_space=pl.ANY),
        scratch_shapes=[pltpu.VMEM((2, LOCAL_ROWS, d), BF16), pltpu.VMEM((MOE_ROWS, d), BF16),
                        pltpu.SemaphoreType.DMA((2,)), pltpu.SemaphoreType.DMA],
    )
    return pl.pallas_call(
        functools.partial(_dispatch_kernel, n_tiles=n_tiles, n_blocks=n_blocks),
        grid_spec=grid_spec,
        out_shape=jax.ShapeDtypeStruct((n_blocks * MOE_ROWS, d), BF16),
        compiler_params=pltpu.CompilerParams(
            dimension_semantics=("arbitrary",), vmem_limit_bytes=VMEM_LIMIT),
        name="dispatch",
    )(n_chunks, n_used, chunk_tab, tail_tab, lp, h2)


def _expert_mlp_kernel(blk_e_ref, nused_ref, x_ref, wgu_ref, bgu_ref, wdn_ref, bdn_ref, y_ref,
                       wgu_bf, wdn_bf):
    i = pl.program_id(0)
    nused = nused_ref[0]
    new_expert = jnp.logical_or(i == 0, blk_e_ref[i] != blk_e_ref[jnp.maximum(i - 1, 0)])

    @pl.when(jnp.logical_and(new_expert, i < nused))
    def _():
        wgu_bf[...] = wgu_ref[...].astype(BF16)
        wdn_bf[...] = wdn_ref[...].astype(BF16)

    @pl.when(i < nused)
    def _():
        x = x_ref[...]
        gu = jnp.dot(x, wgu_bf[...], preferred_element_type=F32) + bgu_ref[...]
        d_ff = gu.shape[1] // 2
        gate = jnp.minimum(gu[:, :d_ff], SWIGLU_LIMIT)
        up = jnp.clip(gu[:, d_ff:], -SWIGLU_LIMIT, SWIGLU_LIMIT)
        act = gate * _sigmoid(SWIGLU_ALPHA * gate) * (up + 1.0)
        y = jnp.dot(act.astype(BF16), wdn_bf[...], preferred_element_type=F32) + bdn_ref[...]
        y_ref[...] = y.astype(BF16)

    @pl.when(i >= nused)
    def _():
        y_ref[...] = jnp.zeros_like(y_ref)


def _expert_mlp(blk_expert, n_used, xs, w_gu, b_gu, w_dn, b_dn):
    d = xs.shape[1]
    n_blocks = blk_expert.shape[0]
    n_rows = n_blocks * MOE_ROWS
    e, _, f2 = w_gu.shape
    grid_spec = pltpu.PrefetchScalarGridSpec(
        num_scalar_prefetch=2,
        grid=(n_blocks,),
        in_specs=[
            pl.BlockSpec((MOE_ROWS, d), lambda i, be, nu: (jnp.minimum(i, nu[0] - 1), 0)),
            pl.BlockSpec((None, d, f2), lambda i, be, nu: (be[i], 0, 0)),
            pl.BlockSpec((None, 1, f2), lambda i, be, nu: (be[i], 0, 0)),
            pl.BlockSpec((None, f2 // 2, d), lambda i, be, nu: (be[i], 0, 0)),
            pl.BlockSpec((None, 1, d), lambda i, be, nu: (be[i], 0, 0)),
        ],
        out_specs=pl.BlockSpec((MOE_ROWS, d), lambda i, be, nu: (i, 0)),
        scratch_shapes=[pltpu.VMEM((d, f2), BF16), pltpu.VMEM((f2 // 2, d), BF16)],
    )
    return pl.pallas_call(
        _expert_mlp_kernel,
        grid_spec=grid_spec,
        out_shape=jax.ShapeDtypeStruct((n_rows, d), BF16),
        compiler_params=pltpu.CompilerParams(
            dimension_semantics=("arbitrary",), vmem_limit_bytes=VMEM_LIMIT),
        name="experts",
    )(blk_expert, n_used, xs, w_gu, b_gu.reshape(e, 1, f2), w_dn, b_dn.reshape(e, 1, d))


def _moe_combine_kernel(nch_ref, tab_cur_ref, tab_nxt_ref, y_hbm, lpt_ref, gate_ref, x1_ref,
                        p_ref, gp_ref, wg_ref, wp_ref, gfin_ref, o_ref, buf, sem, *, n_tiles):
    i = pl.program_id(0)
    slot = i % 2

    @pl.when(i == 0)
    def _():
        buf[...] = jnp.zeros_like(buf)
        _chunk_start_n(buf, y_hbm, tab_cur_ref, sem, 0, nch_ref[0], False)

    @pl.when(i + 1 < n_tiles)
    def _():
        _chunk_start_n(buf, y_hbm, tab_nxt_ref, sem, 1 - slot,
                       nch_ref[jnp.minimum(i + 1, n_tiles - 1)], False)

    _chunk_wait_n(buf, y_hbm, sem, slot, nch_ref[i], False)
    lpt = lpt_ref[...]
    gates = gate_ref[...]
    tm = lpt.shape[0]
    cols = lax.broadcasted_iota(I32, (tm, LOCAL_ROWS), 1)
    weights = jnp.zeros((tm, LOCAL_ROWS), F32)
    for k in range(TOP_K):
        weights = jnp.where(cols == lpt[:, k:k + 1], gates[:, k:k + 1], weights)
    x2 = x1_ref[...] + jnp.dot(weights.astype(BF16), buf[slot], preferred_element_type=F32)
    hg = _rms(x2, gp_ref[...]).astype(BF16)
    gate = _sigmoid(jnp.dot(hg, wg_ref[...], preferred_element_type=F32))
    emb = jnp.dot(p_ref[...].astype(BF16), wp_ref[...], preferred_element_type=F32)
    x3 = x2 + gate * emb
    o_ref[...] = _rms(x3, gfin_ref[...])


def _moe_combine(n_chunks, chunk_tab, y, lpt, gates_tk, x1, p2, ple_g, w_gate, w_proj, fin_g):
    t, d = x1.shape
    n_tiles = t // MOE_TILE
    last = n_tiles - 1
    ple = p2.shape[1]
    row = lambda i, nc: (i, 0)
    const = lambda i, nc: (0, 0)
    grid_spec = pltpu.PrefetchScalarGridSpec(
        num_scalar_prefetch=1,
        grid=(n_tiles,),
        in_specs=[pl.BlockSpec((1, 1, CHUNK_TABLE), lambda i, nc: (i, 0, 0),
                               memory_space=pltpu.SMEM),
                  pl.BlockSpec((1, 1, CHUNK_TABLE),
                               lambda i, nc: (jnp.minimum(i + 1, last), 0, 0),
                               memory_space=pltpu.SMEM),
                  pl.BlockSpec(memory_space=pl.ANY),
                  pl.BlockSpec((MOE_TILE, TOP_K), row),
                  pl.BlockSpec((MOE_TILE, TOP_K), row),
                  pl.BlockSpec((MOE_TILE, d), row),
                  pl.BlockSpec((MOE_TILE, ple), row),
                  pl.BlockSpec((1, d), const),
                  pl.BlockSpec((d, d), const),
                  pl.BlockSpec((ple, d), const),
                  pl.BlockSpec((1, d), const)],
        out_specs=pl.BlockSpec((MOE_TILE, d), row),
        scratch_shapes=[pltpu.VMEM((2, LOCAL_ROWS, d), BF16), pltpu.SemaphoreType.DMA((2,))],
    )
    return pl.pallas_call(
        functools.partial(_moe_combine_kernel, n_tiles=n_tiles),
        grid_spec=grid_spec,
        out_shape=jax.ShapeDtypeStruct((t, d), F32),
        compiler_params=pltpu.CompilerParams(
            dimension_semantics=("arbitrary",), vmem_limit_bytes=VMEM_LIMIT),
        name="combine",
    )(n_chunks, chunk_tab, chunk_tab, y, lpt, gates_tk, x1, p2, ple_g.reshape(1, d),
      w_gate, w_proj, fin_g.reshape(1, d))


def _moe_tables(topi, lrank, cnt):
    k, t = topi.shape
    n_tiles = t // MOE_TILE
    cnt = cnt.astype(I32)
    pc = (cnt + SEG_ALIGN - 1) // SEG_ALIGN * SEG_ALIGN
    l_end = jnp.cumsum(pc, axis=1)
    l_start = l_end - pc
    tot = jnp.sum(pc, axis=0)
    e_pad = (tot + MOE_ROWS - 1) // MOE_ROWS * MOE_ROWS
    e_end = jnp.cumsum(e_pad)
    seg_row0 = (e_end - e_pad)[None, :] + jnp.cumsum(pc, axis=0) - pc
    eids = jnp.arange(N_EXPERTS, dtype=I32)[:, None, None, None]
    topi3 = topi.reshape(k, n_tiles, MOE_TILE)
    base = jnp.sum(jnp.where(topi3[None] == eids, l_start.T[:, None, :, None], 0), axis=0)
    lp = (lrank.reshape(k, n_tiles, MOE_TILE) + base).reshape(k, t)
    j16 = jnp.arange(CHUNK_TABLE, dtype=I32)[None, :] * SEG_ALIGN
    ce = jnp.minimum(jnp.sum((l_end[:, :, None] <= j16[:, None, :]).astype(I32), axis=1),
                     N_EXPERTS - 1)
    pick = ce[:, None, :] == jnp.arange(N_EXPERTS, dtype=I32)[None, :, None]
    seg0 = jnp.sum(jnp.where(pick, seg_row0[:, :, None], 0), axis=1)
    loc0 = jnp.sum(jnp.where(pick, l_start[:, :, None], 0), axis=1)
    n_chunks = l_end[:, -1] // SEG_ALIGN
    valid = jnp.arange(CHUNK_TABLE, dtype=I32)[None, :] < n_chunks[:, None]
    tab = jnp.where(valid, (seg0 + j16 - loc0) // SEG_ALIGN, 0).astype(I32)
    n_rows_max = k * t + n_tiles * N_EXPERTS * (SEG_ALIGN - 1) + N_EXPERTS * (MOE_ROWS - 1)
    n_blocks = (n_rows_max + MOE_ROWS - 1) // MOE_ROWS
    n_used = (e_end[-1] // MOE_ROWS).astype(I32).reshape(1)
    blk_row0 = jnp.arange(n_blocks, dtype=I32) * MOE_ROWS
    blk_expert = jnp.minimum(
        jnp.sum((e_end[None, :] <= blk_row0[:, None]).astype(I32), axis=1), N_EXPERTS - 1)
    per_e = MOE_ROWS // SEG_ALIGN
    tj = jnp.arange(per_e, dtype=I32)[None, :]
    tail = jnp.where(tj < ((e_pad - tot) // SEG_ALIGN)[:, None],
                     ((e_end - e_pad + tot) // SEG_ALIGN)[:, None] + tj, -1).astype(I32)
    return (lp, n_chunks.astype(I32), tab.reshape(n_tiles, 1, CHUNK_TABLE),
            tail.reshape(1, 1, TAIL_TABLE), blk_expert, n_used)


def _layer(x2, p2, seq, norm_mix_g, w_in, b_f, lam_re, lam_im, log_dt, b_re, b_im, c_re, c_im,
           d_skip, w_glu, b_glu, attn_out_g, ssm_out_g, w_out, norm_ffn_g, w_router, b_router,
           w_gu, b_gu, w_dn, b_dn, norm_ple_g, w_ple_gate, w_ple_proj, final_g):
    t, d = x2.shape
    nb = t // seq
    aw = ATTN_WIDTH
    w_main = jnp.concatenate([w_in[:, :3 * aw], w_in[:, 3 * aw + N_HEADS:]], axis=1).astype(BF16)
    wft = w_in[:, 3 * aw:3 * aw + N_HEADS].T.astype(BF16)

    a_re, a_im, bbr, bbi = _ssm_prep(lam_re, lam_im, log_dt, b_re, b_im)
    bre = _block_diag(bbr, True).astype(BF16)
    bim = _block_diag(bbi, True).astype(BF16)
    cre = _block_diag(jnp.transpose(c_re, (0, 2, 1)), False).astype(BF16)
    cim = _block_diag(jnp.transpose(c_im, (0, 2, 1)), False).astype(BF16)

    q, k, v, u, c_t = _in_proj(x2, norm_mix_g, w_main, wft, b_f, seq=seq)
    shp = (nb, seq, aw)
    c3 = c_t.reshape(N_HEADS // 2, 2, t)
    attn = _attention(q.reshape(shp), k.reshape(shp), v.reshape(shp), c3)
    ssm = _ssm(u.reshape(nb, seq, SSM_WIDTH), bre, bim, cre, cim, a_re, a_im,
               d_skip, w_glu.astype(BF16), b_glu, ssm_out_g)

    x1, h2, topi, gates, lrank, cnt = _out_proj(
        x2, attn.reshape(t, aw), ssm.reshape(t, SSM_WIDTH), attn_out_g,
        w_out[:aw].astype(BF16), w_out[aw:].astype(BF16), norm_ffn_g,
        w_router.T.astype(BF16), b_router, tm=MOE_TILE)

    lp, n_chunks, chunk_tab, tail_tab, blk_expert, n_used = _moe_tables(topi, lrank, cnt[:, :, 0])
    xs = _dispatch(n_chunks, n_used, chunk_tab, tail_tab, lp, h2, blk_expert.shape[0])
    y = _expert_mlp(blk_expert, n_used, xs, w_gu, b_gu, w_dn, b_dn)
    return _moe_combine(n_chunks, chunk_tab, y, lp.T, gates.T, x1, p2, norm_ple_g,
                        w_ple_gate.astype(BF16), w_ple_proj.astype(BF16), final_g)


def kernel(x, p, norm_mix_g, w_in, b_f, lam_re, lam_im, log_dt, b_re, b_im, c_re, c_im, d_skip, w_glu, b_glu, attn_out_g, ssm_out_g, w_out, norm_ffn_g, w_router, b_router, w_gu, b_gu, w_dn, b_dn, norm_ple_g, w_ple_gate, w_ple_proj, norm_final_g):
    bsz, seq, d = x.shape
    depth = w_in.shape[0]
    assert depth == 1, "one layer: the final rmsnorm is fused into the layer's last kernel"
    out = _layer(x.reshape(bsz * seq, d), p[0].reshape(bsz * seq, -1), seq,
                 norm_mix_g[0], w_in[0], b_f[0], lam_re[0], lam_im[0], log_dt[0],
                 b_re[0], b_im[0], c_re[0], c_im[0], d_skip[0], w_glu[0], b_glu[0],
                 attn_out_g[0], ssm_out_g[0], w_out[0], norm_ffn_g[0], w_router[0],
                 b_router[0], w_gu[0], b_gu[0], w_dn[0], b_dn[0], norm_ple_g[0],
                 w_ple_gate[0], w_ple_proj[0], norm_final_g)
    return out.reshape(bsz, seq, d)
```

```python
import functools
import math

import jax
import jax.numpy as jnp
from jax import lax
from jax.experimental import pallas as pl
from jax.experimental.pallas import tpu as pltpu

F32 = jnp.float32
BF16 = jnp.bfloat16
I32 = jnp.int32

NORM_EPS = 1e-5
HEAD_DIM = 64
N_HEADS = 8
ATTN_WIDTH = 512
SSM_WIDTH = 512
SSM_GROUP = 16
N_SSM_GROUPS = 32
SSM_STATE = 64
N_STATE = N_SSM_GROUPS * SSM_STATE
N_EXPERTS = 32
TOP_K = 4
SWIGLU_LIMIT = 7.0
SWIGLU_ALPHA = 1.702
LANES = 128
SUBLANES = 8
ROW_CHUNKS = 8
MOE_ROWS = 512
NEG_BIG = -1e30
LOG2E = math.log2(math.e)
VMEM_LIMIT = 56 * 1024 * 1024

_NT = (((1,), (1,)), ((), ()))


def _rms(xf, g):
    ms = jnp.mean(xf * xf, axis=-1, keepdims=True)
    return xf * lax.rsqrt(ms + NORM_EPS) * g


def _sigmoid(x):
    return 1.0 / (1.0 + jnp.exp(-x))


def _ssm_prep_kernel(lr_ref, li_ref, ldt_ref, brt_ref, bit_ref,
                     ar_ref, ai_ref, bbr_ref, bbi_ref):
    lr = lr_ref[...]
    li = li_ref[...]
    dt = jnp.exp(ldt_ref[...])
    mag = jnp.exp(lr * dt)
    ab_re = mag * jnp.cos(li * dt)
    ab_im = mag * jnp.sin(li * dt)
    den = lr * lr + li * li
    nr = ab_re - 1.0
    z_re = (nr * lr + ab_im * li) / den
    z_im = (ab_im * lr - nr * li) / den
    ar_ref[...] = ab_re
    ai_ref[...] = ab_im
    br = brt_ref[...]
    bi = bit_ref[...]
    bbr_ref[...] = z_re * br - z_im * bi
    bbi_ref[...] = z_re * bi + z_im * br


def _ssm_prep(lam_re, lam_im, log_dt, b_re, b_im):
    g, p, c = b_re.shape
    brt = jnp.transpose(b_re, (0, 2, 1))
    bit = jnp.transpose(b_im, (0, 2, 1))
    return pl.pallas_call(
        _ssm_prep_kernel,
        out_shape=(jax.ShapeDtypeStruct((g, 1, p), F32), jax.ShapeDtypeStruct((g, 1, p), F32),
                   jax.ShapeDtypeStruct((g, c, p), F32), jax.ShapeDtypeStruct((g, c, p), F32)),
        name="ssm_prep",
    )(lam_re.reshape(g, 1, p), lam_im.reshape(g, 1, p), log_dt.reshape(g, 1, 1), brt, bit)


def _block_diag(w, rows_first):
    g, a, b = w.shape
    half = g // 2
    eye = jnp.eye(half, dtype=w.dtype)
    w4 = w.reshape(2, half, a, b)
    out = jnp.einsum('hgab,gk->hgakb', w4, eye)
    del rows_first
    return out.reshape(2, half * a, half * b)


def _inproj_kernel(x_ref, g_ref, w_ref, wft_ref, bf_ref, tri_ref,
                   q_ref, k_ref, v_ref, u_ref, c_ref, carry_ref, *, tiles_per_seq, tm):
    i = pl.program_id(0)

    @pl.when(i % tiles_per_seq == 0)
    def _():
        carry_ref[...] = jnp.zeros_like(carry_ref)

    h = _rms(x_ref[...], g_ref[...]).astype(BF16)
    proj = jnp.dot(h, w_ref[...], preferred_element_type=F32)
    aw = ATTN_WIDTH
    q_ref[...] = (proj[:, 0:aw] * (LOG2E * HEAD_DIM ** -0.5)).astype(BF16)
    k_ref[...] = proj[:, aw:2 * aw].astype(BF16)
    v_ref[...] = proj[:, 2 * aw:3 * aw].astype(BF16)
    u_ref[...] = proj[:, 3 * aw:3 * aw + SSM_WIDTH].astype(BF16)

    fl = lax.dot_general(wft_ref[...], h, _NT, preferred_element_type=F32)
    z = fl + bf_ref[...]
    lf = jnp.minimum(z, 0.0) - jnp.log1p(jnp.exp(-jnp.abs(z)))
    hi = lf.astype(BF16)
    lo = (lf - hi.astype(F32)).astype(BF16)
    tri = tri_ref[...]
    cs = (jnp.dot(hi, tri, preferred_element_type=F32)
          + jnp.dot(lo, tri, preferred_element_type=F32))
    c = cs + carry_ref[:, 0:1]
    c_ref[...] = c * LOG2E
    carry_ref[...] = jnp.broadcast_to(c[:, tm - 1:tm], carry_ref.shape)


def _in_proj(x2, norm_g, w_main, wft, b_f, *, seq, tm=512):
    t, d = x2.shape
    n_main = w_main.shape[1]
    tri = jnp.triu(jnp.ones((tm, tm), F32)).astype(BF16)
    kern = functools.partial(_inproj_kernel, tiles_per_seq=seq // tm, tm=tm)
    row = lambda i: (i, 0)
    const = lambda i: (0, 0)
    act = jax.ShapeDtypeStruct((t, ATTN_WIDTH), BF16)
    return pl.pallas_call(
        kern,
        grid=(t // tm,),
        in_specs=[pl.BlockSpec((tm, d), row),
                  pl.BlockSpec((1, d), const),
                  pl.BlockSpec((d, n_main), const),
                  pl.BlockSpec((N_HEADS, d), const),
                  pl.BlockSpec((N_HEADS, 1), const),
                  pl.BlockSpec((tm, tm), const)],
        out_specs=[pl.BlockSpec((tm, ATTN_WIDTH), row)] * 4
        + [pl.BlockSpec((N_HEADS, tm), lambda i: (0, i))],
        out_shape=[act, act, act, act, jax.ShapeDtypeStruct((N_HEADS, t), F32)],
        scratch_shapes=[pltpu.VMEM((N_HEADS, LANES), F32)],
        compiler_params=pltpu.CompilerParams(
            dimension_semantics=("arbitrary",), vmem_limit_bytes=VMEM_LIMIT),
        name="in_proj",
    )(x2, norm_g.reshape(1, d), w_main, wft, b_f.reshape(N_HEADS, 1), tri)


def _attn_kernel(q_ref, k_ref, v_ref, c_ref, o_ref, *, tq):
    i = pl.program_id(2)
    q2 = q_ref[...]
    lane = lax.broadcasted_iota(I32, (1, LANES), 1)
    first = lane < HEAD_DIM
    zero = jnp.zeros_like(q2)
    q_heads = (jnp.where(first, q2, zero), jnp.where(first, zero, q2))

    half = tq // 2

    def block(off, width, r0, carry, masked):
        kj = k_ref[pl.ds(off, width), :]
        vj = v_ref[pl.ds(off, width), :]
        cj = c_ref[:, pl.ds(off, width)]
        one = jnp.ones_like(vj)
        v_heads = (jnp.where(first, vj, one), jnp.where(first, one, vj))
        out = []
        for h in range(2):
            m, acc = carry[h]
            s = (lax.dot_general(q_heads[h][r0:], kj, _NT, preferred_element_type=F32)
                 - cj[h:h + 1, :])
            if masked:
                rr = lax.broadcasted_iota(I32, s.shape, 0)
                cc = lax.broadcasted_iota(I32, s.shape, 1)
                s = jnp.where(cc <= rr, s, NEG_BIG)
            m_new = jnp.maximum(m[r0:], jnp.max(s, axis=-1, keepdims=True))
            alpha = jnp.exp2(m[r0:] - m_new)
            p = jnp.exp2(s - m_new).astype(BF16)
            acc_new = alpha * acc[r0:] + jnp.dot(p, v_heads[h], preferred_element_type=F32)
            if r0:
                m_new = jnp.concatenate([m[:r0], m_new], axis=0)
                acc_new = jnp.concatenate([acc[:r0], acc_new], axis=0)
            out.append((m_new, acc_new))
        return tuple(out)

    init_one = (jnp.full((tq, 1), NEG_BIG, F32), jnp.zeros((tq, LANES), F32))
    carry = lax.fori_loop(
        0, i, lambda j, c: block(pl.multiple_of(j * tq, tq), tq, 0, c, False),
        (init_one, init_one))
    diag = pl.multiple_of(i * tq, tq)
    carry = block(diag, half, 0, carry, True)
    (_, acc_a), (_, acc_b) = block(pl.multiple_of(diag + half, half), half, half, carry, True)
    o = jnp.where(first, acc_a / pltpu.roll(acc_a, HEAD_DIM, axis=1),
                  acc_b / pltpu.roll(acc_b, HEAD_DIM, axis=1))
    o_ref[...] = o.astype(BF16)


def _attention(q, k, v, c3, *, tq=1024):
    b, s, w = q.shape
    n_pairs = w // LANES
    kern = functools.partial(_attn_kernel, tq=tq)
    return pl.pallas_call(
        kern,
        grid=(b, n_pairs, s // tq),
        in_specs=[pl.BlockSpec((None, tq, LANES), lambda bi, hp, i: (bi, i, hp)),
                  pl.BlockSpec((None, s, LANES), lambda bi, hp, i: (bi, 0, hp)),
                  pl.BlockSpec((None, s, LANES), lambda bi, hp, i: (bi, 0, hp)),
                  pl.BlockSpec((None, 2, s), lambda bi, hp, i: (hp, 0, bi))],
        out_specs=pl.BlockSpec((None, tq, LANES), lambda bi, hp, i: (bi, i, hp)),
        out_shape=jax.ShapeDtypeStruct((b, s, w), BF16),
        compiler_params=pltpu.CompilerParams(
            dimension_semantics=("arbitrary", "arbitrary", "arbitrary"),
            vmem_limit_bytes=VMEM_LIMIT),
        name="attention",
    )(q, k, v, c3)


def _ssm_kernel(u_ref, bre_ref, bim_ref, cre_ref, cim_ref, ar_ref, ai_ref, dsk_ref,
                wglu_ref, bglu_ref, g_ref, o_ref,
                us_ref, xr_ref, xi_ref, str_ref, sti_ref, res_ref, *, tt, nb):
    i = pl.program_id(0)

    @pl.when(i == 0)
    def _():
        str_ref[...] = jnp.zeros_like(str_ref)
        sti_ref[...] = jnp.zeros_like(sti_ref)

    n_chunks = SSM_WIDTH // LANES
    for b in range(nb):
        ub32 = u_ref[b].astype(F32)
        for c in range(n_chunks):
            us_ref[c, pl.ds(b, tt, stride=nb), :] = ub32[:, c * LANES:(c + 1) * LANES]
    uf = jnp.concatenate([us_ref[c] for c in range(n_chunks)], axis=-1)
    ub = uf.astype(BF16)
    half_in = SSM_WIDTH // 2
    half_st = N_STATE // 2
    for hf in range(2):
        uh = ub[:, hf * half_in:(hf + 1) * half_in]
        xr_ref[:, hf * half_st:(hf + 1) * half_st] = jnp.dot(
            uh, bre_ref[hf], preferred_element_type=F32)
        xi_ref[:, hf * half_st:(hf + 1) * half_st] = jnp.dot(
            uh, bim_ref[hf], preferred_element_type=F32)

    n_col_groups = 2
    wcol = N_STATE // n_col_groups
    unroll = 4
    for cg in range(n_col_groups):
        cols = slice(cg * wcol, (cg + 1) * wcol)
        ar = jnp.broadcast_to(ar_ref[:, cols], (nb, wcol))
        ai = jnp.broadcast_to(ai_ref[:, cols], (nb, wcol))

        def steps(tb, carry, cols=cols, ar=ar, ai=ai):
            sr, si = carry
            for k in range(unroll):
                r0 = pl.multiple_of((tb * unroll + k) * nb, nb)
                br = xr_ref[pl.ds(r0, nb), cols]
                bi = xi_ref[pl.ds(r0, nb), cols]
                nr = ar * sr - ai * si + br
                ni = ar * si + ai * sr + bi
                xr_ref[pl.ds(r0, nb), cols] = nr
                xi_ref[pl.ds(r0, nb), cols] = ni
                sr, si = nr, ni
            return sr, si

        sr, si = lax.fori_loop(0, tt // unroll, steps, (str_ref[:, cols], sti_ref[:, cols]))
        str_ref[:, cols] = sr
        sti_ref[:, cols] = si

    ys = []
    for hf in range(2):
        xr = xr_ref[:, hf * half_st:(hf + 1) * half_st].astype(BF16)
        xi = xi_ref[:, hf * half_st:(hf + 1) * half_st].astype(BF16)
        ys.append(jnp.dot(xr, cre_ref[hf], preferred_element_type=F32)
                  - jnp.dot(xi, cim_ref[hf], preferred_element_type=F32))
    y = jnp.concatenate(ys, axis=-1) + dsk_ref[...] * uf
    gl = 0.5 * y * (1.0 + jnp.tanh(math.sqrt(2.0 / math.pi) * (y + 0.044715 * (y * y * y))))
    zz = jnp.dot(gl.astype(BF16), wglu_ref[...], preferred_element_type=F32) + bglu_ref[...]
    out = gl * _sigmoid(zz)
    res = _rms(out, g_ref[...])
    for c in range(n_chunks):
        res_ref[c] = res[:, c * LANES:(c + 1) * LANES]
    for b in range(nb):
        o_ref[b] = jnp.concatenate(
            [res_ref[c, pl.ds(b, tt, stride=nb), :] for c in range(n_chunks)],
            axis=-1).astype(BF16)


def _ssm(u3, bre, bim, cre, cim, a_re, a_im, d_skip, w_glu, b_glu, out_g, *, tt=64):
    nb, s, w = u3.shape
    rows = tt * nb
    kern = functools.partial(_ssm_kernel, tt=tt, nb=nb)
    c3 = lambda i: (0, 0, 0)
    c2 = lambda i: (0, 0)
    return pl.pallas_call(
        kern,
        grid=(s // tt,),
        in_specs=[pl.BlockSpec((nb, tt, w), lambda i: (0, i, 0)),
                  pl.BlockSpec(bre.shape, c3), pl.BlockSpec(bim.shape, c3),
                  pl.BlockSpec(cre.shape, c3), pl.BlockSpec(cim.shape, c3),
                  pl.BlockSpec((1, N_STATE), c2), pl.BlockSpec((1, N_STATE), c2),
                  pl.BlockSpec((1, w), c2),
                  pl.BlockSpec((w, w), c2), pl.BlockSpec((1, w), c2), pl.BlockSpec((1, w), c2)],
        out_specs=pl.BlockSpec((nb, tt, w), lambda i: (0, i, 0)),
        out_shape=jax.ShapeDtypeStruct((nb, s, w), BF16),
        scratch_shapes=[pltpu.VMEM((w // LANES, rows, LANES), F32),
                        pltpu.VMEM((rows, N_STATE), F32), pltpu.VMEM((rows, N_STATE), F32),
                        pltpu.VMEM((nb, N_STATE), F32), pltpu.VMEM((nb, N_STATE), F32),
                        pltpu.VMEM((w // LANES, rows, LANES), F32)],
        compiler_params=pltpu.CompilerParams(
            dimension_semantics=("arbitrary",), vmem_limit_bytes=VMEM_LIMIT),
        name="ssm",
    )(u3, bre, bim, cre, cim, a_re.reshape(1, N_STATE), a_im.reshape(1, N_STATE),
      d_skip.reshape(1, w), w_glu, b_glu.reshape(1, w), out_g.reshape(1, w))


def _outproj_kernel(x_ref, a_ref, s_ref, ga_ref, woa_ref, wos_ref, gf_ref, wrt_ref, br_ref,
                    tri_ref, x1_ref, h2_ref, topi_ref, gate_ref, rank_ref, cnt_ref, *, tm):
    a = _rms(a_ref[...].astype(F32), ga_ref[...]).astype(BF16)
    x1 = (x_ref[...] + jnp.dot(a, woa_ref[...], preferred_element_type=F32)
          + jnp.dot(s_ref[...], wos_ref[...], preferred_element_type=F32))
    x1_ref[...] = x1
    h2 = _rms(x1, gf_ref[...]).astype(BF16)
    h2_ref[...] = h2

    lg = lax.dot_general(wrt_ref[...], h2, _NT,
                         preferred_element_type=F32) + br_ref[...]
    ids = lax.broadcasted_iota(I32, (N_EXPERTS, tm), 0)
    work = lg
    vals, idxs = [], []
    for _ in range(TOP_K):
        m = jnp.max(work, axis=0, keepdims=True)
        idx = jnp.min(jnp.where(work == m, ids, N_EXPERTS), axis=0, keepdims=True)
        vals.append(m)
        idxs.append(idx)
        work = jnp.where(ids == idx, -jnp.inf, work)
    exps = [jnp.exp(v - vals[0]) for v in vals]
    den = exps[0] + exps[1] + exps[2] + exps[3]
    gate_ref[...] = jnp.concatenate([e / den for e in exps], axis=0)
    topi_ref[...] = jnp.concatenate(idxs, axis=0)

    sel = jnp.zeros((N_EXPERTS, tm), F32)
    for idx in idxs:
        sel = sel + jnp.where(ids == idx, 1.0, 0.0)
    before = jnp.dot(sel.astype(BF16), tri_ref[...], preferred_element_type=F32)
    ranks = [jnp.sum(jnp.where(ids == idx, before, 0.0), axis=0, keepdims=True) for idx in idxs]
    rank_ref[...] = jnp.concatenate(ranks, axis=0).astype(I32)
    cnt_ref[0] = jnp.broadcast_to(jnp.sum(sel, axis=1, keepdims=True), (N_EXPERTS, LANES))


def _out_proj(x2, attn, ssm, attn_g, wo_a, wo_s, ffn_g, wrt, b_router, *, tm=512):
    t, d = x2.shape
    tri = jnp.triu(jnp.ones((tm, tm), F32), k=1).astype(BF16)
    kern = functools.partial(_outproj_kernel, tm=tm)
    row = lambda i: (i, 0)
    const = lambda i: (0, 0)
    colblk = lambda i: (0, i)
    return pl.pallas_call(
        kern,
        grid=(t // tm,),
        in_specs=[pl.BlockSpec((tm, d), row),
                  pl.BlockSpec((tm, ATTN_WIDTH), row),
                  pl.BlockSpec((tm, SSM_WIDTH), row),
                  pl.BlockSpec((1, ATTN_WIDTH), const),
                  pl.BlockSpec((ATTN_WIDTH, d), const),
                  pl.BlockSpec((SSM_WIDTH, d), const),
                  pl.BlockSpec((1, d), const),
                  pl.BlockSpec((N_EXPERTS, d), const),
                  pl.BlockSpec((N_EXPERTS, 1), const),
                  pl.BlockSpec((tm, tm), const)],
        out_specs=[pl.BlockSpec((tm, d), row),
                   pl.BlockSpec((tm, d), row),
                   pl.BlockSpec((TOP_K, tm), colblk),
                   pl.BlockSpec((TOP_K, tm), colblk),
                   pl.BlockSpec((TOP_K, tm), colblk),
                   pl.BlockSpec((1, N_EXPERTS, LANES), lambda i: (i, 0, 0))],
        out_shape=[jax.ShapeDtypeStruct((t, d), F32),
                   jax.ShapeDtypeStruct((t, d), BF16),
                   jax.ShapeDtypeStruct((TOP_K, t), I32),
                   jax.ShapeDtypeStruct((TOP_K, t), F32),
                   jax.ShapeDtypeStruct((TOP_K, t), I32),
                   jax.ShapeDtypeStruct((t // tm, N_EXPERTS, LANES), F32)],
        compiler_params=pltpu.CompilerParams(
            dimension_semantics=("arbitrary",), vmem_limit_bytes=VMEM_LIMIT),
        name="out_proj",
    )(x2, attn, ssm, attn_g.reshape(1, -1), wo_a, wo_s, ffn_g.reshape(1, d), wrt,
      b_router.reshape(N_EXPERTS, 1), tri)


def _row_gather_start(src_hbm, idx_ref, buf, sem, slot, rows):
    for r in rows:
        if r % 16:
            continue
        t0 = jnp.minimum(idx_ref[0, 0, r], src_hbm.shape[0] - 16)
        pltpu.make_async_copy(
            src_hbm.at[pl.ds(t0, 16)].reshape(16 * ROW_CHUNKS, LANES),
            buf.at[slot, pl.ds(r * ROW_CHUNKS, 16 * ROW_CHUNKS), :],
            sem.at[slot]).start(priority=(r // 16) % 2)


def _row_gather_wait(buf, sem, slot):
    pltpu.make_async_copy(buf.at[slot], buf.at[slot], sem.at[slot]).wait()


def _issue_anchor(buf, slot):
    return buf[slot, 0:1, 0:1] * 0.0


def _rows_to_matrix(buf, slot, row0, n_rows):
    return jnp.concatenate(
        [buf[slot, pl.ds(row0 * ROW_CHUNKS + c, n_rows, stride=ROW_CHUNKS), :]
         for c in range(ROW_CHUNKS)], axis=-1)


def _expert_kernel(blk_e_ref, nused_ref, tok_cur_ref, tok_nxt_ref, h2_hbm,
                   wgu_ref, bgu_ref, wdn_ref, bdn_ref, y_ref, xbuf, sem):
    del blk_e_ref
    i = pl.program_id(0)
    slot = i % 2
    nused = nused_ref[0]
    d_ff = wdn_ref.shape[0]
    n_col = 4
    wc = d_ff // n_col
    per = MOE_ROWS // (2 * n_col)

    @pl.when(i == 0)
    def _():
        _row_gather_start(h2_hbm, tok_cur_ref, xbuf, sem, 0, range(MOE_ROWS))

    @pl.when(i < nused)
    def _():
        _row_gather_wait(xbuf, sem, slot)
        x = _rows_to_matrix(xbuf, slot, 0, MOE_ROWS).astype(BF16)
        acts = []
        for c in range(n_col):
            g = (jnp.dot(x, wgu_ref[:, c * wc:(c + 1) * wc], preferred_element_type=F32)
                 + bgu_ref[:, c * wc:(c + 1) * wc])
            _row_gather_start(h2_hbm, tok_nxt_ref, xbuf, sem, 1 - slot,
                              range(2 * c * per, (2 * c + 1) * per))
            g = g + _issue_anchor(xbuf, slot)
            u = (jnp.dot(x, wgu_ref[:, d_ff + c * wc:d_ff + (c + 1) * wc],
                         preferred_element_type=F32)
                 + bgu_ref[:, d_ff + c * wc:d_ff + (c + 1) * wc])
            _row_gather_start(h2_hbm, tok_nxt_ref, xbuf, sem, 1 - slot,
                              range((2 * c + 1) * per, (2 * c + 2) * per))
            u = u + _issue_anchor(xbuf, slot)
            gate = jnp.minimum(g, SWIGLU_LIMIT)
            up = jnp.clip(u, -SWIGLU_LIMIT, SWIGLU_LIMIT)
            acts.append((gate * _sigmoid(SWIGLU_ALPHA * gate) * (up + 1.0)).astype(BF16))
        act = jnp.concatenate(acts, axis=-1)
        y = jnp.dot(act, wdn_ref[...], preferred_element_type=F32) + bdn_ref[...]
        for c in range(ROW_CHUNKS):
            y_ref[pl.ds(c, MOE_ROWS, stride=ROW_CHUNKS), :] = y[:, c * LANES:(c + 1) * LANES]

        @pl.when(i == nused - 1)
        def _():
            _row_gather_wait(xbuf, sem, 1 - slot)

    @pl.when(i >= nused)
    def _():
        y_ref[...] = jnp.zeros_like(y_ref)


def _experts(blk_expert, n_used, row_tok3, h2r, w_gu, b_gu, w_dn, b_dn):
    n_blocks = row_tok3.shape[0]
    e, d, f2 = w_gu.shape
    blk_rows = MOE_ROWS * ROW_CHUNKS
    grid_spec = pltpu.PrefetchScalarGridSpec(
        num_scalar_prefetch=2,
        grid=(n_blocks,),
        in_specs=[
            pl.BlockSpec((1, 1, MOE_ROWS), lambda i, be, nu: (i, 0, 0),
                         memory_space=pltpu.SMEM),
            pl.BlockSpec((1, 1, MOE_ROWS),
                         lambda i, be, nu: (jnp.minimum(i + 1, nu[0] - 1), 0, 0),
                         memory_space=pltpu.SMEM),
            pl.BlockSpec(memory_space=pl.ANY),
            pl.BlockSpec((None, d, f2), lambda i, be, nu: (be[i], 0, 0)),
            pl.BlockSpec((None, 1, f2), lambda i, be, nu: (be[i], 0, 0)),
            pl.BlockSpec((None, f2 // 2, d), lambda i, be, nu: (be[i], 0, 0)),
            pl.BlockSpec((None, 1, d), lambda i, be, nu: (be[i], 0, 0)),
        ],
        out_specs=pl.BlockSpec((blk_rows, LANES), lambda i, be, nu: (i, 0)),
        scratch_shapes=[pltpu.VMEM((2, blk_rows, LANES), F32),
                        pltpu.SemaphoreType.DMA((2,))],
    )
    return pl.pallas_call(
        _expert_kernel,
        grid_spec=grid_spec,
        out_shape=jax.ShapeDtypeStruct((n_blocks * blk_rows, LANES), F32),
        compiler_params=pltpu.CompilerParams(
            dimension_semantics=("arbitrary",), vmem_limit_bytes=VMEM_LIMIT),
        name="experts",
    )(blk_expert, n_used, row_tok3, row_tok3, h2r, w_gu, b_gu.reshape(e, 1, f2),
      w_dn, b_dn.reshape(e, 1, d))


def _combine_kernel(dst_cur_ref, dst_nxt_ref, y_hbm, x1_ref, gate_ref, p_ref, gp_ref,
                    wg_ref, wp_ref, gfin_ref, o_ref, ybuf, sem, *, tm, n_tiles):
    i = pl.program_id(0)
    slot = i % 2
    n_rows = TOP_K * tm
    d = x1_ref.shape[1]
    n_col = 4
    wc = d // n_col
    per = n_rows // (TOP_K + n_col)

    def start_next(part):
        _row_gather_start(y_hbm, dst_nxt_ref, ybuf, sem, 1 - slot,
                          range(part * per, (part + 1) * per))

    @pl.when(i == 0)
    def _():
        _row_gather_start(y_hbm, dst_cur_ref, ybuf, sem, 0, range(n_rows))

    _row_gather_wait(ybuf, sem, slot)
    gates = gate_ref[...]
    x2 = x1_ref[...]
    for k in range(TOP_K):
        x2 = x2 + gates[:, k:k + 1] * _rows_to_matrix(ybuf, slot, k * tm, tm)
        start_next(k)
    hg = _rms(x2, gp_ref[...]).astype(BF16)
    pb = p_ref[...].astype(BF16)
    x3 = []
    for c in range(n_col):
        cols = slice(c * wc, (c + 1) * wc)
        gate = _sigmoid(jnp.dot(hg, wg_ref[:, cols], preferred_element_type=F32))
        emb = jnp.dot(pb, wp_ref[:, cols], preferred_element_type=F32)
        start_next(TOP_K + c)
        x3.append(x2[:, cols] + gate * emb + _issue_anchor(ybuf, slot))
    o_ref[...] = _rms(jnp.concatenate(x3, axis=-1), gfin_ref[...])

    @pl.when(i == n_tiles - 1)
    def _():
        _row_gather_wait(ybuf, sem, 1 - slot)


def _combine(dest3, y, x1, gates_tk, p2, ple_g, w_gate, w_proj, fin_g, *, tm=256):
    t, d = x1.shape
    n_tiles = t // tm
    last = n_tiles - 1
    n_rows = TOP_K * tm
    ple = p2.shape[1]
    kern = functools.partial(_combine_kernel, tm=tm, n_tiles=n_tiles)
    row = lambda i: (i, 0)
    const = lambda i: (0, 0)
    return pl.pallas_call(
        kern,
        grid=(n_tiles,),
        in_specs=[pl.BlockSpec((1, 1, n_rows), lambda i: (i, 0, 0), memory_space=pltpu.SMEM),
                  pl.BlockSpec((1, 1, n_rows), lambda i: (jnp.minimum(i + 1, last), 0, 0),
                               memory_space=pltpu.SMEM),
                  pl.BlockSpec(memory_space=pl.ANY),
                  pl.BlockSpec((tm, d), row),
                  pl.BlockSpec((tm, TOP_K), row),
                  pl.BlockSpec((tm, ple), row),
                  pl.BlockSpec((1, d), const),
                  pl.BlockSpec((d, d), const),
                  pl.BlockSpec((ple, d), const),
                  pl.BlockSpec((1, d), const)],
        out_specs=pl.BlockSpec((tm, d), row),
        out_shape=jax.ShapeDtypeStruct((t, d), F32),
        scratch_shapes=[pltpu.VMEM((2, n_rows * ROW_CHUNKS, LANES), F32),
                        pltpu.SemaphoreType.DMA((2,))],
        compiler_params=pltpu.CompilerParams(
            dimension_semantics=("arbitrary",), vmem_limit_bytes=VMEM_LIMIT),
        name="combine",
    )(dest3, dest3, y.reshape(-1, ROW_CHUNKS, LANES), x1, gates_tk, p2, ple_g.reshape(1, d),
      w_gate, w_proj, fin_g.reshape(1, d))


def _routing_tables(topi, rank, counts, *, tm_combine):
    k, t = topi.shape
    n_blocks = (k * t) // MOE_ROWS + N_EXPERTS
    padded = (counts + MOE_ROWS - 1) // MOE_ROWS * MOE_ROWS
    pad_end = jnp.cumsum(padded)
    pad_start = pad_end - padded
    eids = jnp.arange(N_EXPERTS, dtype=I32)[:, None, None]
    dest = rank + jnp.sum(jnp.where(topi[None] == eids, pad_start[:, None, None], 0), axis=0)
    n_used = (pad_end[-1] // MOE_ROWS).astype(I32).reshape(1)
    blk_row0 = jnp.arange(n_blocks, dtype=I32) * MOE_ROWS
    blk_expert = jnp.minimum(
        jnp.sum((pad_end[None, :] <= blk_row0[:, None]).astype(I32), axis=1), N_EXPERTS - 1)
    tok = jnp.broadcast_to(jnp.arange(t, dtype=I32)[None, :], (k, t))
    row_tok = jnp.zeros((n_blocks * MOE_ROWS,), I32).at[dest.reshape(-1)].set(tok.reshape(-1))
    row_tok3 = row_tok.reshape(n_blocks, 1, MOE_ROWS)
    n_tiles = t // tm_combine
    dest3 = dest.reshape(k, n_tiles, tm_combine).transpose(1, 0, 2).reshape(
        n_tiles, 1, k * tm_combine)
    return blk_expert, n_used, row_tok3, dest3


SEG_ALIGN = 16
MOE_TILE = 512
LOCAL_ROWS = TOP_K * MOE_TILE + N_EXPERTS * SEG_ALIGN
MAX_CHUNKS = LOCAL_ROWS // SEG_ALIGN
CHUNK_TABLE = 256
TAIL_TABLE = N_EXPERTS * (MOE_ROWS // SEG_ALIGN)
START_GROUP = 4
WAIT_GROUP = 8


def _chunk_copy(src, dst, sem):
    return pltpu.make_async_copy(src, dst, sem)


def _chunk_wait_n(local_buf, hbm, sem, slot, n, to_hbm):
    def wait_rows(rows):
        loc = local_buf.at[slot, pl.ds(0, rows), :]
        far = hbm.at[pl.ds(0, rows), :]

        def body(j, carry):
            (_chunk_copy(loc, far, sem.at[slot]) if to_hbm
             else _chunk_copy(far, loc, sem.at[slot])).wait()
            return carry
        return body

    n_groups = n // WAIT_GROUP
    lax.fori_loop(0, n_groups, wait_rows(WAIT_GROUP * SEG_ALIGN), 0)
    lax.fori_loop(n_groups * WAIT_GROUP, n, wait_rows(SEG_ALIGN), 0)


def _chunk_start_n(local_buf, hbm, tab_ref, sem, slot, n, to_hbm):
    def start(j, priority):
        loc = local_buf.at[slot, pl.ds(pl.multiple_of(j * SEG_ALIGN, SEG_ALIGN), SEG_ALIGN), :]
        far = hbm.at[pl.ds(pl.multiple_of(tab_ref[0, 0, j] * SEG_ALIGN, SEG_ALIGN), SEG_ALIGN), :]
        (_chunk_copy(loc, far, sem.at[slot]) if to_hbm
         else _chunk_copy(far, loc, sem.at[slot])).start(priority=priority)

    def group(g, carry):
        for u in range(START_GROUP):
            start(g * START_GROUP + u, u % 2)
        return carry

    def single(j, carry):
        start(j, 0)
        return carry

    n_groups = n // START_GROUP
    lax.fori_loop(0, n_groups, group, 0)
    lax.fori_loop(n_groups * START_GROUP, n, single, 0)


def _dispatch_kernel(nch_ref, nused_ref, tab_ref, tail_ref, lp_ref, h2_ref, xs_hbm,
                     buf, zbuf, sem, zsem, *, n_tiles, n_blocks):
    i = pl.program_id(0)
    slot = i % 2

    @pl.when(i >= 2)
    def _():
        _chunk_wait_n(buf, xs_hbm, sem, slot, nch_ref[jnp.maximum(i - 2, 0)], True)

    lp = lp_ref[...]
    tm = lp.shape[1]
    rows = lax.broadcasted_iota(I32, (LOCAL_ROWS, tm), 0)
    onehot = jnp.zeros((LOCAL_ROWS, tm), F32)
    for k in range(TOP_K):
        onehot = jnp.where(rows == lp[k:k + 1, :], 1.0, onehot)
    buf[slot] = jnp.dot(onehot.astype(BF16), h2_ref[...],
                        preferred_element_type=F32).astype(BF16)
    _chunk_start_n(buf, xs_hbm, tab_ref, sem, slot, nch_ref[i], True)

    @pl.when(i == n_tiles - 1)
    def _():
        zbuf[...] = jnp.zeros_like(zbuf)
        zrow = zbuf.at[pl.ds(0, SEG_ALIGN), :]

        def fill(j, carry):
            @pl.when(tail_ref[0, 0, j] >= 0)
            def _():
                row0 = pl.multiple_of(tail_ref[0, 0, j] * SEG_ALIGN, SEG_ALIGN)
                _chunk_copy(zrow, xs_hbm.at[pl.ds(row0, SEG_ALIGN), :], zsem).start()
            return carry

        def drain(j, carry):
            @pl.when(tail_ref[0, 0, j] >= 0)
            def _():
                _chunk_copy(zrow, xs_hbm.at[pl.ds(0, SEG_ALIGN), :], zsem).wait()
            return carry

        def fill_block(b, carry):
            row0 = pl.multiple_of(b * MOE_ROWS, MOE_ROWS)
            _chunk_copy(zbuf, xs_hbm.at[pl.ds(row0, MOE_ROWS), :], zsem).start()
            return carry

        def drain_block(b, carry):
            _chunk_copy(zbuf, xs_hbm.at[pl.ds(0, MOE_ROWS), :], zsem).wait()
            return carry

        lax.fori_loop(0, TAIL_TABLE, fill, 0)
        lax.fori_loop(0, TAIL_TABLE, drain, 0)
        lax.fori_loop(nused_ref[0], n_blocks, fill_block, 0)
        lax.fori_loop(nused_ref[0], n_blocks, drain_block, 0)
        _chunk_wait_n(buf, xs_hbm, sem, slot, nch_ref[i], True)
        if n_tiles > 1:
            _chunk_wait_n(buf, xs_hbm, sem, 1 - slot, nch_ref[jnp.maximum(i - 1, 0)], True)


def _dispatch(n_chunks, n_used, chunk_tab, tail_tab, lp, h2, n_blocks):
    k, t = lp.shape
    d = h2.shape[1]
    n_tiles = t // MOE_TILE
    grid_spec = pltpu.PrefetchScalarGridSpec(
        num_scalar_prefetch=2,
        grid=(n_tiles,),
        in_specs=[pl.BlockSpec((1, 1, CHUNK_TABLE), lambda i, nc, nu: (i, 0, 0),
                               memory_space=pltpu.SMEM),
                  pl.BlockSpec((1, 1, TAIL_TABLE), lambda i, nc, nu: (0, 0, 0),
                               memory_space=pltpu.SMEM),
                  pl.BlockSpec((k, MOE_TILE), lambda i, nc, nu: (0, i)),
                  pl.BlockSpec((MOE_TILE, d), lambda i, nc, nu: (i, 0))],
        out_specs=pl.BlockSpec(memory_space=pl.ANY),
        scratch_shapes=[pltpu.VMEM((2, LOCAL_ROWS, d), BF16), pltpu.VMEM((MOE_ROWS, d), BF16),
                        pltpu.SemaphoreType.DMA((2,)), pltpu.SemaphoreType.DMA],
    )
    return pl.pallas_call(
        functools.partial(_dispatch_kernel, n_tiles=n_tiles, n_blocks=n_blocks),
        grid_spec=grid_spec,
        out_shape=jax.ShapeDtypeStruct((n_blocks * MOE_ROWS, d), BF16),
        compiler_params=pltpu.CompilerParams(
            dimension_semantics=("arbitrary",), vmem_limit_bytes=VMEM_LIMIT),
        name="dispatch",
    )(n_chunks, n_used, chunk_tab, tail_tab, lp, h2)


def _expert_mlp_kernel(blk_e_ref, nused_ref, x_ref, wgu_ref, bgu_ref, wdn_ref, bdn_ref, y_ref,
                       wgu_bf, wdn_bf):
    i = pl.program_id(0)
    nused = nused_ref[0]
    new_expert = jnp.logical_or(i == 0, blk_e_ref[i] != blk_e_ref[jnp.maximum(i - 1, 0)])

    @pl.when(jnp.logical_and(new_expert, i < nused))
    def _():
        wgu_bf[...] = wgu_ref[...].astype(BF16)
        wdn_bf[...] = wdn_ref[...].astype(BF16)

    @pl.when(i < nused)
    def _():
        x = x_ref[...]
        gu = jnp.dot(x, wgu_bf[...], preferred_element_type=F32) + bgu_ref[...]
        d_ff = gu.shape[1] // 2
        gate = jnp.minimum(gu[:, :d_ff], SWIGLU_LIMIT)
        up = jnp.clip(gu[:, d_ff:], -SWIGLU_LIMIT, SWIGLU_LIMIT)
        act = gate * _sigmoid(SWIGLU_ALPHA * gate) * (up + 1.0)
        y = jnp.dot(act.astype(BF16), wdn_bf[...], preferred_element_type=F32) + bdn_ref[...]
        y_ref[...] = y.astype(BF16)

    @pl.when(i >= nused)
    def _():
        y_ref[...] = jnp.zeros_like(y_ref)


def _expert_mlp(blk_expert, n_used, xs, w_gu, b_gu, w_dn, b_dn):
    d = xs.shape[1]
    n_blocks = blk_expert.shape[0]
    n_rows = n_blocks * MOE_ROWS
    e, _, f2 = w_gu.shape
    grid_spec = pltpu.PrefetchScalarGridSpec(
        num_scalar_prefetch=2,
        grid=(n_blocks,),
        in_specs=[
            pl.BlockSpec((MOE_ROWS, d), lambda i, be, nu: (jnp.minimum(i, nu[0] - 1), 0)),
            pl.BlockSpec((None, d, f2), lambda i, be, nu: (be[i], 0, 0)),
            pl.BlockSpec((None, 1, f2), lambda i, be, nu: (be[i], 0, 0)),
            pl.BlockSpec((None, f2 // 2, d), lambda i, be, nu: (be[i], 0, 0)),
            pl.BlockSpec((None, 1, d), lambda i, be, nu: (be[i], 0, 0)),
        ],
        out_specs=pl.BlockSpec((MOE_ROWS, d), lambda i, be, nu: (i, 0)),
        scratch_shapes=[pltpu.VMEM((d, f2), BF16), pltpu.VMEM((f2 // 2, d), BF16)],
    )
    return pl.pallas_call(
        _expert_mlp_kernel,
        grid_spec=grid_spec,
        out_shape=jax.ShapeDtypeStruct((n_rows, d), BF16),
        compiler_params=pltpu.CompilerParams(
            dimension_semantics=("arbitrary",), vmem_limit_bytes=VMEM_LIMIT),
        name="experts",
    )(blk_expert, n_used, xs, w_gu, b_gu.reshape(e, 1, f2), w_dn, b_dn.reshape(e, 1, d))


def _moe_combine_kernel(nch_ref, tab_cur_ref, tab_nxt_ref, y_hbm, lpt_ref, gate_ref, x1_ref,
                        p_ref, gp_ref, wg_ref, wp_ref, gfin_ref, o_ref, buf, sem, *, n_tiles):
    i = pl.program_id(0)
    slot = i % 2

    @pl.when(i == 0)
    def _():
        buf[...] = jnp.zeros_like(buf)
        _chunk_start_n(buf, y_hbm, tab_cur_ref, sem, 0, nch_ref[0], False)

    @pl.when(i + 1 < n_tiles)
    def _():
        _chunk_start_n(buf, y_hbm, tab_nxt_ref, sem, 1 - slot,
                       nch_ref[jnp.minimum(i + 1, n_tiles - 1)], False)

    _chunk_wait_n(buf, y_hbm, sem, slot, nch_ref[i], False)
    lpt = lpt_ref[...]
    gates = gate_ref[...]
    tm = lpt.shape[0]
    cols = lax.broadcasted_iota(I32, (tm, LOCAL_ROWS), 1)
    weights = jnp.zeros((tm, LOCAL_ROWS), F32)
    for k in range(TOP_K):
        weights = jnp.where(cols == lpt[:, k:k + 1], gates[:, k:k + 1], weights)
    x2 = x1_ref[...] + jnp.dot(weights.astype(BF16), buf[slot], preferred_element_type=F32)
    hg = _rms(x2, gp_ref[...]).astype(BF16)
    gate = _sigmoid(jnp.dot(hg, wg_ref[...], preferred_element_type=F32))
    emb = jnp.dot(p_ref[...].astype(BF16), wp_ref[...], preferred_element_type=F32)
    x3 = x2 + gate * emb
    o_ref[...] = _rms(x3, gfin_ref[...])


def _moe_combine(n_chunks, chunk_tab, y, lpt, gates_tk, x1, p2, ple_g, w_gate, w_proj, fin_g):
    t, d = x1.shape
    n_tiles = t // MOE_TILE
    last = n_tiles - 1
    ple = p2.shape[1]
    row = lambda i, nc: (i, 0)
    const = lambda i, nc: (0, 0)
    grid_spec = pltpu.PrefetchScalarGridSpec(
        num_scalar_prefetch=1,
        grid=(n_tiles,),
        in_specs=[pl.BlockSpec((1, 1, CHUNK_TABLE), lambda i, nc: (i, 0, 0),
                               memory_space=pltpu.SMEM),
                  pl.BlockSpec((1, 1, CHUNK_TABLE),
                               lambda i, nc: (jnp.minimum(i + 1, last), 0, 0),
                               memory_space=pltpu.SMEM),
                  pl.BlockSpec(memory_space=pl.ANY),
                  pl.BlockSpec((MOE_TILE, TOP_K), row),
                  pl.BlockSpec((MOE_TILE, TOP_K), row),
                  pl.BlockSpec((MOE_TILE, d), row),
                  pl.BlockSpec((MOE_TILE, ple), row),
                  pl.BlockSpec((1, d), const),
                  pl.BlockSpec((d, d), const),
                  pl.BlockSpec((ple, d), const),
                  pl.BlockSpec((1, d), const)],
        out_specs=pl.BlockSpec((MOE_TILE, d), row),
        scratch_shapes=[pltpu.VMEM((2, LOCAL_ROWS, d), BF16), pltpu.SemaphoreType.DMA((2,))],
    )
    return pl.pallas_call(
        functools.partial(_moe_combine_kernel, n_tiles=n_tiles),
        grid_spec=grid_spec,
        out_shape=jax.ShapeDtypeStruct((t, d), F32),
        compiler_params=pltpu.CompilerParams(
            dimension_semantics=("arbitrary",), vmem_limit_bytes=VMEM_LIMIT),
        name="combine",
    )(n_chunks, chunk_tab, chunk_tab, y, lpt, gates_tk, x1, p2, ple_g.reshape(1, d),
      w_gate, w_proj, fin_g.reshape(1, d))


def _moe_tables(topi, lrank, cnt):
    k, t = topi.shape
    n_tiles = t // MOE_TILE
    cnt = cnt.astype(I32)
    pc = (cnt + SEG_ALIGN - 1) // SEG_ALIGN * SEG_ALIGN
    l_end = jnp.cumsum(pc, axis=1)
    l_start = l_end - pc
    tot = jnp.sum(pc, axis=0)
    e_pad = (tot + MOE_ROWS - 1) // MOE_ROWS * MOE_ROWS
    e_end = jnp.cumsum(e_pad)
    seg_row0 = (e_end - e_pad)[None, :] + jnp.cumsum(pc, axis=0) - pc
    eids = jnp.arange(N_EXPERTS, dtype=I32)[:, None, None, None]
    topi3 = topi.reshape(k, n_tiles, MOE_TILE)
    base = jnp.sum(jnp.where(topi3[None] == eids, l_start.T[:, None, :, None], 0), axis=0)
    lp = (lrank.reshape(k, n_tiles, MOE_TILE) + base).reshape(k, t)
    j16 = jnp.arange(CHUNK_TABLE, dtype=I32)[None, :] * SEG_ALIGN
    ce = jnp.minimum(jnp.sum((l_end[:, :, None] <= j16[:, None, :]).astype(I32), axis=1),
                     N_EXPERTS - 1)
    pick = ce[:, None, :] == jnp.arange(N_EXPERTS, dtype=I32)[None, :, None]
    seg0 = jnp.sum(jnp.where(pick, seg_row0[:, :, None], 0), axis=1)
    loc0 = jnp.sum(jnp.where(pick, l_start[:, :, None], 0), axis=1)
    n_chunks = l_end[:, -1] // SEG_ALIGN
    valid = jnp.arange(CHUNK_TABLE, dtype=I32)[None, :] < n_chunks[:, None]
    tab = jnp.where(valid, (seg0 + j16 - loc0) // SEG_ALIGN, 0).astype(I32)
    n_rows_max = k * t + n_tiles * N_EXPERTS * (SEG_ALIGN - 1) + N_EXPERTS * (MOE_ROWS - 1)
    n_blocks = (n_rows_max + MOE_ROWS - 1) // MOE_ROWS
    n_used = (e_end[-1] // MOE_ROWS).astype(I32).reshape(1)
    blk_row0 = jnp.arange(n_blocks, dtype=I32) * MOE_ROWS
    blk_expert = jnp.minimum(
        jnp.sum((e_end[None, :] <= blk_row0[:, None]).astype(I32), axis=1), N_EXPERTS - 1)
    per_e = MOE_ROWS // SEG_ALIGN
    tj = jnp.arange(per_e, dtype=I32)[None, :]
    tail = jnp.where(tj < ((e_pad - tot) // SEG_ALIGN)[:, None],
                     ((e_end - e_pad + tot) // SEG_ALIGN)[:, None] + tj, -1).astype(I32)
    return (lp, n_chunks.astype(I32), tab.reshape(n_tiles, 1, CHUNK_TABLE),
            tail.reshape(1, 1, TAIL_TABLE), blk_expert, n_used)


def _layer(x2, p2, seq, norm_mix_g, w_in, b_f, lam_re, lam_im, log_dt, b_re, b_im, c_re, c_im,
           d_skip, w_glu, b_glu, attn_out_g, ssm_out_g, w_out, norm_ffn_g, w_router, b_router,
           w_gu, b_gu, w_dn, b_dn, norm_ple_g, w_ple_gate, w_ple_proj, final_g):
    t, d = x2.shape
    nb = t // seq
    aw = ATTN_WIDTH
    w_main = jnp.concatenate([w_in[:, :3 * aw], w_in[:, 3 * aw + N_HEADS:]], axis=1).astype(BF16)
    wft = w_in[:, 3 * aw:3 * aw + N_HEADS].T.astype(BF16)

    a_re, a_im, bbr, bbi = _ssm_prep(lam_re, lam_im, log_dt, b_re, b_im)
    bre = _block_diag(bbr, True).astype(BF16)
    bim = _block_diag(bbi, True).astype(BF16)
    cre = _block_diag(jnp.transpose(c_re, (0, 2, 1)), False).astype(BF16)
    cim = _block_diag(jnp.transpose(c_im, (0, 2, 1)), False).astype(BF16)

    q, k, v, u, c_t = _in_proj(x2, norm_mix_g, w_main, wft, b_f, seq=seq)
    shp = (nb, seq, aw)
    c3 = c_t.reshape(N_HEADS // 2, 2, t)
    attn = _attention(q.reshape(shp), k.reshape(shp), v.reshape(shp), c3)
    ssm = _ssm(u.reshape(nb, seq, SSM_WIDTH), bre, bim, cre, cim, a_re, a_im,
               d_skip, w_glu.astype(BF16), b_glu, ssm_out_g)

    x1, h2, topi, gates, lrank, cnt = _out_proj(
        x2, attn.reshape(t, aw), ssm.reshape(t, SSM_WIDTH), attn_out_g,
        w_out[:aw].astype(BF16), w_out[aw:].astype(BF16), norm_ffn_g,
        w_router.T.astype(BF16), b_router, tm=MOE_TILE)

    lp, n_chunks, chunk_tab, tail_tab, blk_expert, n_used = _moe_tables(topi, lrank, cnt[:, :, 0])
    xs = _dispatch(n_chunks, n_used, chunk_tab, tail_tab, lp, h2, blk_expert.shape[0])
    y = _expert_mlp(blk_expert, n_used, xs, w_gu, b_gu, w_dn, b_dn)
    return _moe_combine(n_chunks, chunk_tab, y, lp.T, gates.T, x1, p2, norm_ple_g,
                        w_ple_gate.astype(BF16), w_ple_proj.astype(BF16), final_g)


def kernel(x, p, norm_mix_g, w_in, b_f, lam_re, lam_im, log_dt, b_re, b_im, c_re, c_im, d_skip, w_glu, b_glu, attn_out_g, ssm_out_g, w_out, norm_ffn_g, w_router, b_router, w_gu, b_gu, w_dn, b_dn, norm_ple_g, w_ple_gate, w_ple_proj, norm_final_g):
    bsz, seq, d = x.shape
    depth = w_in.shape[0]
    assert depth == 1, "one layer: the final rmsnorm is fused into the layer's last kernel"
    out = _layer(x.reshape(bsz * seq, d), p[0].reshape(bsz * seq, -1), seq,
                 norm_mix_g[0], w_in[0], b_f[0], lam_re[0], lam_im[0], log_dt[0],
                 b_re[0], b_im[0], c_re[0], c_im[0], d_skip[0], w_glu[0], b_glu[0],
                 attn_out_g[0], ssm_out_g[0], w_out[0], norm_ffn_g[0], w_router[0],
                 b_router[0], w_gu[0], b_gu[0], w_dn[0], b_dn[0], norm_ple_g[0],
                 w_ple_gate[0], w_ple_proj[0], norm_final_g)
    return out.reshape(bsz, seq, d)
```

```python
import functools
import math

import jax
import jax.numpy as jnp
from jax import lax
from jax.experimental import pallas as pl
from jax.experimental.pallas import tpu as pltpu

F32 = jnp.float32
BF16 = jnp.bfloat16
I32 = jnp.int32

NORM_EPS = 1e-5
HEAD_DIM = 64
N_HEADS = 8
ATTN_WIDTH = 512
SSM_WIDTH = 512
SSM_GROUP = 16
N_SSM_GROUPS = 32
SSM_STATE = 64
N_STATE = N_SSM_GROUPS * SSM_STATE
N_EXPERTS = 32
TOP_K = 4
SWIGLU_LIMIT = 7.0
SWIGLU_ALPHA = 1.702
LANES = 128
MOE_ROWS = 512
NEG_BIG = -1e30
LOG2E = math.log2(math.e)
VMEM_LIMIT = 56 * 1024 * 1024

_NT = (((1,), (1,)), ((), ()))


def _rms(xf, g):
    ms = jnp.mean(xf * xf, axis=-1, keepdims=True)
    return xf * lax.rsqrt(ms + NORM_EPS) * g


def _sigmoid(x):
    return 1.0 / (1.0 + jnp.exp(-x))


def _ssm_prep_kernel(lr_ref, li_ref, ldt_ref, brt_ref, bit_ref,
                     ar_ref, ai_ref, bbr_ref, bbi_ref):
    lr = lr_ref[...]
    li = li_ref[...]
    dt = jnp.exp(ldt_ref[...])
    mag = jnp.exp(lr * dt)
    ab_re = mag * jnp.cos(li * dt)
    ab_im = mag * jnp.sin(li * dt)
    den = lr * lr + li * li
    nr = ab_re - 1.0
    z_re = (nr * lr + ab_im * li) / den
    z_im = (ab_im * lr - nr * li) / den
    ar_ref[...] = ab_re
    ai_ref[...] = ab_im
    br = brt_ref[...]
    bi = bit_ref[...]
    bbr_ref[...] = z_re * br - z_im * bi
    bbi_ref[...] = z_re * bi + z_im * br


def _ssm_prep(lam_re, lam_im, log_dt, b_re, b_im):
    g, p, c = b_re.shape
    brt = jnp.transpose(b_re, (0, 2, 1))
    bit = jnp.transpose(b_im, (0, 2, 1))
    return pl.pallas_call(
        _ssm_prep_kernel,
        out_shape=(jax.ShapeDtypeStruct((g, 1, p), F32), jax.ShapeDtypeStruct((g, 1, p), F32),
                   jax.ShapeDtypeStruct((g, c, p), F32), jax.ShapeDtypeStruct((g, c, p), F32)),
        name="ssm_prep",
    )(lam_re.reshape(g, 1, p), lam_im.reshape(g, 1, p), log_dt.reshape(g, 1, 1), brt, bit)


def _block_diag(w):
    g, a, b = w.shape
    half = g // 2
    eye = jnp.eye(half, dtype=w.dtype)
    w4 = w.reshape(2, half, a, b)
    out = jnp.einsum('hgab,gk->hgakb', w4, eye)
    return out.reshape(2, half * a, half * b)


def _inproj_kernel(x_ref, g_ref, w_ref, wft_ref, bf_ref, tri_ref,
                   q_ref, k_ref, v_ref, u_ref, c_ref, carry_ref, *, tiles_per_seq, tm):
    i = pl.program_id(0)

    @pl.when(i % tiles_per_seq == 0)
    def _():
        carry_ref[...] = jnp.zeros_like(carry_ref)

    h = _rms(x_ref[...], g_ref[...]).astype(BF16)
    proj = jnp.dot(h, w_ref[...], preferred_element_type=F32)
    aw = ATTN_WIDTH
    q_ref[...] = (proj[:, 0:aw] * (LOG2E * HEAD_DIM ** -0.5)).astype(BF16)
    k_ref[...] = proj[:, aw:2 * aw].astype(BF16)
    v_ref[...] = proj[:, 2 * aw:3 * aw].astype(BF16)
    u_ref[...] = proj[:, 3 * aw:3 * aw + SSM_WIDTH].astype(BF16)

    fl = lax.dot_general(wft_ref[...], h, _NT, preferred_element_type=F32)
    z = fl + bf_ref[...]
    lf = jnp.minimum(z, 0.0) - jnp.log1p(jnp.exp(-jnp.abs(z)))
    hi = lf.astype(BF16)
    lo = (lf - hi.astype(F32)).astype(BF16)
    tri = tri_ref[...]
    cs = (jnp.dot(hi, tri, preferred_element_type=F32)
          + jnp.dot(lo, tri, preferred_element_type=F32))
    c = cs + carry_ref[:, 0:1]
    c_ref[...] = c * LOG2E
    carry_ref[...] = jnp.broadcast_to(c[:, tm - 1:tm], carry_ref.shape)


def _in_proj(x2, norm_g, w_main, wft, b_f, *, seq, tm=1024):
    t, d = x2.shape
    n_main = w_main.shape[1]
    tri = jnp.triu(jnp.ones((tm, tm), F32)).astype(BF16)
    kern = functools.partial(_inproj_kernel, tiles_per_seq=seq // tm, tm=tm)
    row = lambda i: (i, 0)
    const = lambda i: (0, 0)
    act = jax.ShapeDtypeStruct((t, ATTN_WIDTH), BF16)
    return pl.pallas_call(
        kern,
        grid=(t // tm,),
        in_specs=[pl.BlockSpec((tm, d), row),
                  pl.BlockSpec((1, d), const),
                  pl.BlockSpec((d, n_main), const),
                  pl.BlockSpec((N_HEADS, d), const),
                  pl.BlockSpec((N_HEADS, 1), const),
                  pl.BlockSpec((tm, tm), const)],
        out_specs=[pl.BlockSpec((tm, ATTN_WIDTH), row)] * 4
        + [pl.BlockSpec((N_HEADS, tm), lambda i: (0, i))],
        out_shape=[act, act, act, act, jax.ShapeDtypeStruct((N_HEADS, t), F32)],
        scratch_shapes=[pltpu.VMEM((N_HEADS, LANES), F32)],
        compiler_params=pltpu.CompilerParams(
            dimension_semantics=("arbitrary",), vmem_limit_bytes=VMEM_LIMIT),
        name="in_proj",
    )(x2, norm_g.reshape(1, d), w_main, wft, b_f.reshape(N_HEADS, 1), tri)


def _attn_kernel(q_ref, k_ref, v_ref, c_ref, o_ref, *, tq, n_pairs):
    i = pl.program_id(2)
    lane = lax.broadcasted_iota(I32, (1, LANES), 1)
    first = lane < HEAD_DIM
    n_heads = 2 * n_pairs
    q_heads = []
    for p in range(n_pairs):
        q2 = q_ref[:, p * LANES:(p + 1) * LANES]
        zero = jnp.zeros_like(q2)
        q_heads += [jnp.where(first, q2, zero), jnp.where(first, zero, q2)]
    half = tq // 2

    def block(off, width, r0, carry, masked):
        cj = c_ref[:, pl.ds(off, width)]
        out = []
        for p in range(n_pairs):
            kj = k_ref[pl.ds(off, width), p * LANES:(p + 1) * LANES]
            vj = v_ref[pl.ds(off, width), p * LANES:(p + 1) * LANES]
            one = jnp.ones_like(vj)
            v_heads = (jnp.where(first, vj, one), jnp.where(first, one, vj))
            for h in range(2):
                m, acc = carry[2 * p + h]
                s = (lax.dot_general(q_heads[2 * p + h][r0:], kj, _NT,
                                     preferred_element_type=F32)
                     - cj[2 * p + h:2 * p + h + 1, :])
                if masked:
                    rr = lax.broadcasted_iota(I32, s.shape, 0)
                    cc = lax.broadcasted_iota(I32, s.shape, 1)
                    s = jnp.where(cc <= rr, s, NEG_BIG)
                m_new = jnp.maximum(m[r0:], jnp.max(s, axis=-1, keepdims=True))
                alpha = jnp.exp2(m[r0:] - m_new)
                pr = jnp.exp2(s - m_new).astype(BF16)
                acc_new = alpha * acc[r0:] + jnp.dot(pr, v_heads[h], preferred_element_type=F32)
                if r0:
                    m_new = jnp.concatenate([m[:r0], m_new], axis=0)
                    acc_new = jnp.concatenate([acc[:r0], acc_new], axis=0)
                out.append((m_new, acc_new))
        return tuple(out)

    init_one = (jnp.full((tq, 1), NEG_BIG, F32), jnp.zeros((tq, LANES), F32))
    carry = lax.fori_loop(
        0, i, lambda j, c: block(pl.multiple_of(j * tq, tq), tq, 0, c, False),
        (init_one,) * n_heads)
    diag = pl.multiple_of(i * tq, tq)
    for n in range(tq // half):
        carry = block(pl.multiple_of(diag + n * half, half), half, n * half, carry, True)
    outs = []
    for p in range(n_pairs):
        acc_a = carry[2 * p][1]
        acc_b = carry[2 * p + 1][1]
        outs.append(jnp.where(first, acc_a / pltpu.roll(acc_a, HEAD_DIM, axis=1),
                              acc_b / pltpu.roll(acc_b, HEAD_DIM, axis=1)))
    o_ref[...] = jnp.concatenate(outs, axis=-1).astype(BF16)


def _attention(q, k, v, c_t, *, tq=1024, n_pairs=2):
    b, s, w = q.shape
    wb = n_pairs * LANES
    n_groups = w // wb
    c3 = c_t.reshape(n_groups, 2 * n_pairs, b * s)
    kern = functools.partial(_attn_kernel, tq=tq, n_pairs=n_pairs)
    return pl.pallas_call(
        kern,
        grid=(b, n_groups, s // tq),
        in_specs=[pl.BlockSpec((None, tq, wb), lambda bi, g, i: (bi, i, g)),
                  pl.BlockSpec((None, s, wb), lambda bi, g, i: (bi, 0, g)),
                  pl.BlockSpec((None, s, wb), lambda bi, g, i: (bi, 0, g)),
                  pl.BlockSpec((None, 2 * n_pairs, s), lambda bi, g, i: (g, 0, bi))],
        out_specs=pl.BlockSpec((None, tq, wb), lambda bi, g, i: (bi, i, g)),
        out_shape=jax.ShapeDtypeStruct((b, s, w), BF16),
        compiler_params=pltpu.CompilerParams(
            dimension_semantics=("arbitrary", "arbitrary", "arbitrary"),
            vmem_limit_bytes=VMEM_LIMIT),
        name="attention",
    )(q, k, v, c3)


def _ssm_kernel(u_ref, bre_ref, bim_ref, cre_ref, cim_ref, ar_ref, ai_ref, dsk_ref,
                wglu_ref, bglu_ref, g_ref, o_ref,
                us_ref, xr_ref, xi_ref, str_ref, sti_ref, res_ref, *, tt, nb):
    i = pl.program_id(0)

    @pl.when(i == 0)
    def _():
        str_ref[...] = jnp.zeros_like(str_ref)
        sti_ref[...] = jnp.zeros_like(sti_ref)

    n_chunks = SSM_WIDTH // LANES
    for b in range(nb):
        ub32 = u_ref[b].astype(F32)
        for c in range(n_chunks):
            us_ref[c, pl.ds(b, tt, stride=nb), :] = ub32[:, c * LANES:(c + 1) * LANES]
    uf = jnp.concatenate([us_ref[c] for c in range(n_chunks)], axis=-1)
    ub = uf.astype(BF16)
    half_in = SSM_WIDTH // 2
    half_st = N_STATE // 2
    for hf in range(2):
        uh = ub[:, hf * half_in:(hf + 1) * half_in]
        xr_ref[:, hf * half_st:(hf + 1) * half_st] = jnp.dot(
            uh, bre_ref[hf], preferred_element_type=F32)
        xi_ref[:, hf * half_st:(hf + 1) * half_st] = jnp.dot(
            uh, bim_ref[hf], preferred_element_type=F32)

    n_col_groups = 2
    wcol = N_STATE // n_col_groups
    unroll = 4
    for cg in range(n_col_groups):
        cols = slice(cg * wcol, (cg + 1) * wcol)
        ar = jnp.broadcast_to(ar_ref[:, cols], (nb, wcol))
        ai = jnp.broadcast_to(ai_ref[:, cols], (nb, wcol))

        def steps(tb, carry, cols=cols, ar=ar, ai=ai):
            sr, si = carry
            for k in range(unroll):
                r0 = pl.multiple_of((tb * unroll + k) * nb, nb)
                br = xr_ref[pl.ds(r0, nb), cols]
                bi = xi_ref[pl.ds(r0, nb), cols]
                nr = ar * sr - ai * si + br
                ni = ar * si + ai * sr + bi
                xr_ref[pl.ds(r0, nb), cols] = nr
                xi_ref[pl.ds(r0, nb), cols] = ni
                sr, si = nr, ni
            return sr, si

        sr, si = lax.fori_loop(0, tt // unroll, steps, (str_ref[:, cols], sti_ref[:, cols]))
        str_ref[:, cols] = sr
        sti_ref[:, cols] = si

    ys = []
    for hf in range(2):
        xr = xr_ref[:, hf * half_st:(hf + 1) * half_st].astype(BF16)
        xi = xi_ref[:, hf * half_st:(hf + 1) * half_st].astype(BF16)
        ys.append(jnp.dot(xr, cre_ref[hf], preferred_element_type=F32)
                  - jnp.dot(xi, cim_ref[hf], preferred_element_type=F32))
    y = jnp.concatenate(ys, axis=-1) + dsk_ref[...] * uf
    gl = 0.5 * y * (1.0 + jnp.tanh(math.sqrt(2.0 / math.pi) * (y + 0.044715 * (y * y * y))))
    zz = jnp.dot(gl.astype(BF16), wglu_ref[...], preferred_element_type=F32) + bglu_ref[...]
    out = gl * _sigmoid(zz)
    res = _rms(out, g_ref[...])
    for c in range(n_chunks):
        res_ref[c] = res[:, c * LANES:(c + 1) * LANES]
    for b in range(nb):
        o_ref[b] = jnp.concatenate(
            [res_ref[c, pl.ds(b, tt, stride=nb), :] for c in range(n_chunks)],
            axis=-1).astype(BF16)


def _ssm(u3, bre, bim, cre, cim, a_re, a_im, d_skip, w_glu, b_glu, out_g, *, tt=128):
    nb, s, w = u3.shape
    rows = tt * nb
    kern = functools.partial(_ssm_kernel, tt=tt, nb=nb)
    c3 = lambda i: (0, 0, 0)
    c2 = lambda i: (0, 0)
    return pl.pallas_call(
        kern,
        grid=(s // tt,),
        in_specs=[pl.BlockSpec((nb, tt, w), lambda i: (0, i, 0)),
                  pl.BlockSpec(bre.shape, c3), pl.BlockSpec(bim.shape, c3),
                  pl.BlockSpec(cre.shape, c3), pl.BlockSpec(cim.shape, c3),
                  pl.BlockSpec((1, N_STATE), c2), pl.BlockSpec((1, N_STATE), c2),
                  pl.BlockSpec((1, w), c2),
                  pl.BlockSpec((w, w), c2), pl.BlockSpec((1, w), c2), pl.BlockSpec((1, w), c2)],
        out_specs=pl.BlockSpec((nb, tt, w), lambda i: (0, i, 0)),
        out_shape=jax.ShapeDtypeStruct((nb, s, w), BF16),
        scratch_shapes=[pltpu.VMEM((w // LANES, rows, LANES), F32),
                        pltpu.VMEM((rows, N_STATE), F32), pltpu.VMEM((rows, N_STATE), F32),
                        pltpu.VMEM((nb, N_STATE), F32), pltpu.VMEM((nb, N_STATE), F32),
                        pltpu.VMEM((w // LANES, rows, LANES), F32)],
        compiler_params=pltpu.CompilerParams(
            dimension_semantics=("arbitrary",), vmem_limit_bytes=VMEM_LIMIT),
        name="ssm",
    )(u3, bre, bim, cre, cim, a_re.reshape(1, N_STATE), a_im.reshape(1, N_STATE),
      d_skip.reshape(1, w), w_glu, b_glu.reshape(1, w), out_g.reshape(1, w))


def _outproj_kernel(x_ref, a_ref, s_ref, ga_ref, woa_ref, wos_ref, gf_ref, wrt_ref, br_ref,
                    tri_ref, x1_ref, h2_ref, topi_ref, gate_ref, rank_ref, cnt_ref, *, tm):
    a = _rms(a_ref[...].astype(F32), ga_ref[...]).astype(BF16)
    x1 = (x_ref[...] + jnp.dot(a, woa_ref[...], preferred_element_type=F32)
          + jnp.dot(s_ref[...], wos_ref[...], preferred_element_type=F32))
    x1_ref[...] = x1
    h2 = _rms(x1, gf_ref[...]).astype(BF16)
    h2_ref[...] = h2

    lg = lax.dot_general(wrt_ref[...], h2, _NT,
                         preferred_element_type=F32) + br_ref[...]
    ids = lax.broadcasted_iota(I32, (N_EXPERTS, tm), 0)
    work = lg
    vals, idxs = [], []
    for _ in range(TOP_K):
        m = jnp.max(work, axis=0, keepdims=True)
        idx = jnp.min(jnp.where(work == m, ids, N_EXPERTS), axis=0, keepdims=True)
        vals.append(m)
        idxs.append(idx)
        work = jnp.where(ids == idx, -jnp.inf, work)
    exps = [jnp.exp(v - vals[0]) for v in vals]
    den = exps[0] + exps[1] + exps[2] + exps[3]
    gate_ref[...] = jnp.concatenate([e / den for e in exps], axis=0)
    topi_ref[...] = jnp.concatenate(idxs, axis=0)

    sel = jnp.zeros((N_EXPERTS, tm), F32)
    for idx in idxs:
        sel = sel + jnp.where(ids == idx, 1.0, 0.0)
    before = jnp.dot(sel.astype(BF16), tri_ref[...], preferred_element_type=F32)
    ranks = [jnp.sum(jnp.where(ids == idx, before, 0.0), axis=0, keepdims=True) for idx in idxs]
    rank_ref[...] = jnp.concatenate(ranks, axis=0).astype(I32)
    cnt_ref[0] = jnp.broadcast_to(jnp.sum(sel, axis=1, keepdims=True), (N_EXPERTS, LANES))


def _out_proj(x2, attn, ssm, attn_g, wo_a, wo_s, ffn_g, wrt, b_router, *, tm=512):
    t, d = x2.shape
    tri = jnp.triu(jnp.ones((tm, tm), F32), k=1).astype(BF16)
    kern = functools.partial(_outproj_kernel, tm=tm)
    row = lambda i: (i, 0)
    const = lambda i: (0, 0)
    colblk = lambda i: (0, i)
    return pl.pallas_call(
        kern,
        grid=(t // tm,),
        in_specs=[pl.BlockSpec((tm, d), row),
                  pl.BlockSpec((tm, ATTN_WIDTH), row),
                  pl.BlockSpec((tm, SSM_WIDTH), row),
                  pl.BlockSpec((1, ATTN_WIDTH), const),
                  pl.BlockSpec((ATTN_WIDTH, d), const),
                  pl.BlockSpec((SSM_WIDTH, d), const),
                  pl.BlockSpec((1, d), const),
                  pl.BlockSpec((N_EXPERTS, d), const),
                  pl.BlockSpec((N_EXPERTS, 1), const),
                  pl.BlockSpec((tm, tm), const)],
        out_specs=[pl.BlockSpec((tm, d), row),
                   pl.BlockSpec((tm, d), row),
                   pl.BlockSpec((TOP_K, tm), colblk),
                   pl.BlockSpec((TOP_K, tm), colblk),
                   pl.BlockSpec((TOP_K, tm), colblk),
                   pl.BlockSpec((1, N_EXPERTS, LANES), lambda i: (i, 0, 0))],
        out_shape=[jax.ShapeDtypeStruct((t, d), F32),
                   jax.ShapeDtypeStruct((t, d), BF16),
                   jax.ShapeDtypeStruct((TOP_K, t), I32),
                   jax.ShapeDtypeStruct((TOP_K, t), F32),
                   jax.ShapeDtypeStruct((TOP_K, t), I32),
                   jax.ShapeDtypeStruct((t // tm, N_EXPERTS, LANES), F32)],
        compiler_params=pltpu.CompilerParams(
            dimension_semantics=("arbitrary",), vmem_limit_bytes=VMEM_LIMIT),
        name="out_proj",
    )(x2, attn, ssm, attn_g.reshape(1, -1), wo_a, wo_s, ffn_g.reshape(1, d), wrt,
      b_router.reshape(N_EXPERTS, 1), tri)


SEG_ALIGN = 16
MOE_TILE = 512
LOCAL_ROWS = TOP_K * MOE_TILE + N_EXPERTS * SEG_ALIGN
MAX_CHUNKS = LOCAL_ROWS // SEG_ALIGN
CHUNK_TABLE = 256
TAIL_TABLE = N_EXPERTS * (MOE_ROWS // SEG_ALIGN)
START_GROUP = 4
WAIT_GROUP = 8


def _chunk_copy(src, dst, sem):
    return pltpu.make_async_copy(src, dst, sem)


def _chunk_wait_n(local_buf, hbm, sem, slot, n, to_hbm):
    def wait_rows(rows):
        loc = local_buf.at[slot, pl.ds(0, rows), :]
        far = hbm.at[pl.ds(0, rows), :]

        def body(j, carry):
            (_chunk_copy(loc, far, sem.at[slot]) if to_hbm
             else _chunk_copy(far, loc, sem.at[slot])).wait()
            return carry
        return body

    n_groups = n // WAIT_GROUP
    lax.fori_loop(0, n_groups, wait_rows(WAIT_GROUP * SEG_ALIGN), 0)
    lax.fori_loop(n_groups * WAIT_GROUP, n, wait_rows(SEG_ALIGN), 0)


def _chunk_start_n(local_buf, hbm, tab_ref, sem, slot, n, to_hbm):
    def start(j, priority):
        loc = local_buf.at[slot, pl.ds(pl.multiple_of(j * SEG_ALIGN, SEG_ALIGN), SEG_ALIGN), :]
        far = hbm.at[pl.ds(pl.multiple_of(tab_ref[0, 0, j] * SEG_ALIGN, SEG_ALIGN), SEG_ALIGN), :]
        (_chunk_copy(loc, far, sem.at[slot]) if to_hbm
         else _chunk_copy(far, loc, sem.at[slot])).start(priority=priority)

    def group(g, carry):
        for u in range(START_GROUP):
            start(g * START_GROUP + u, u % 2)
        return carry

    def single(j, carry):
        start(j, 0)
        return carry

    n_groups = n // START_GROUP
    lax.fori_loop(0, n_groups, group, 0)
    lax.fori_loop(n_groups * START_GROUP, n, single, 0)


def _dispatch_kernel(nch_ref, nused_ref, tab_ref, tail_ref, lp_ref, h2_ref, xs_hbm,
                     buf, zbuf, sem, zsem, *, n_tiles, n_blocks):
    i = pl.program_id(0)
    slot = i % 2

    @pl.when(i >= 2)
    def _():
        _chunk_wait_n(buf, xs_hbm, sem, slot, nch_ref[jnp.maximum(i - 2, 0)], True)

    lp = lp_ref[...]
    tm = lp.shape[1]
    rows = lax.broadcasted_iota(I32, (LOCAL_ROWS, tm), 0)
    onehot = jnp.zeros((LOCAL_ROWS, tm), F32)
    for k in range(TOP_K):
        onehot = jnp.where(rows == lp[k:k + 1, :], 1.0, onehot)
    buf[slot] = jnp.dot(onehot.astype(BF16), h2_ref[...],
                        preferred_element_type=F32).astype(BF16)
    _chunk_start_n(buf, xs_hbm, tab_ref, sem, slot, nch_ref[i], True)

    @pl.when(i == n_tiles - 1)
    def _():
        zbuf[...] = jnp.zeros_like(zbuf)
        zrow = zbuf.at[pl.ds(0, SEG_ALIGN), :]

        def fill(j, carry):
            @pl.when(tail_ref[0, 0, j] >= 0)
            def _():
                row0 = pl.multiple_of(tail_ref[0, 0, j] * SEG_ALIGN, SEG_ALIGN)
                _chunk_copy(zrow, xs_hbm.at[pl.ds(row0, SEG_ALIGN), :], zsem).start()
            return carry

        def drain(j, carry):
            @pl.when(tail_ref[0, 0, j] >= 0)
            def _():
                _chunk_copy(zrow, xs_hbm.at[pl.ds(0, SEG_ALIGN), :], zsem).wait()
            return carry

        def fill_block(b, carry):
            row0 = pl.multiple_of(b * MOE_ROWS, MOE_ROWS)
            _chunk_copy(zbuf, xs_hbm.at[pl.ds(row0, MOE_ROWS), :], zsem).start()
            return carry

        def drain_block(b, carry):
            _chunk_copy(zbuf, xs_hbm.at[pl.ds(0, MOE_ROWS), :], zsem).wait()
            return carry

        lax.fori_loop(0, TAIL_TABLE, fill, 0)
        lax.fori_loop(0, TAIL_TABLE, drain, 0)
        lax.fori_loop(nused_ref[0], n_blocks, fill_block, 0)
        lax.fori_loop(nused_ref[0], n_blocks, drain_block, 0)
        _chunk_wait_n(buf, xs_hbm, sem, slot, nch_ref[i], True)
        if n_tiles > 1:
            _chunk_wait_n(buf, xs_hbm, sem, 1 - slot, nch_ref[jnp.maximum(i - 1, 0)], True)


def _dispatch(n_chunks, n_used, chunk_tab, tail_tab, lp, h2, n_blocks):
    k, t = lp.shape
    d = h2.shape[1]
    n_tiles = t // MOE_TILE
    grid_spec = pltpu.PrefetchScalarGridSpec(
        num_scalar_prefetch=2,
        grid=(n_tiles,),
        in_specs=[pl.BlockSpec((1, 1, CHUNK_TABLE), lambda i, nc, nu: (i, 0, 0),
                               memory_space=pltpu.SMEM),
                  pl.BlockSpec((1, 1, TAIL_TABLE), lambda i, nc, nu: (0, 0, 0),
                               memory_space=pltpu.SMEM),
                  pl.BlockSpec((k, MOE_TILE), lambda i, nc, nu: (0, i)),
                  pl.BlockSpec((MOE_TILE, d), lambda i, nc, nu: (i, 0))],
        out_specs=pl.BlockSpec(memory_space=pl.ANY),
        scratch_shapes=[pltpu.VMEM((2, LOCAL_ROWS, d), BF16), pltpu.VMEM((MOE_ROWS, d), BF16),
                        pltpu.SemaphoreType.DMA((2,)), pltpu.SemaphoreType.DMA],
    )
    return pl.pallas_call(
        functools.partial(_dispatch_kernel, n_tiles=n_tiles, n_blocks=n_blocks),
        grid_spec=grid_spec,
        out_shape=jax.ShapeDtypeStruct((n_blocks * MOE_ROWS, d), BF16),
        compiler_params=pltpu.CompilerParams(
            dimension_semantics=("arbitrary",), vmem_limit_bytes=VMEM_LIMIT),
        name="dispatch",
    )(n_chunks, n_used, chunk_tab, tail_tab, lp, h2)


def _expert_mlp_kernel(blk_e_ref, nused_ref, x_ref, wgu_ref, bgu_ref, wdn_ref, bdn_ref, y_ref,
                       wgu_bf, wdn_bf):
    i = pl.program_id(0)
    nused = nused_ref[0]
    new_expert = jnp.logical_or(i == 0, blk_e_ref[i] != blk_e_ref[jnp.maximum(i - 1, 0)])

    @pl.when(jnp.logical_and(new_expert, i < nused))
    def _():
        wgu_bf[...] = wgu_ref[...].astype(BF16)
        wdn_bf[...] = wdn_ref[...].astype(BF16)

    @pl.when(i < nused)
    def _():
        x = x_ref[...]
        gu = jnp.dot(x, wgu_bf[...], preferred_element_type=F32) + bgu_ref[...]
        d_ff = gu.shape[1] // 2
        gate = jnp.minimum(gu[:, :d_ff], SWIGLU_LIMIT)
        up = jnp.clip(gu[:, d_ff:], -SWIGLU_LIMIT, SWIGLU_LIMIT)
        act = gate * _sigmoid(SWIGLU_ALPHA * gate) * (up + 1.0)
        y = jnp.dot(act.astype(BF16), wdn_bf[...], preferred_element_type=F32) + bdn_ref[...]
        y_ref[...] = y.astype(BF16)

    @pl.when(i >= nused)
    def _():
        y_ref[...] = jnp.zeros_like(y_ref)


def _expert_mlp(blk_expert, n_used, xs, w_gu, b_gu, w_dn, b_dn):
    d = xs.shape[1]
    n_blocks = blk_expert.shape[0]
    n_rows = n_blocks * MOE_ROWS
    e, _, f2 = w_gu.shape
    grid_spec = pltpu.PrefetchScalarGridSpec(
        num_scalar_prefetch=2,
        grid=(n_blocks,),
        in_specs=[
            pl.BlockSpec((MOE_ROWS, d), lambda i, be, nu: (jnp.minimum(i, nu[0] - 1), 0)),
            pl.BlockSpec((None, d, f2), lambda i, be, nu: (be[i], 0, 0)),
            pl.BlockSpec((None, 1, f2), lambda i, be, nu: (be[i], 0, 0)),
            pl.BlockSpec((None, f2 // 2, d), lambda i, be, nu: (be[i], 0, 0)),
            pl.BlockSpec((None, 1, d), lambda i, be, nu: (be[i], 0, 0)),
        ],
        out_specs=pl.BlockSpec((MOE_ROWS, d), lambda i, be, nu: (i, 0)),
        scratch_shapes=[pltpu.VMEM((d, f2), BF16), pltpu.VMEM((f2 // 2, d), BF16)],
    )
    return pl.pallas_call(
        _expert_mlp_kernel,
        grid_spec=grid_spec,
        out_shape=jax.ShapeDtypeStruct((n_rows, d), BF16),
        compiler_params=pltpu.CompilerParams(
            dimension_semantics=("arbitrary",), vmem_limit_bytes=VMEM_LIMIT),
        name="experts",
    )(blk_expert, n_used, xs, w_gu, b_gu.reshape(e, 1, f2), w_dn, b_dn.reshape(e, 1, d))


def _moe_combine_kernel(nch_ref, tab_cur_ref, tab_nxt_ref, y_hbm, lpt_ref, gate_ref, x1_ref,
                        p_ref, gp_ref, wg_ref, wp_ref, gfin_ref, o_ref, buf, sem, *, n_tiles):
    i = pl.program_id(0)
    slot = i % 2

    @pl.when(i == 0)
    def _():
        buf[...] = jnp.zeros_like(buf)
        _chunk_start_n(buf, y_hbm, tab_cur_ref, sem, 0, nch_ref[0], False)

    @pl.when(i + 1 < n_tiles)
    def _():
        _chunk_start_n(buf, y_hbm, tab_nxt_ref, sem, 1 - slot,
                       nch_ref[jnp.minimum(i + 1, n_tiles - 1)], False)

    _chunk_wait_n(buf, y_hbm, sem, slot, nch_ref[i], False)
    lpt = lpt_ref[...]
    gates = gate_ref[...]
    tm = lpt.shape[0]
    cols = lax.broadcasted_iota(I32, (tm, LOCAL_ROWS), 1)
    weights = jnp.zeros((tm, LOCAL_ROWS), F32)
    for k in range(TOP_K):
        weights = jnp.where(cols == lpt[:, k:k + 1], gates[:, k:k + 1], weights)
    x2 = x1_ref[...] + jnp.dot(weights.astype(BF16), buf[slot], preferred_element_type=F32)
    hg = _rms(x2, gp_ref[...]).astype(BF16)
    gate = _sigmoid(jnp.dot(hg, wg_ref[...], preferred_element_type=F32))
    emb = jnp.dot(p_ref[...].astype(BF16), wp_ref[...], preferred_element_type=F32)
    x3 = x2 + gate * emb
    o_ref[...] = _rms(x3, gfin_ref[...])


def _moe_combine(n_chunks, chunk_tab, y, lpt, gates_tk, x1, p2, ple_g, w_gate, w_proj, fin_g):
    t, d = x1.shape
    n_tiles = t // MOE_TILE
    last = n_tiles - 1
    ple = p2.shape[1]
    row = lambda i, nc: (i, 0)
    const = lambda i, nc: (0, 0)
    grid_spec = pltpu.PrefetchScalarGridSpec(
        num_scalar_prefetch=1,
        grid=(n_tiles,),
        in_specs=[pl.BlockSpec((1, 1, CHUNK_TABLE), lambda i, nc: (i, 0, 0),
                               memory_space=pltpu.SMEM),
                  pl.BlockSpec((1, 1, CHUNK_TABLE),
                               lambda i, nc: (jnp.minimum(i + 1, last), 0, 0),
                               memory_space=pltpu.SMEM),
                  pl.BlockSpec(memory_space=pl.ANY),
                  pl.BlockSpec((MOE_TILE, TOP_K), row),
                  pl.BlockSpec((MOE_TILE, TOP_K), row),
                  pl.BlockSpec((MOE_TILE, d), row),
                  pl.BlockSpec((MOE_TILE, ple), row),
                  pl.BlockSpec((1, d), const),
                  pl.BlockSpec((d, d), const),
                  pl.BlockSpec((ple, d), const),
                  pl.BlockSpec((1, d), const)],
        out_specs=pl.BlockSpec((MOE_TILE, d), row),
        scratch_shapes=[pltpu.VMEM((2, LOCAL_ROWS, d), BF16), pltpu.SemaphoreType.DMA((2,))],
    )
    return pl.pallas_call(
        functools.partial(_moe_combine_kernel, n_tiles=n_tiles),
        grid_spec=grid_spec,
        out_shape=jax.ShapeDtypeStruct((t, d), F32),
        compiler_params=pltpu.CompilerParams(
            dimension_semantics=("arbitrary",), vmem_limit_bytes=VMEM_LIMIT),
        name="combine",
    )(n_chunks, chunk_tab, chunk_tab, y, lpt, gates_tk, x1, p2, ple_g.reshape(1, d),
      w_gate, w_proj, fin_g.reshape(1, d))


def _moe_tables(topi, lrank, cnt):
    k, t = topi.shape
    n_tiles = t // MOE_TILE
    cnt = cnt.astype(I32)
    pc = (cnt + SEG_ALIGN - 1) // SEG_ALIGN * SEG_ALIGN
    l_end = jnp.cumsum(pc, axis=1)
    l_start = l_end - pc
    tot = jnp.sum(pc, axis=0)
    e_pad = (tot + MOE_ROWS - 1) // MOE_ROWS * MOE_ROWS
    e_end = jnp.cumsum(e_pad)
    seg_row0 = (e_end - e_pad)[None, :] + jnp.cumsum(pc, axis=0) - pc
    eids = jnp.arange(N_EXPERTS, dtype=I32)[:, None, None, None]
    topi3 = topi.reshape(k, n_tiles, MOE_TILE)
    base = jnp.sum(jnp.where(topi3[None] == eids, l_start.T[:, None, :, None], 0), axis=0)
    lp = (lrank.reshape(k, n_tiles, MOE_TILE) + base).reshape(k, t)
    j16 = jnp.arange(CHUNK_TABLE, dtype=I32)[None, :] * SEG_ALIGN
    ce = jnp.minimum(jnp.sum((l_end[:, :, None] <= j16[:, None, :]).astype(I32), axis=1),
                     N_EXPERTS - 1)
    pick = ce[:, None, :] == jnp.arange(N_EXPERTS, dtype=I32)[None, :, None]
    seg0 = jnp.sum(jnp.where(pick, seg_row0[:, :, None], 0), axis=1)
    loc0 = jnp.sum(jnp.where(pick, l_start[:, :, None], 0), axis=1)
    n_chunks = l_end[:, -1] // SEG_ALIGN
    valid = jnp.arange(CHUNK_TABLE, dtype=I32)[None, :] < n_chunks[:, None]
    tab = jnp.where(valid, (seg0 + j16 - loc0) // SEG_ALIGN, 0).astype(I32)
    n_rows_max = k * t + n_tiles * N_EXPERTS * (SEG_ALIGN - 1) + N_EXPERTS * (MOE_ROWS - 1)
    n_blocks = (n_rows_max + MOE_ROWS - 1) // MOE_ROWS
    n_used = (e_end[-1] // MOE_ROWS).astype(I32).reshape(1)
    blk_row0 = jnp.arange(n_blocks, dtype=I32) * MOE_ROWS
    blk_expert = jnp.minimum(
        jnp.sum((e_end[None, :] <= blk_row0[:, None]).astype(I32), axis=1), N_EXPERTS - 1)
    per_e = MOE_ROWS // SEG_ALIGN
    tj = jnp.arange(per_e, dtype=I32)[None, :]
    tail = jnp.where(tj < ((e_pad - tot) // SEG_ALIGN)[:, None],
                     ((e_end - e_pad + tot) // SEG_ALIGN)[:, None] + tj, -1).astype(I32)
    return (lp, n_chunks.astype(I32), tab.reshape(n_tiles, 1, CHUNK_TABLE),
            tail.reshape(1, 1, TAIL_TABLE), blk_expert, n_used)


def _layer(x2, p2, seq, norm_mix_g, w_in, b_f, lam_re, lam_im, log_dt, b_re, b_im, c_re, c_im,
           d_skip, w_glu, b_glu, attn_out_g, ssm_out_g, w_out, norm_ffn_g, w_router, b_router,
           w_gu, b_gu, w_dn, b_dn, norm_ple_g, w_ple_gate, w_ple_proj, final_g):
    t, d = x2.shape
    nb = t // seq
    aw = ATTN_WIDTH
    w_main = jnp.concatenate([w_in[:, :3 * aw], w_in[:, 3 * aw + N_HEADS:]], axis=1).astype(BF16)
    wft = w_in[:, 3 * aw:3 * aw + N_HEADS].T.astype(BF16)

    a_re, a_im, bbr, bbi = _ssm_prep(lam_re, lam_im, log_dt, b_re, b_im)
    bre = _block_diag(bbr).astype(BF16)
    bim = _block_diag(bbi).astype(BF16)
    cre = _block_diag(jnp.transpose(c_re, (0, 2, 1))).astype(BF16)
    cim = _block_diag(jnp.transpose(c_im, (0, 2, 1))).astype(BF16)

    q, k, v, u, c_t = _in_proj(x2, norm_mix_g, w_main, wft, b_f, seq=seq)
    shp = (nb, seq, aw)
    attn = _attention(q.reshape(shp), k.reshape(shp), v.reshape(shp), c_t)
    ssm = _ssm(u.reshape(nb, seq, SSM_WIDTH), bre, bim, cre, cim, a_re, a_im,
               d_skip, w_glu.astype(BF16), b_glu, ssm_out_g)

    x1, h2, topi, gates, lrank, cnt = _out_proj(
        x2, attn.reshape(t, aw), ssm.reshape(t, SSM_WIDTH), attn_out_g,
        w_out[:aw].astype(BF16), w_out[aw:].astype(BF16), norm_ffn_g,
        w_router.T.astype(BF16), b_router, tm=MOE_TILE)

    lp, n_chunks, chunk_tab, tail_tab, blk_expert, n_used = _moe_tables(topi, lrank, cnt[:, :, 0])
    xs = _dispatch(n_chunks, n_used, chunk_tab, tail_tab, lp, h2, blk_expert.shape[0])
    y = _expert_mlp(blk_expert, n_used, xs, w_gu, b_gu, w_dn, b_dn)
    return _moe_combine(n_chunks, chunk_tab, y, lp.T, gates.T, x1, p2, norm_ple_g,
                        w_ple_gate.astype(BF16), w_ple_proj.astype(BF16), final_g)


def kernel(x, p, norm_mix_g, w_in, b_f, lam_re, lam_im, log_dt, b_re, b_im, c_re, c_im, d_skip, w_glu, b_glu, attn_out_g, ssm_out_g, w_out, norm_ffn_g, w_router, b_router, w_gu, b_gu, w_dn, b_dn, norm_ple_g, w_ple_gate, w_ple_proj, norm_final_g):
    bsz, seq, d = x.shape
    depth = w_in.shape[0]
    assert depth == 1, "one layer: the final rmsnorm is fused into the layer's last kernel"
    out = _layer(x.reshape(bsz * seq, d), p[0].reshape(bsz * seq, -1), seq,
                 norm_mix_g[0], w_in[0], b_f[0], lam_re[0], lam_im[0], log_dt[0],
                 b_re[0], b_im[0], c_re[0], c_im[0], d_skip[0], w_glu[0], b_glu[0],
                 attn_out_g[0], ssm_out_g[0], w_out[0], norm_ffn_g[0], w_router[0],
                 b_router[0], w_gu[0], b_gu[0], w_dn[0], b_dn[0], norm_ple_g[0],
                 w_ple_gate[0], w_ple_proj[0], norm_final_g)
    return out.reshape(bsz, seq, d)
```

```python
import functools
import math

import jax
import jax.numpy as jnp
from jax import lax
from jax.experimental import pallas as pl
from jax.experimental.pallas import tpu as pltpu

F32 = jnp.float32
BF16 = jnp.bfloat16
I32 = jnp.int32

NORM_EPS = 1e-5
HEAD_DIM = 64
N_HEADS = 8
ATTN_WIDTH = 512
SSM_WIDTH = 512
SSM_GROUP = 16
N_SSM_GROUPS = 32
SSM_STATE = 64
N_STATE = N_SSM_GROUPS * SSM_STATE
N_EXPERTS = 32
TOP_K = 4
SWIGLU_LIMIT = 7.0
SWIGLU_ALPHA = 1.702
LANES = 128
MOE_ROWS = 1024
NEG_BIG = -1e30
LOG2E = math.log2(math.e)
VMEM_LIMIT = 56 * 1024 * 1024

_NT = (((1,), (1,)), ((), ()))


def _rms(xf, g):
    ms = jnp.mean(xf * xf, axis=-1, keepdims=True)
    return xf * lax.rsqrt(ms + NORM_EPS) * g


def _sigmoid(x):
    return 1.0 / (1.0 + jnp.exp(-x))


def _ssm_prep_kernel(lr_ref, li_ref, ldt_ref, brt_ref, bit_ref,
                     ar_ref, ai_ref, bbr_ref, bbi_ref):
    lr = lr_ref[...]
    li = li_ref[...]
    dt = jnp.exp(ldt_ref[...])
    mag = jnp.exp(lr * dt)
    ab_re = mag * jnp.cos(li * dt)
    ab_im = mag * jnp.sin(li * dt)
    den = lr * lr + li * li
    nr = ab_re - 1.0
    z_re = (nr * lr + ab_im * li) / den
    z_im = (ab_im * lr - nr * li) / den
    ar_ref[...] = ab_re
    ai_ref[...] = ab_im
    br = brt_ref[...]
    bi = bit_ref[...]
    bbr_ref[...] = z_re * br - z_im * bi
    bbi_ref[...] = z_re * bi + z_im * br


def _ssm_prep(lam_re, lam_im, log_dt, b_re, b_im):
    g, p, c = b_re.shape
    brt = jnp.transpose(b_re, (0, 2, 1))
    bit = jnp.transpose(b_im, (0, 2, 1))
    return pl.pallas_call(
        _ssm_prep_kernel,
        out_shape=(jax.ShapeDtypeStruct((g, 1, p), F32), jax.ShapeDtypeStruct((g, 1, p), F32),
                   jax.ShapeDtypeStruct((g, c, p), F32), jax.ShapeDtypeStruct((g, c, p), F32)),
        name="ssm_prep",
    )(lam_re.reshape(g, 1, p), lam_im.reshape(g, 1, p), log_dt.reshape(g, 1, 1), brt, bit)


def _block_diag(w):
    g, a, b = w.shape
    half = g // 2
    eye = jnp.eye(half, dtype=w.dtype)
    w4 = w.reshape(2, half, a, b)
    out = jnp.einsum('hgab,gk->hgakb', w4, eye)
    return out.reshape(2, half * a, half * b)


def _inproj_kernel(x_ref, g_ref, w_ref, wft_ref, bf_ref, tri_ref,
                   q_ref, k_ref, v_ref, u_ref, c_ref, carry_ref, *, tiles_per_seq, tm):
    i = pl.program_id(0)

    @pl.when(i % tiles_per_seq == 0)
    def _():
        carry_ref[...] = jnp.zeros_like(carry_ref)

    h = _rms(x_ref[...], g_ref[...]).astype(BF16)
    proj = jnp.dot(h, w_ref[...], preferred_element_type=F32)
    aw = ATTN_WIDTH
    q_ref[...] = (proj[:, 0:aw] * (LOG2E * HEAD_DIM ** -0.5)).astype(BF16)
    k_ref[...] = proj[:, aw:2 * aw].astype(BF16)
    v_ref[...] = proj[:, 2 * aw:3 * aw].astype(BF16)
    u_ref[...] = proj[:, 3 * aw:3 * aw + SSM_WIDTH].astype(BF16)

    fl = lax.dot_general(wft_ref[...], h, _NT, preferred_element_type=F32)
    z = fl + bf_ref[...]
    lf = jnp.minimum(z, 0.0) - jnp.log1p(jnp.exp(-jnp.abs(z)))
    hi = lf.astype(BF16)
    lo = (lf - hi.astype(F32)).astype(BF16)
    tri = tri_ref[...]
    cs = (jnp.dot(hi, tri, preferred_element_type=F32)
          + jnp.dot(lo, tri, preferred_element_type=F32))
    c = cs + carry_ref[:, 0:1]
    c_ref[...] = c * LOG2E
    carry_ref[...] = jnp.broadcast_to(c[:, tm - 1:tm], carry_ref.shape)


def _in_proj(x2, norm_g, w_main, wft, b_f, *, seq, tm=1024):
    t, d = x2.shape
    n_main = w_main.shape[1]
    tri = jnp.triu(jnp.ones((tm, tm), F32)).astype(BF16)
    kern = functools.partial(_inproj_kernel, tiles_per_seq=seq // tm, tm=tm)
    row = lambda i: (i, 0)
    const = lambda i: (0, 0)
    act = jax.ShapeDtypeStruct((t, ATTN_WIDTH), BF16)
    return pl.pallas_call(
        kern,
        grid=(t // tm,),
        in_specs=[pl.BlockSpec((tm, d), row),
                  pl.BlockSpec((1, d), const),
                  pl.BlockSpec((d, n_main), const),
                  pl.BlockSpec((N_HEADS, d), const),
                  pl.BlockSpec((N_HEADS, 1), const),
                  pl.BlockSpec((tm, tm), const)],
        out_specs=[pl.BlockSpec((tm, ATTN_WIDTH), row)] * 4
        + [pl.BlockSpec((N_HEADS, tm), lambda i: (0, i))],
        out_shape=[act, act, act, act, jax.ShapeDtypeStruct((N_HEADS, t), F32)],
        scratch_shapes=[pltpu.VMEM((N_HEADS, LANES), F32)],
        compiler_params=pltpu.CompilerParams(
            dimension_semantics=("arbitrary",), vmem_limit_bytes=VMEM_LIMIT),
        name="in_proj",
    )(x2, norm_g.reshape(1, d), w_main, wft, b_f.reshape(N_HEADS, 1), tri)


def _attn_kernel(q_ref, k_ref, v_ref, c_ref, o_ref, *, tq, n_pairs):
    i = pl.program_id(2)
    lane = lax.broadcasted_iota(I32, (1, LANES), 1)
    first = lane < HEAD_DIM
    n_heads = 2 * n_pairs
    q_heads = []
    for p in range(n_pairs):
        q2 = q_ref[:, p * LANES:(p + 1) * LANES]
        zero = jnp.zeros_like(q2)
        q_heads += [jnp.where(first, q2, zero), jnp.where(first, zero, q2)]
    half = tq // 2

    def block(off, width, r0, carry, masked):
        cj = c_ref[:, pl.ds(off, width)]
        out = []
        for p in range(n_pairs):
            kj = k_ref[pl.ds(off, width), p * LANES:(p + 1) * LANES]
            vj = v_ref[pl.ds(off, width), p * LANES:(p + 1) * LANES]
            one = jnp.ones_like(vj)
            v_heads = (jnp.where(first, vj, one), jnp.where(first, one, vj))
            for h in range(2):
                m, acc = carry[2 * p + h]
                s = (lax.dot_general(q_heads[2 * p + h][r0:], kj, _NT,
                                     preferred_element_type=F32)
                     - cj[2 * p + h:2 * p + h + 1, :])
                if masked:
                    rr = lax.broadcasted_iota(I32, s.shape, 0)
                    cc = lax.broadcasted_iota(I32, s.shape, 1)
                    s = jnp.where(cc <= rr, s, NEG_BIG)
                m_new = jnp.maximum(m[r0:], jnp.max(s, axis=-1, keepdims=True))
                alpha = jnp.exp2(m[r0:] - m_new)
                pr = jnp.exp2(s - m_new).astype(BF16)
                acc_new = alpha * acc[r0:] + jnp.dot(pr, v_heads[h], preferred_element_type=F32)
                if r0:
                    m_new = jnp.concatenate([m[:r0], m_new], axis=0)
                    acc_new = jnp.concatenate([acc[:r0], acc_new], axis=0)
                out.append((m_new, acc_new))
        return tuple(out)

    init_one = (jnp.full((tq, 1), NEG_BIG, F32), jnp.zeros((tq, LANES), F32))
    carry = lax.fori_loop(
        0, i, lambda j, c: block(pl.multiple_of(j * tq, tq), tq, 0, c, False),
        (init_one,) * n_heads)
    diag = pl.multiple_of(i * tq, tq)
    for n in range(tq // half):
        carry = block(pl.multiple_of(diag + n * half, half), half, n * half, carry, True)
    outs = []
    for p in range(n_pairs):
        acc_a = carry[2 * p][1]
        acc_b = carry[2 * p + 1][1]
        outs.append(jnp.where(first, acc_a / pltpu.roll(acc_a, HEAD_DIM, axis=1),
                              acc_b / pltpu.roll(acc_b, HEAD_DIM, axis=1)))
    o_ref[...] = jnp.concatenate(outs, axis=-1).astype(BF16)


def _attention(q, k, v, c_t, *, tq=1024, n_pairs=2):
    b, s, w = q.shape
    wb = n_pairs * LANES
    n_groups = w // wb
    c3 = c_t.reshape(n_groups, 2 * n_pairs, b * s)
    kern = functools.partial(_attn_kernel, tq=tq, n_pairs=n_pairs)
    return pl.pallas_call(
        kern,
        grid=(b, n_groups, s // tq),
        in_specs=[pl.BlockSpec((None, tq, wb), lambda bi, g, i: (bi, i, g)),
                  pl.BlockSpec((None, s, wb), lambda bi, g, i: (bi, 0, g)),
                  pl.BlockSpec((None, s, wb), lambda bi, g, i: (bi, 0, g)),
                  pl.BlockSpec((None, 2 * n_pairs, s), lambda bi, g, i: (g, 0, bi))],
        out_specs=pl.BlockSpec((None, tq, wb), lambda bi, g, i: (bi, i, g)),
        out_shape=jax.ShapeDtypeStruct((b, s, w), BF16),
        compiler_params=pltpu.CompilerParams(
            dimension_semantics=("arbitrary", "arbitrary", "arbitrary"),
            vmem_limit_bytes=VMEM_LIMIT),
        name="attention",
    )(q, k, v, c3)


def _ssm_kernel(u_ref, bre_ref, bim_ref, cre_ref, cim_ref, ar_ref, ai_ref, dsk_ref,
                wglu_ref, bglu_ref, g_ref, o_ref,
                us_ref, xr_ref, xi_ref, str_ref, sti_ref, res_ref, *, tt, nb):
    i = pl.program_id(0)

    @pl.when(i == 0)
    def _():
        str_ref[...] = jnp.zeros_like(str_ref)
        sti_ref[...] = jnp.zeros_like(sti_ref)

    n_chunks = SSM_WIDTH // LANES
    for b in range(nb):
        ub32 = u_ref[b].astype(F32)
        for c in range(n_chunks):
            us_ref[c, pl.ds(b, tt, stride=nb), :] = ub32[:, c * LANES:(c + 1) * LANES]
    uf = jnp.concatenate([us_ref[c] for c in range(n_chunks)], axis=-1)
    ub = uf.astype(BF16)
    half_in = SSM_WIDTH // 2
    half_st = N_STATE // 2
    for hf in range(2):
        uh = ub[:, hf * half_in:(hf + 1) * half_in]
        xr_ref[:, hf * half_st:(hf + 1) * half_st] = jnp.dot(
            uh, bre_ref[hf], preferred_element_type=F32)
        xi_ref[:, hf * half_st:(hf + 1) * half_st] = jnp.dot(
            uh, bim_ref[hf], preferred_element_type=F32)

    n_col_groups = 2
    wcol = N_STATE // n_col_groups
    unroll = 4
    for cg in range(n_col_groups):
        cols = slice(cg * wcol, (cg + 1) * wcol)
        ar = jnp.broadcast_to(ar_ref[:, cols], (nb, wcol))
        ai = jnp.broadcast_to(ai_ref[:, cols], (nb, wcol))

        def steps(tb, carry, cols=cols, ar=ar, ai=ai):
            sr, si = carry
            for k in range(unroll):
                r0 = pl.multiple_of((tb * unroll + k) * nb, nb)
                br = xr_ref[pl.ds(r0, nb), cols]
                bi = xi_ref[pl.ds(r0, nb), cols]
                nr = ar * sr - ai * si + br
                ni = ar * si + ai * sr + bi
                xr_ref[pl.ds(r0, nb), cols] = nr
                xi_ref[pl.ds(r0, nb), cols] = ni
                sr, si = nr, ni
            return sr, si

        sr, si = lax.fori_loop(0, tt // unroll, steps, (str_ref[:, cols], sti_ref[:, cols]))
        str_ref[:, cols] = sr
        sti_ref[:, cols] = si

    ys = []
    for hf in range(2):
        xr = xr_ref[:, hf * half_st:(hf + 1) * half_st].astype(BF16)
        xi = xi_ref[:, hf * half_st:(hf + 1) * half_st].astype(BF16)
        ys.append(jnp.dot(xr, cre_ref[hf], preferred_element_type=F32)
                  - jnp.dot(xi, cim_ref[hf], preferred_element_type=F32))
    y = jnp.concatenate(ys, axis=-1) + dsk_ref[...] * uf
    gl = 0.5 * y * (1.0 + jnp.tanh(math.sqrt(2.0 / math.pi) * (y + 0.044715 * (y * y * y))))
    zz = jnp.dot(gl.astype(BF16), wglu_ref[...], preferred_element_type=F32) + bglu_ref[...]
    out = gl * _sigmoid(zz)
    res = _rms(out, g_ref[...])
    for c in range(n_chunks):
        res_ref[c] = res[:, c * LANES:(c + 1) * LANES]
    for b in range(nb):
        o_ref[b] = jnp.concatenate(
            [res_ref[c, pl.ds(b, tt, stride=nb), :] for c in range(n_chunks)],
            axis=-1).astype(BF16)


def _ssm(u3, bre, bim, cre, cim, a_re, a_im, d_skip, w_glu, b_glu, out_g, *, tt=128):
    nb, s, w = u3.shape
    rows = tt * nb
    kern = functools.partial(_ssm_kernel, tt=tt, nb=nb)
    c3 = lambda i: (0, 0, 0)
    c2 = lambda i: (0, 0)
    return pl.pallas_call(
        kern,
        grid=(s // tt,),
        in_specs=[pl.BlockSpec((nb, tt, w), lambda i: (0, i, 0)),
                  pl.BlockSpec(bre.shape, c3), pl.BlockSpec(bim.shape, c3),
                  pl.BlockSpec(cre.shape, c3), pl.BlockSpec(cim.shape, c3),
                  pl.BlockSpec((1, N_STATE), c2), pl.BlockSpec((1, N_STATE), c2),
                  pl.BlockSpec((1, w), c2),
                  pl.BlockSpec((w, w), c2), pl.BlockSpec((1, w), c2), pl.BlockSpec((1, w), c2)],
        out_specs=pl.BlockSpec((nb, tt, w), lambda i: (0, i, 0)),
        out_shape=jax.ShapeDtypeStruct((nb, s, w), BF16),
        scratch_shapes=[pltpu.VMEM((w // LANES, rows, LANES), F32),
                        pltpu.VMEM((rows, N_STATE), F32), pltpu.VMEM((rows, N_STATE), F32),
                        pltpu.VMEM((nb, N_STATE), F32), pltpu.VMEM((nb, N_STATE), F32),
                        pltpu.VMEM((w // LANES, rows, LANES), F32)],
        compiler_params=pltpu.CompilerParams(
            dimension_semantics=("arbitrary",), vmem_limit_bytes=VMEM_LIMIT),
        name="ssm",
    )(u3, bre, bim, cre, cim, a_re.reshape(1, N_STATE), a_im.reshape(1, N_STATE),
      d_skip.reshape(1, w), w_glu, b_glu.reshape(1, w), out_g.reshape(1, w))


def _outproj_kernel(x_ref, a_ref, s_ref, ga_ref, woa_ref, wos_ref, gf_ref, wrt_ref, br_ref,
                    tri_ref, x1_ref, h2_ref, topi_ref, gate_ref, rank_ref, cnt_ref, *, tm):
    a = _rms(a_ref[...].astype(F32), ga_ref[...]).astype(BF16)
    x1 = (x_ref[...] + jnp.dot(a, woa_ref[...], preferred_element_type=F32)
          + jnp.dot(s_ref[...], wos_ref[...], preferred_element_type=F32))
    x1_ref[...] = x1
    h2 = _rms(x1, gf_ref[...]).astype(BF16)
    h2_ref[...] = h2

    lg = lax.dot_general(wrt_ref[...], h2, _NT,
                         preferred_element_type=F32) + br_ref[...]
    ids = lax.broadcasted_iota(I32, (N_EXPERTS, tm), 0)
    work = lg
    vals, idxs = [], []
    for _ in range(TOP_K):
        m = jnp.max(work, axis=0, keepdims=True)
        idx = jnp.min(jnp.where(work == m, ids, N_EXPERTS), axis=0, keepdims=True)
        vals.append(m)
        idxs.append(idx)
        work = jnp.where(ids == idx, -jnp.inf, work)
    exps = [jnp.exp(v - vals[0]) for v in vals]
    den = exps[0] + exps[1] + exps[2] + exps[3]
    gate_ref[...] = jnp.concatenate([e / den for e in exps], axis=0)
    topi_ref[...] = jnp.concatenate(idxs, axis=0)

    sel = jnp.zeros((N_EXPERTS, tm), F32)
    for idx in idxs:
        sel = sel + jnp.where(ids == idx, 1.0, 0.0)
    before = jnp.dot(sel.astype(BF16), tri_ref[...], preferred_element_type=F32)
    ranks = [jnp.sum(jnp.where(ids == idx, before, 0.0), axis=0, keepdims=True) for idx in idxs]
    rank_ref[...] = jnp.concatenate(ranks, axis=0).astype(I32)
    cnt_ref[0] = jnp.broadcast_to(jnp.sum(sel, axis=1, keepdims=True), (N_EXPERTS, LANES))


def _out_proj(x2, attn, ssm, attn_g, wo_a, wo_s, ffn_g, wrt, b_router, *, tm=512):
    t, d = x2.shape
    tri = jnp.triu(jnp.ones((tm, tm), F32), k=1).astype(BF16)
    kern = functools.partial(_outproj_kernel, tm=tm)
    row = lambda i: (i, 0)
    const = lambda i: (0, 0)
    colblk = lambda i: (0, i)
    return pl.pallas_call(
        kern,
        grid=(t // tm,),
        in_specs=[pl.BlockSpec((tm, d), row),
                  pl.BlockSpec((tm, ATTN_WIDTH), row),
                  pl.BlockSpec((tm, SSM_WIDTH), row),
                  pl.BlockSpec((1, ATTN_WIDTH), const),
                  pl.BlockSpec((ATTN_WIDTH, d), const),
                  pl.BlockSpec((SSM_WIDTH, d), const),
                  pl.BlockSpec((1, d), const),
                  pl.BlockSpec((N_EXPERTS, d), const),
                  pl.BlockSpec((N_EXPERTS, 1), const),
                  pl.BlockSpec((tm, tm), const)],
        out_specs=[pl.BlockSpec((tm, d), row),
                   pl.BlockSpec((tm, d), row),
                   pl.BlockSpec((TOP_K, tm), colblk),
                   pl.BlockSpec((TOP_K, tm), colblk),
                   pl.BlockSpec((TOP_K, tm), colblk),
                   pl.BlockSpec((1, N_EXPERTS, LANES), lambda i: (i, 0, 0))],
        out_shape=[jax.ShapeDtypeStruct((t, d), F32),
                   jax.ShapeDtypeStruct((t, d), BF16),
                   jax.ShapeDtypeStruct((TOP_K, t), I32),
                   jax.ShapeDtypeStruct((TOP_K, t), F32),
                   jax.ShapeDtypeStruct((TOP_K, t), I32),
                   jax.ShapeDtypeStruct((t // tm, N_EXPERTS, LANES), F32)],
        compiler_params=pltpu.CompilerParams(
            dimension_semantics=("arbitrary",), vmem_limit_bytes=VMEM_LIMIT),
        name="out_proj",
    )(x2, attn, ssm, attn_g.reshape(1, -1), wo_a, wo_s, ffn_g.reshape(1, d), wrt,
      b_router.reshape(N_EXPERTS, 1), tri)


SEG_ALIGN = 16
MOE_TILE = 512
LOCAL_ROWS = TOP_K * MOE_TILE + N_EXPERTS * SEG_ALIGN
LOCAL_MAIN = TOP_K * MOE_TILE + N_EXPERTS * SEG_ALIGN // 2
CHUNK_TABLE = 256
TAIL_TABLE = N_EXPERTS * (MOE_ROWS // SEG_ALIGN)
START_GROUP = 4
WAIT_GROUP = 8


def _chunk_copy(src, dst, sem):
    return pltpu.make_async_copy(src, dst, sem)


def _chunk_wait_n(local_buf, hbm, sem, slot, n, to_hbm):
    def wait_rows(rows):
        loc = local_buf.at[slot, pl.ds(0, rows), :]
        far = hbm.at[pl.ds(0, rows), :]

        def body(j, carry):
            (_chunk_copy(loc, far, sem.at[slot]) if to_hbm
             else _chunk_copy(far, loc, sem.at[slot])).wait()
            return carry
        return body

    n_groups = n // WAIT_GROUP
    lax.fori_loop(0, n_groups, wait_rows(WAIT_GROUP * SEG_ALIGN), 0)
    lax.fori_loop(n_groups * WAIT_GROUP, n, wait_rows(SEG_ALIGN), 0)


def _chunk_start_n(local_buf, hbm, tab_ref, sem, slot, n, to_hbm):
    def start(j, priority):
        loc = local_buf.at[slot, pl.ds(pl.multiple_of(j * SEG_ALIGN, SEG_ALIGN), SEG_ALIGN), :]
        far = hbm.at[pl.ds(pl.multiple_of(tab_ref[0, 0, j] * SEG_ALIGN, SEG_ALIGN), SEG_ALIGN), :]
        (_chunk_copy(loc, far, sem.at[slot]) if to_hbm
         else _chunk_copy(far, loc, sem.at[slot])).start(priority=priority)

    def group(g, carry):
        for u in range(START_GROUP):
            start(g * START_GROUP + u, u % 2)
        return carry

    def single(j, carry):
        start(j, 0)
        return carry

    n_groups = n // START_GROUP
    lax.fori_loop(0, n_groups, group, 0)
    lax.fori_loop(n_groups * START_GROUP, n, single, 0)


def _dispatch_kernel(nch_ref, nused_ref, tab_ref, tail_ref, lp_ref, h2_ref, xs_hbm,
                     buf, zbuf, sem, zsem, *, n_tiles, n_blocks):
    i = pl.program_id(0)
    slot = i % 2

    @pl.when(i >= 2)
    def _():
        _chunk_wait_n(buf, xs_hbm, sem, slot, nch_ref[jnp.maximum(i - 2, 0)], True)

    lp = lp_ref[...]
    tm = lp.shape[1]

    def sort_rows(row0, n_rows):
        rows = lax.broadcasted_iota(I32, (n_rows, tm), 0) + row0
        onehot = jnp.zeros((n_rows, tm), F32)
        for k in range(TOP_K):
            onehot = jnp.where(rows == lp[k:k + 1, :], 1.0, onehot)
        buf[slot, pl.ds(row0, n_rows), :] = jnp.dot(
            onehot.astype(BF16), h2_ref[...], preferred_element_type=F32).astype(BF16)

    sort_rows(0, LOCAL_MAIN)

    @pl.when(nch_ref[i] * SEG_ALIGN > LOCAL_MAIN)
    def _():
        sort_rows(LOCAL_MAIN, LOCAL_ROWS - LOCAL_MAIN)

    _chunk_start_n(buf, xs_hbm, tab_ref, sem, slot, nch_ref[i], True)

    @pl.when(i == n_tiles - 1)
    def _():
        zbuf[...] = jnp.zeros_like(zbuf)
        zrow = zbuf.at[pl.ds(0, SEG_ALIGN), :]

        def fill(j, carry):
            @pl.when(tail_ref[0, 0, j] >= 0)
            def _():
                row0 = pl.multiple_of(tail_ref[0, 0, j] * SEG_ALIGN, SEG_ALIGN)
                _chunk_copy(zrow, xs_hbm.at[pl.ds(row0, SEG_ALIGN), :], zsem).start()
            return carry

        def drain(j, carry):
            @pl.when(tail_ref[0, 0, j] >= 0)
            def _():
                _chunk_copy(zrow, xs_hbm.at[pl.ds(0, SEG_ALIGN), :], zsem).wait()
            return carry

        def fill_block(b, carry):
            row0 = pl.multiple_of(b * MOE_ROWS, MOE_ROWS)
            _chunk_copy(zbuf, xs_hbm.at[pl.ds(row0, MOE_ROWS), :], zsem).start()
            return carry

        def drain_block(b, carry):
            _chunk_copy(zbuf, xs_hbm.at[pl.ds(0, MOE_ROWS), :], zsem).wait()
            return carry

        lax.fori_loop(0, TAIL_TABLE, fill, 0)
        lax.fori_loop(0, TAIL_TABLE, drain, 0)
        lax.fori_loop(nused_ref[0], n_blocks, fill_block, 0)
        lax.fori_loop(nused_ref[0], n_blocks, drain_block, 0)
        _chunk_wait_n(buf, xs_hbm, sem, slot, nch_ref[i], True)
        if n_tiles > 1:
            _chunk_wait_n(buf, xs_hbm, sem, 1 - slot, nch_ref[jnp.maximum(i - 1, 0)], True)


def _dispatch(n_chunks, n_used, chunk_tab, tail_tab, lp, h2, n_blocks):
    k, t = lp.shape
    d = h2.shape[1]
    n_tiles = t // MOE_TILE
    grid_spec = pltpu.PrefetchScalarGridSpec(
        num_scalar_prefetch=2,
        grid=(n_tiles,),
        in_specs=[pl.BlockSpec((1, 1, CHUNK_TABLE), lambda i, nc, nu: (i, 0, 0),
                               memory_space=pltpu.SMEM),
                  pl.BlockSpec((1, 1, TAIL_TABLE), lambda i, nc, nu: (0, 0, 0),
                               memory_space=pltpu.SMEM),
                  pl.BlockSpec((k, MOE_TILE), lambda i, nc, nu: (0, i)),
                  pl.BlockSpec((MOE_TILE, d), lambda i, nc, nu: (i, 0))],
        out_specs=pl.BlockSpec(memory_space=pl.ANY),
        scratch_shapes=[pltpu.VMEM((2, LOCAL_ROWS, d), BF16), pltpu.VMEM((MOE_ROWS, d), BF16),
                        pltpu.SemaphoreType.DMA((2,)), pltpu.SemaphoreType.DMA],
    )
    return pl.pallas_call(
        functools.partial(_dispatch_kernel, n_tiles=n_tiles, n_blocks=n_blocks),
        grid_spec=grid_spec,
        out_shape=jax.ShapeDtypeStruct((n_blocks * MOE_ROWS, d), BF16),
        compiler_params=pltpu.CompilerParams(
            dimension_semantics=("arbitrary",), vmem_limit_bytes=VMEM_LIMIT),
        name="dispatch",
    )(n_chunks, n_used, chunk_tab, tail_tab, lp, h2)


def _expert_mlp_kernel(blk_e_ref, nused_ref, x_ref, wgu_ref, bgu_ref, wdn_ref, bdn_ref, y_ref,
                       wgu_bf, wdn_bf):
    i = pl.program_id(0)
    nused = nused_ref[0]
    new_expert = jnp.logical_or(i == 0, blk_e_ref[i] != blk_e_ref[jnp.maximum(i - 1, 0)])

    @pl.when(jnp.logical_and(new_expert, i < nused))
    def _():
        wgu_bf[...] = wgu_ref[...].astype(BF16)
        wdn_bf[...] = wdn_ref[...].astype(BF16)

    @pl.when(i < nused)
    def _():
        x = x_ref[...]
        gu = jnp.dot(x, wgu_bf[...], preferred_element_type=F32) + bgu_ref[...]
        d_ff = gu.shape[1] // 2
        gate = jnp.minimum(gu[:, :d_ff], SWIGLU_LIMIT)
        up = jnp.clip(gu[:, d_ff:], -SWIGLU_LIMIT, SWIGLU_LIMIT)
        act = gate * _sigmoid(SWIGLU_ALPHA * gate) * (up + 1.0)
        y = jnp.dot(act.astype(BF16), wdn_bf[...], preferred_element_type=F32) + bdn_ref[...]
        y_ref[...] = y.astype(BF16)

    @pl.when(i >= nused)
    def _():
        y_ref[...] = jnp.zeros_like(y_ref)


def _expert_mlp(blk_expert, n_used, xs, w_gu, b_gu, w_dn, b_dn):
    d = xs.shape[1]
    n_blocks = blk_expert.shape[0]
    n_rows = n_blocks * MOE_ROWS
    e, _, f2 = w_gu.shape
    grid_spec = pltpu.PrefetchScalarGridSpec(
        num_scalar_prefetch=2,
        grid=(n_blocks,),
        in_specs=[
            pl.BlockSpec((MOE_ROWS, d), lambda i, be, nu: (jnp.minimum(i, nu[0] - 1), 0)),
            pl.BlockSpec((None, d, f2), lambda i, be, nu: (be[i], 0, 0)),
            pl.BlockSpec((None, 1, f2), lambda i, be, nu: (be[i], 0, 0)),
            pl.BlockSpec((None, f2 // 2, d), lambda i, be, nu: (be[i], 0, 0)),
            pl.BlockSpec((None, 1, d), lambda i, be, nu: (be[i], 0, 0)),
        ],
        out_specs=pl.BlockSpec((MOE_ROWS, d), lambda i, be, nu: (i, 0)),
        scratch_shapes=[pltpu.VMEM((d, f2), BF16), pltpu.VMEM((f2 // 2, d), BF16)],
    )
    return pl.pallas_call(
        _expert_mlp_kernel,
        grid_spec=grid_spec,
        out_shape=jax.ShapeDtypeStruct((n_rows, d), BF16),
        compiler_params=pltpu.CompilerParams(
            dimension_semantics=("arbitrary",), vmem_limit_bytes=VMEM_LIMIT),
        name="experts",
    )(blk_expert, n_used, xs, w_gu, b_gu.reshape(e, 1, f2), w_dn, b_dn.reshape(e, 1, d))


def _moe_combine_kernel(nch_ref, tab_cur_ref, tab_nxt_ref, y_hbm, lpt_ref, gate_ref, x1_ref,
                        p_ref, gp_ref, wg_ref, wp_ref, gfin_ref, o_ref, buf, sem, *, n_tiles):
    i = pl.program_id(0)
    slot = i % 2

    @pl.when(i == 0)
    def _():
        buf[...] = jnp.zeros_like(buf)
        _chunk_start_n(buf, y_hbm, tab_cur_ref, sem, 0, nch_ref[0], False)

    @pl.when(i + 1 < n_tiles)
    def _():
        _chunk_start_n(buf, y_hbm, tab_nxt_ref, sem, 1 - slot,
                       nch_ref[jnp.minimum(i + 1, n_tiles - 1)], False)

    _chunk_wait_n(buf, y_hbm, sem, slot, nch_ref[i], False)
    lpt = lpt_ref[...]
    gates = gate_ref[...]
    tm = lpt.shape[0]
    cols = lax.broadcasted_iota(I32, (tm, LOCAL_ROWS), 1)
    weights = jnp.zeros((tm, LOCAL_ROWS), F32)
    for k in range(TOP_K):
        weights = jnp.where(cols == lpt[:, k:k + 1], gates[:, k:k + 1], weights)
    weights = weights.astype(BF16)
    x2 = x1_ref[...] + jnp.dot(weights[:, :LOCAL_MAIN], buf[slot, :LOCAL_MAIN, :],
                               preferred_element_type=F32)
    x2 = lax.cond(
        nch_ref[i] * SEG_ALIGN > LOCAL_MAIN,
        lambda v: v + jnp.dot(weights[:, LOCAL_MAIN:], buf[slot, LOCAL_MAIN:, :],
                              preferred_element_type=F32),
        lambda v: v, x2)
    hg = _rms(x2, gp_ref[...]).astype(BF16)
    gate = _sigmoid(jnp.dot(hg, wg_ref[...], preferred_element_type=F32))
    emb = jnp.dot(p_ref[...].astype(BF16), wp_ref[...], preferred_element_type=F32)
    x3 = x2 + gate * emb
    o_ref[...] = _rms(x3, gfin_ref[...])


def _moe_combine(n_chunks, chunk_tab, y, lpt, gates_tk, x1, p2, ple_g, w_gate, w_proj, fin_g):
    t, d = x1.shape
    n_tiles = t // MOE_TILE
    last = n_tiles - 1
    ple = p2.shape[1]
    row = lambda i, nc: (i, 0)
    const = lambda i, nc: (0, 0)
    grid_spec = pltpu.PrefetchScalarGridSpec(
        num_scalar_prefetch=1,
        grid=(n_tiles,),
        in_specs=[pl.BlockSpec((1, 1, CHUNK_TABLE), lambda i, nc: (i, 0, 0),
                               memory_space=pltpu.SMEM),
                  pl.BlockSpec((1, 1, CHUNK_TABLE),
                               lambda i, nc: (jnp.minimum(i + 1, last), 0, 0),
                               memory_space=pltpu.SMEM),
                  pl.BlockSpec(memory_space=pl.ANY),
                  pl.BlockSpec((MOE_TILE, TOP_K), row),
                  pl.BlockSpec((MOE_TILE, TOP_K), row),
                  pl.BlockSpec((MOE_TILE, d), row),
                  pl.BlockSpec((MOE_TILE, ple), row),
                  pl.BlockSpec((1, d), const),
                  pl.BlockSpec((d, d), const),
                  pl.BlockSpec((ple, d), const),
                  pl.BlockSpec((1, d), const)],
        out_specs=pl.BlockSpec((MOE_TILE, d), row),
        scratch_shapes=[pltpu.VMEM((2, LOCAL_ROWS, d), BF16), pltpu.SemaphoreType.DMA((2,))],
    )
    return pl.pallas_call(
        functools.partial(_moe_combine_kernel, n_tiles=n_tiles),
        grid_spec=grid_spec,
        out_shape=jax.ShapeDtypeStruct((t, d), F32),
        compiler_params=pltpu.CompilerParams(
            dimension_semantics=("arbitrary",), vmem_limit_bytes=VMEM_LIMIT),
        name="combine",
    )(n_chunks, chunk_tab, chunk_tab, y, lpt, gates_tk, x1, p2, ple_g.reshape(1, d),
      w_gate, w_proj, fin_g.reshape(1, d))


def _moe_tables(topi, lrank, cnt):
    k, t = topi.shape
    n_tiles = t // MOE_TILE
    cnt = cnt.astype(I32)
    pc = (cnt + SEG_ALIGN - 1) // SEG_ALIGN * SEG_ALIGN
    l_end = jnp.cumsum(pc, axis=1)
    l_start = l_end - pc
    tot = jnp.sum(pc, axis=0)
    e_pad = (tot + MOE_ROWS - 1) // MOE_ROWS * MOE_ROWS
    e_end = jnp.cumsum(e_pad)
    seg_row0 = (e_end - e_pad)[None, :] + jnp.cumsum(pc, axis=0) - pc
    eids = jnp.arange(N_EXPERTS, dtype=I32)[:, None, None, None]
    topi3 = topi.reshape(k, n_tiles, MOE_TILE)
    base = jnp.sum(jnp.where(topi3[None] == eids, l_start.T[:, None, :, None], 0), axis=0)
    lp = (lrank.reshape(k, n_tiles, MOE_TILE) + base).reshape(k, t)
    j16 = jnp.arange(CHUNK_TABLE, dtype=I32)[None, :] * SEG_ALIGN
    ce = jnp.minimum(jnp.sum((l_end[:, :, None] <= j16[:, None, :]).astype(I32), axis=1),
                     N_EXPERTS - 1)
    pick = ce[:, None, :] == jnp.arange(N_EXPERTS, dtype=I32)[None, :, None]
    seg0 = jnp.sum(jnp.where(pick, seg_row0[:, :, None], 0), axis=1)
    loc0 = jnp.sum(jnp.where(pick, l_start[:, :, None], 0), axis=1)
    n_chunks = l_end[:, -1] // SEG_ALIGN
    valid = jnp.arange(CHUNK_TABLE, dtype=I32)[None, :] < n_chunks[:, None]
    tab = jnp.where(valid, (seg0 + j16 - loc0) // SEG_ALIGN, 0).astype(I32)
    n_rows_max = k * t + n_tiles * N_EXPERTS * (SEG_ALIGN - 1) + N_EXPERTS * (MOE_ROWS - 1)
    n_blocks = (n_rows_max + MOE_ROWS - 1) // MOE_ROWS
    n_used = (e_end[-1] // MOE_ROWS).astype(I32).reshape(1)
    blk_row0 = jnp.arange(n_blocks, dtype=I32) * MOE_ROWS
    blk_expert = jnp.minimum(
        jnp.sum((e_end[None, :] <= blk_row0[:, None]).astype(I32), axis=1), N_EXPERTS - 1)
    per_e = MOE_ROWS // SEG_ALIGN
    tj = jnp.arange(per_e, dtype=I32)[None, :]
    tail = jnp.where(tj < ((e_pad - tot) // SEG_ALIGN)[:, None],
                     ((e_end - e_pad + tot) // SEG_ALIGN)[:, None] + tj, -1).astype(I32)
    return (lp, n_chunks.astype(I32), tab.reshape(n_tiles, 1, CHUNK_TABLE),
            tail.reshape(1, 1, TAIL_TABLE), blk_expert, n_used)


def _layer(x2, p2, seq, norm_mix_g, w_in, b_f, lam_re, lam_im, log_dt, b_re, b_im, c_re, c_im,
           d_skip, w_glu, b_glu, attn_out_g, ssm_out_g, w_out, norm_ffn_g, w_router, b_router,
           w_gu, b_gu, w_dn, b_dn, norm_ple_g, w_ple_gate, w_ple_proj, final_g):
    t, d = x2.shape
    nb = t // seq
    aw = ATTN_WIDTH
    w_main = jnp.concatenate([w_in[:, :3 * aw], w_in[:, 3 * aw + N_HEADS:]], axis=1).astype(BF16)
    wft = w_in[:, 3 * aw:3 * aw + N_HEADS].T.astype(BF16)

    a_re, a_im, bbr, bbi = _ssm_prep(lam_re, lam_im, log_dt, b_re, b_im)
    bre = _block_diag(bbr).astype(BF16)
    bim = _block_diag(bbi).astype(BF16)
    cre = _block_diag(jnp.transpose(c_re, (0, 2, 1))).astype(BF16)
    cim = _block_diag(jnp.transpose(c_im, (0, 2, 1))).astype(BF16)

    q, k, v, u, c_t = _in_proj(x2, norm_mix_g, w_main, wft, b_f, seq=seq)
    shp = (nb, seq, aw)
    attn = _attention(q.reshape(shp), k.reshape(shp), v.reshape(shp), c_t)
    ssm = _ssm(u.reshape(nb, seq, SSM_WIDTH), bre, bim, cre, cim, a_re, a_im,
               d_skip, w_glu.astype(BF16), b_glu, ssm_out_g)

    x1, h2, topi, gates, lrank, cnt = _out_proj(
        x2, attn.reshape(t, aw), ssm.reshape(t, SSM_WIDTH), attn_out_g,
        w_out[:aw].astype(BF16), w_out[aw:].astype(BF16), norm_ffn_g,
        w_router.T.astype(BF16), b_router, tm=MOE_TILE)

    lp, n_chunks, chunk_tab, tail_tab, blk_expert, n_used = _moe_tables(topi, lrank, cnt[:, :, 0])
    xs = _dispatch(n_chunks, n_used, chunk_tab, tail_tab, lp, h2, blk_expert.shape[0])
    y = _expert_mlp(blk_expert, n_used, xs, w_gu, b_gu, w_dn, b_dn)
    return _moe_combine(n_chunks, chunk_tab, y, lp.T, gates.T, x1, p2, norm_ple_g,
                        w_ple_gate.astype(BF16), w_ple_proj.astype(BF16), final_g)


def kernel(x, p, norm_mix_g, w_in, b_f, lam_re, lam_im, log_dt, b_re, b_im, c_re, c_im, d_skip, w_glu, b_glu, attn_out_g, ssm_out_g, w_out, norm_ffn_g, w_router, b_router, w_gu, b_gu, w_dn, b_dn, norm_ple_g, w_ple_gate, w_ple_proj, norm_final_g):
    bsz, seq, d = x.shape
    depth = w_in.shape[0]
    assert depth == 1, "one layer: the final rmsnorm is fused into the layer's last kernel"
    out = _layer(x.reshape(bsz * seq, d), p[0].reshape(bsz * seq, -1), seq,
                 norm_mix_g[0], w_in[0], b_f[0], lam_re[0], lam_im[0], log_dt[0],
                 b_re[0], b_im[0], c_re[0], c_im[0], d_skip[0], w_glu[0], b_glu[0],
                 attn_out_g[0], ssm_out_g[0], w_out[0], norm_ffn_g[0], w_router[0],
                 b_router[0], w_gu[0], b_gu[0], w_dn[0], b_dn[0], norm_ple_g[0],
                 w_ple_gate[0], w_ple_proj[0], norm_final_g)
    return out.reshape(bsz, seq, d)
```

```python
import functools
import math

import jax
import jax.numpy as jnp
from jax import lax
from jax.experimental import pallas as pl
from jax.experimental.pallas import tpu as pltpu

F32 = jnp.float32
BF16 = jnp.bfloat16
I32 = jnp.int32

NORM_EPS = 1e-5
HEAD_DIM = 64
N_HEADS = 8
ATTN_WIDTH = 512
SSM_WIDTH = 512
SSM_GROUP = 16
N_SSM_GROUPS = 32
SSM_STATE = 64
N_STATE = N_SSM_GROUPS * SSM_STATE
N_EXPERTS = 32
TOP_K = 4
SWIGLU_LIMIT = 7.0
SWIGLU_ALPHA = 1.702
LANES = 128
MOE_ROWS = 1024
NEG_BIG = -1e30
LOG2E = math.log2(math.e)
VMEM_LIMIT = 56 * 1024 * 1024

_NT = (((1,), (1,)), ((), ()))


def _rms(xf, g):
    ms = jnp.mean(xf * xf, axis=-1, keepdims=True)
    return xf * lax.rsqrt(ms + NORM_EPS) * g


def _sigmoid(x):
    return 1.0 / (1.0 + jnp.exp(-x))


def _ssm_prep_kernel(lr_ref, li_ref, ldt_ref, brt_ref, bit_ref,
                     ar_ref, ai_ref, bbr_ref, bbi_ref):
    lr = lr_ref[...]
    li = li_ref[...]
    dt = jnp.exp(ldt_ref[...])
    mag = jnp.exp(lr * dt)
    ab_re = mag * jnp.cos(li * dt)
    ab_im = mag * jnp.sin(li * dt)
    den = lr * lr + li * li
    nr = ab_re - 1.0
    z_re = (nr * lr + ab_im * li) / den
    z_im = (ab_im * lr - nr * li) / den
    ar_ref[...] = ab_re
    ai_ref[...] = ab_im
    br = brt_ref[...]
    bi = bit_ref[...]
    bbr_ref[...] = z_re * br - z_im * bi
    bbi_ref[...] = z_re * bi + z_im * br


def _ssm_prep(lam_re, lam_im, log_dt, b_re, b_im):
    g, p, c = b_re.shape
    brt = jnp.transpose(b_re, (0, 2, 1))
    bit = jnp.transpose(b_im, (0, 2, 1))
    return pl.pallas_call(
        _ssm_prep_kernel,
        out_shape=(jax.ShapeDtypeStruct((g, 1, p), F32), jax.ShapeDtypeStruct((g, 1, p), F32),
                   jax.ShapeDtypeStruct((g, c, p), F32), jax.ShapeDtypeStruct((g, c, p), F32)),
        name="ssm_prep",
    )(lam_re.reshape(g, 1, p), lam_im.reshape(g, 1, p), log_dt.reshape(g, 1, 1), brt, bit)


def _block_diag(w):
    g, a, b = w.shape
    half = g // 2
    eye = jnp.eye(half, dtype=w.dtype)
    w4 = w.reshape(2, half, a, b)
    out = jnp.einsum('hgab,gk->hgakb', w4, eye)
    return out.reshape(2, half * a, half * b)


def _inproj_kernel(x_ref, g_ref, w_ref, wft_ref, bf_ref, tri_ref,
                   q_ref, k_ref, v_ref, u_ref, c_ref, carry_ref, *, tiles_per_seq, tm):
    i = pl.program_id(0)

    @pl.when(i % tiles_per_seq == 0)
    def _():
        carry_ref[...] = jnp.zeros_like(carry_ref)

    h = _rms(x_ref[...], g_ref[...]).astype(BF16)
    proj = jnp.dot(h, w_ref[...], preferred_element_type=F32)
    aw = ATTN_WIDTH
    q_ref[...] = (proj[:, 0:aw] * (LOG2E * HEAD_DIM ** -0.5)).astype(BF16)
    k_ref[...] = proj[:, aw:2 * aw].astype(BF16)
    v_ref[...] = proj[:, 2 * aw:3 * aw].astype(BF16)
    u_ref[...] = proj[:, 3 * aw:3 * aw + SSM_WIDTH].astype(BF16)

    fl = lax.dot_general(wft_ref[...], h, _NT, preferred_element_type=F32)
    z = fl + bf_ref[...]
    lf = jnp.minimum(z, 0.0) - jnp.log1p(jnp.exp(-jnp.abs(z)))
    hi = lf.astype(BF16)
    lo = (lf - hi.astype(F32)).astype(BF16)
    tri = tri_ref[...]
    cs = (jnp.dot(hi, tri, preferred_element_type=F32)
          + jnp.dot(lo, tri, preferred_element_type=F32))
    c = cs + carry_ref[:, 0:1]
    c_ref[...] = c * LOG2E
    carry_ref[...] = jnp.broadcast_to(c[:, tm - 1:tm], carry_ref.shape)


def _in_proj(x2, norm_g, w_main, wft, b_f, *, seq, tm=1024):
    t, d = x2.shape
    n_main = w_main.shape[1]
    tri = jnp.triu(jnp.ones((tm, tm), F32)).astype(BF16)
    kern = functools.partial(_inproj_kernel, tiles_per_seq=seq // tm, tm=tm)
    row = lambda i: (i, 0)
    const = lambda i: (0, 0)
    act = jax.ShapeDtypeStruct((t, ATTN_WIDTH), BF16)
    return pl.pallas_call(
        kern,
        grid=(t // tm,),
        in_specs=[pl.BlockSpec((tm, d), row),
                  pl.BlockSpec((1, d), const),
                  pl.BlockSpec((d, n_main), const),
                  pl.BlockSpec((N_HEADS, d), const),
                  pl.BlockSpec((N_HEADS, 1), const),
                  pl.BlockSpec((tm, tm), const)],
        out_specs=[pl.BlockSpec((tm, ATTN_WIDTH), row)] * 4
        + [pl.BlockSpec((N_HEADS, tm), lambda i: (0, i))],
        out_shape=[act, act, act, act, jax.ShapeDtypeStruct((N_HEADS, t), F32)],
        scratch_shapes=[pltpu.VMEM((N_HEADS, LANES), F32)],
        compiler_params=pltpu.CompilerParams(
            dimension_semantics=("arbitrary",), vmem_limit_bytes=VMEM_LIMIT),
        name="in_proj",
    )(x2, norm_g.reshape(1, d), w_main, wft, b_f.reshape(N_HEADS, 1), tri)


def _attn_kernel(q_ref, k_ref, v_ref, c_ref, o_ref, *, tq, n_pairs):
    i = pl.program_id(2)
    lane = lax.broadcasted_iota(I32, (1, LANES), 1)
    first = lane < HEAD_DIM
    n_heads = 2 * n_pairs
    q_heads = []
    for p in range(n_pairs):
        q2 = q_ref[:, p * LANES:(p + 1) * LANES]
        zero = jnp.zeros_like(q2)
        q_heads += [jnp.where(first, q2, zero), jnp.where(first, zero, q2)]
    half = tq // 2

    def block(off, width, r0, carry, masked):
        cj = c_ref[:, pl.ds(off, width)]
        out = []
        for p in range(n_pairs):
            kj = k_ref[pl.ds(off, width), p * LANES:(p + 1) * LANES]
            vj = v_ref[pl.ds(off, width), p * LANES:(p + 1) * LANES]
            one = jnp.ones_like(vj)
            v_heads = (jnp.where(first, vj, one), jnp.where(first, one, vj))
            for h in range(2):
                m, acc = carry[2 * p + h]
                s = (lax.dot_general(q_heads[2 * p + h][r0:], kj, _NT,
                                     preferred_element_type=F32)
                     - cj[2 * p + h:2 * p + h + 1, :])
                if masked:
                    rr = lax.broadcasted_iota(I32, s.shape, 0)
                    cc = lax.broadcasted_iota(I32, s.shape, 1)
                    s = jnp.where(cc <= rr, s, NEG_BIG)
                m_new = jnp.maximum(m[r0:], jnp.max(s, axis=-1, keepdims=True))
                alpha = jnp.exp2(m[r0:] - m_new)
                pr = jnp.exp2(s - m_new).astype(BF16)
                acc_new = alpha * acc[r0:] + jnp.dot(pr, v_heads[h], preferred_element_type=F32)
                if r0:
                    m_new = jnp.concatenate([m[:r0], m_new], axis=0)
                    acc_new = jnp.concatenate([acc[:r0], acc_new], axis=0)
                out.append((m_new, acc_new))
        return tuple(out)

    init_one = (jnp.full((tq, 1), NEG_BIG, F32), jnp.zeros((tq, LANES), F32))
    carry = lax.fori_loop(
        0, i, lambda j, c: block(pl.multiple_of(j * tq, tq), tq, 0, c, False),
        (init_one,) * n_heads)
    diag = pl.multiple_of(i * tq, tq)
    for n in range(tq // half):
        carry = block(pl.multiple_of(diag + n * half, half), half, n * half, carry, True)
    outs = []
    for p in range(n_pairs):
        acc_a = carry[2 * p][1]
        acc_b = carry[2 * p + 1][1]
        outs.append(jnp.where(first, acc_a / pltpu.roll(acc_a, HEAD_DIM, axis=1),
                              acc_b / pltpu.roll(acc_b, HEAD_DIM, axis=1)))
    o_ref[...] = jnp.concatenate(outs, axis=-1).astype(BF16)


def _attention(q, k, v, c_t, *, tq=1024, n_pairs=2):
    b, s, w = q.shape
    wb = n_pairs * LANES
    n_groups = w // wb
    c3 = c_t.reshape(n_groups, 2 * n_pairs, b * s)
    kern = functools.partial(_attn_kernel, tq=tq, n_pairs=n_pairs)
    return pl.pallas_call(
        kern,
        grid=(b, n_groups, s // tq),
        in_specs=[pl.BlockSpec((None, tq, wb), lambda bi, g, i: (bi, i, g)),
                  pl.BlockSpec((None, s, wb), lambda bi, g, i: (bi, 0, g)),
                  pl.BlockSpec((None, s, wb), lambda bi, g, i: (bi, 0, g)),
                  pl.BlockSpec((None, 2 * n_pairs, s), lambda bi, g, i: (g, 0, bi))],
        out_specs=pl.BlockSpec((None, tq, wb), lambda bi, g, i: (bi, i, g)),
        out_shape=jax.ShapeDtypeStruct((b, s, w), BF16),
        compiler_params=pltpu.CompilerParams(
            dimension_semantics=("arbitrary", "arbitrary", "arbitrary"),
            vmem_limit_bytes=VMEM_LIMIT),
        name="attention",
    )(q, k, v, c3)


def _ssm_kernel(u_ref, bre_ref, bim_ref, cre_ref, cim_ref, ar_ref, ai_ref, dsk_ref,
                wglu_ref, bglu_ref, g_ref, o_ref,
                us_ref, xr_ref, xi_ref, str_ref, sti_ref, res_ref, *, tt, nb):
    i = pl.program_id(0)

    @pl.when(i == 0)
    def _():
        str_ref[...] = jnp.zeros_like(str_ref)
        sti_ref[...] = jnp.zeros_like(sti_ref)

    n_chunks = SSM_WIDTH // LANES
    for b in range(nb):
        ub32 = u_ref[b].astype(F32)
        for c in range(n_chunks):
            us_ref[c, pl.ds(b, tt, stride=nb), :] = ub32[:, c * LANES:(c + 1) * LANES]
    uf = jnp.concatenate([us_ref[c] for c in range(n_chunks)], axis=-1)
    ub = uf.astype(BF16)
    half_in = SSM_WIDTH // 2
    half_st = N_STATE // 2
    for hf in range(2):
        uh = ub[:, hf * half_in:(hf + 1) * half_in]
        xr_ref[:, hf * half_st:(hf + 1) * half_st] = jnp.dot(
            uh, bre_ref[hf], preferred_element_type=F32)
        xi_ref[:, hf * half_st:(hf + 1) * half_st] = jnp.dot(
            uh, bim_ref[hf], preferred_element_type=F32)

    n_col_groups = 2
    wcol = N_STATE // n_col_groups
    unroll = 4
    for cg in range(n_col_groups):
        cols = slice(cg * wcol, (cg + 1) * wcol)
        ar = jnp.broadcast_to(ar_ref[:, cols], (nb, wcol))
        ai = jnp.broadcast_to(ai_ref[:, cols], (nb, wcol))

        def steps(tb, carry, cols=cols, ar=ar, ai=ai):
            sr, si = carry
            for k in range(unroll):
                r0 = pl.multiple_of((tb * unroll + k) * nb, nb)
                br = xr_ref[pl.ds(r0, nb), cols]
                bi = xi_ref[pl.ds(r0, nb), cols]
                nr = ar * sr - ai * si + br
                ni = ar * si + ai * sr + bi
                xr_ref[pl.ds(r0, nb), cols] = nr
                xi_ref[pl.ds(r0, nb), cols] = ni
                sr, si = nr, ni
            return sr, si

        sr, si = lax.fori_loop(0, tt // unroll, steps, (str_ref[:, cols], sti_ref[:, cols]))
        str_ref[:, cols] = sr
        sti_ref[:, cols] = si

    ys = []
    for hf in range(2):
        xr = xr_ref[:, hf * half_st:(hf + 1) * half_st].astype(BF16)
        xi = xi_ref[:, hf * half_st:(hf + 1) * half_st].astype(BF16)
        ys.append(jnp.dot(xr, cre_ref[hf], preferred_element_type=F32)
                  - jnp.dot(xi, cim_ref[hf], preferred_element_type=F32))
    y = jnp.concatenate(ys, axis=-1) + dsk_ref[...] * uf
    gl = 0.5 * y * (1.0 + jnp.tanh(math.sqrt(2.0 / math.pi) * (y + 0.044715 * (y * y * y))))
    zz = jnp.dot(gl.astype(BF16), wglu_ref[...], preferred_element_type=F32) + bglu_ref[...]
    out = gl * _sigmoid(zz)
    res = _rms(out, g_ref[...])
    for c in range(n_chunks):
        res_ref[c] = res[:, c * LANES:(c + 1) * LANES]
    for b in range(nb):
        o_ref[b] = jnp.concatenate(
            [res_ref[c, pl.ds(b, tt, stride=nb), :] for c in range(n_chunks)],
            axis=-1).astype(BF16)


def _ssm(u3, bre, bim, cre, cim, a_re, a_im, d_skip, w_glu, b_glu, out_g, *, tt=128):
    nb, s, w = u3.shape
    rows = tt * nb
    kern = functools.partial(_ssm_kernel, tt=tt, nb=nb)
    c3 = lambda i: (0, 0, 0)
    c2 = lambda i: (0, 0)
    return pl.pallas_call(
        kern,
        grid=(s // tt,),
        in_specs=[pl.BlockSpec((nb, tt, w), lambda i: (0, i, 0)),
                  pl.BlockSpec(bre.shape, c3), pl.BlockSpec(bim.shape, c3),
                  pl.BlockSpec(cre.shape, c3), pl.BlockSpec(cim.shape, c3),
                  pl.BlockSpec((1, N_STATE), c2), pl.BlockSpec((1, N_STATE), c2),
                  pl.BlockSpec((1, w), c2),
                  pl.BlockSpec((w, w), c2), pl.BlockSpec((1, w), c2), pl.BlockSpec((1, w), c2)],
        out_specs=pl.BlockSpec((nb, tt, w), lambda i: (0, i, 0)),
        out_shape=jax.ShapeDtypeStruct((nb, s, w), BF16),
        scratch_shapes=[pltpu.VMEM((w // LANES, rows, LANES), F32),
                        pltpu.VMEM((rows, N_STATE), F32), pltpu.VMEM((rows, N_STATE), F32),
                        pltpu.VMEM((nb, N_STATE), F32), pltpu.VMEM((nb, N_STATE), F32),
                        pltpu.VMEM((w // LANES, rows, LANES), F32)],
        compiler_params=pltpu.CompilerParams(
            dimension_semantics=("arbitrary",), vmem_limit_bytes=VMEM_LIMIT),
        name="ssm",
    )(u3, bre, bim, cre, cim, a_re.reshape(1, N_STATE), a_im.reshape(1, N_STATE),
      d_skip.reshape(1, w), w_glu, b_glu.reshape(1, w), out_g.reshape(1, w))


def _outproj_kernel(x_ref, a_ref, s_ref, ga_ref, woa_ref, wos_ref, gf_ref, wrt_ref, br_ref,
                    tri_ref, x1_ref, h2_ref, topi_ref, gate_ref, rank_ref, cnt_ref, *, tm):
    a = _rms(a_ref[...].astype(F32), ga_ref[...]).astype(BF16)
    x1 = (x_ref[...] + jnp.dot(a, woa_ref[...], preferred_element_type=F32)
          + jnp.dot(s_ref[...], wos_ref[...], preferred_element_type=F32))
    x1_ref[...] = x1
    h2 = _rms(x1, gf_ref[...]).astype(BF16)
    h2_ref[...] = h2

    lg = lax.dot_general(wrt_ref[...], h2, _NT,
                         preferred_element_type=F32) + br_ref[...]
    ids = lax.broadcasted_iota(I32, (N_EXPERTS, tm), 0)
    work = lg
    vals, idxs = [], []
    for _ in range(TOP_K):
        m = jnp.max(work, axis=0, keepdims=True)
        idx = jnp.min(jnp.where(work == m, ids, N_EXPERTS), axis=0, keepdims=True)
        vals.append(m)
        idxs.append(idx)
        work = jnp.where(ids == idx, -jnp.inf, work)
    exps = [jnp.exp(v - vals[0]) for v in vals]
    den = exps[0] + exps[1] + exps[2] + exps[3]
    gate_ref[...] = jnp.concatenate([e / den for e in exps], axis=0)
    topi_ref[...] = jnp.concatenate(idxs, axis=0)

    sel = jnp.zeros((N_EXPERTS, tm), F32)
    for idx in idxs:
        sel = sel + jnp.where(ids == idx, 1.0, 0.0)
    before = jnp.dot(sel.astype(BF16), tri_ref[...], preferred_element_type=F32)
    ranks = [jnp.sum(jnp.where(ids == idx, before, 0.0), axis=0, keepdims=True) for idx in idxs]
    rank_ref[...] = jnp.concatenate(ranks, axis=0).astype(I32)
    cnt_ref[0] = jnp.broadcast_to(jnp.sum(sel, axis=1, keepdims=True), (N_EXPERTS, LANES))


def _out_proj(x2, attn, ssm, attn_g, wo_a, wo_s, ffn_g, wrt, b_router, *, tm=512):
    t, d = x2.shape
    tri = jnp.triu(jnp.ones((tm, tm), F32), k=1).astype(BF16)
    kern = functools.partial(_outproj_kernel, tm=tm)
    row = lambda i: (i, 0)
    const = lambda i: (0, 0)
    colblk = lambda i: (0, i)
    return pl.pallas_call(
        kern,
        grid=(t // tm,),
        in_specs=[pl.BlockSpec((tm, d), row),
                  pl.BlockSpec((tm, ATTN_WIDTH), row),
                  pl.BlockSpec((tm, SSM_WIDTH), row),
                  pl.BlockSpec((1, ATTN_WIDTH), const),
                  pl.BlockSpec((ATTN_WIDTH, d), const),
                  pl.BlockSpec((SSM_WIDTH, d), const),
                  pl.BlockSpec((1, d), const),
                  pl.BlockSpec((N_EXPERTS, d), const),
                  pl.BlockSpec((N_EXPERTS, 1), const),
                  pl.BlockSpec((tm, tm), const)],
        out_specs=[pl.BlockSpec((tm, d), row),
                   pl.BlockSpec((tm, d), row),
                   pl.BlockSpec((TOP_K, tm), colblk),
                   pl.BlockSpec((TOP_K, tm), colblk),
                   pl.BlockSpec((TOP_K, tm), colblk),
                   pl.BlockSpec((1, N_EXPERTS, LANES), lambda i: (i, 0, 0))],
        out_shape=[jax.ShapeDtypeStruct((t, d), F32),
                   jax.ShapeDtypeStruct((t, d), BF16),
                   jax.ShapeDtypeStruct((TOP_K, t), I32),
                   jax.ShapeDtypeStruct((TOP_K, t), F32),
                   jax.ShapeDtypeStruct((TOP_K, t), I32),
                   jax.ShapeDtypeStruct((t // tm, N_EXPERTS, LANES), F32)],
        compiler_params=pltpu.CompilerParams(
            dimension_semantics=("arbitrary",), vmem_limit_bytes=VMEM_LIMIT),
        name="out_proj",
    )(x2, attn, ssm, attn_g.reshape(1, -1), wo_a, wo_s, ffn_g.reshape(1, d), wrt,
      b_router.reshape(N_EXPERTS, 1), tri)


SEG_ALIGN = 16
MOE_TILE = 512
LOCAL_ROWS = TOP_K * MOE_TILE + N_EXPERTS * SEG_ALIGN
CHUNK_TABLE = 256
TAIL_TABLE = N_EXPERTS * (MOE_ROWS // SEG_ALIGN)
START_GROUP = 4
WAIT_GROUP = 8


def _chunk_copy(src, dst, sem):
    return pltpu.make_async_copy(src, dst, sem)


def _chunk_wait_n(local_buf, hbm, sem, slot, n, to_hbm):
    def wait_rows(rows):
        loc = local_buf.at[slot, pl.ds(0, rows), :]
        far = hbm.at[pl.ds(0, rows), :]

        def body(j, carry):
            (_chunk_copy(loc, far, sem.at[slot]) if to_hbm
             else _chunk_copy(far, loc, sem.at[slot])).wait()
            return carry
        return body

    n_groups = n // WAIT_GROUP
    lax.fori_loop(0, n_groups, wait_rows(WAIT_GROUP * SEG_ALIGN), 0)
    lax.fori_loop(n_groups * WAIT_GROUP, n, wait_rows(SEG_ALIGN), 0)


def _chunk_start_n(local_buf, hbm, tab_ref, sem, slot, n, to_hbm):
    def start(j, priority):
        loc = local_buf.at[slot, pl.ds(pl.multiple_of(j * SEG_ALIGN, SEG_ALIGN), SEG_ALIGN), :]
        far = hbm.at[pl.ds(pl.multiple_of(tab_ref[0, 0, j] * SEG_ALIGN, SEG_ALIGN), SEG_ALIGN), :]
        (_chunk_copy(loc, far, sem.at[slot]) if to_hbm
         else _chunk_copy(far, loc, sem.at[slot])).start(priority=priority)

    def group(g, carry):
        for u in range(START_GROUP):
            start(g * START_GROUP + u, u % 2)
        return carry

    def single(j, carry):
        start(j, 0)
        return carry

    n_groups = n // START_GROUP
    lax.fori_loop(0, n_groups, group, 0)
    lax.fori_loop(n_groups * START_GROUP, n, single, 0)


def _dispatch_kernel(nch_ref, nused_ref, ntail_ref, tab_ref, tail_ref, lp_ref, h2_ref, xs_hbm,
                     buf, zbuf, sem, zsem, *, n_tiles, n_blocks):
    i = pl.program_id(0)
    slot = i % 2

    @pl.when(i >= 2)
    def _():
        _chunk_wait_n(buf, xs_hbm, sem, slot, nch_ref[jnp.maximum(i - 2, 0)], True)

    lp = lp_ref[...]
    tm = lp.shape[1]

    rows = lax.broadcasted_iota(I32, (LOCAL_ROWS, tm), 0)
    onehot = jnp.zeros((LOCAL_ROWS, tm), F32)
    for k in range(TOP_K):
        onehot = jnp.where(rows == lp[k:k + 1, :], 1.0, onehot)
    buf[slot] = jnp.dot(onehot.astype(BF16), h2_ref[...],
                        preferred_element_type=F32).astype(BF16)
    _chunk_start_n(buf, xs_hbm, tab_ref, sem, slot, nch_ref[i], True)

    @pl.when(i == n_tiles - 1)
    def _():
        zbuf[...] = jnp.zeros_like(zbuf)
        zrow = zbuf.at[pl.ds(0, SEG_ALIGN), :]

        def fill(j, carry):
            row0 = pl.multiple_of(tail_ref[0, 0, j] * SEG_ALIGN, SEG_ALIGN)
            _chunk_copy(zrow, xs_hbm.at[pl.ds(row0, SEG_ALIGN), :], zsem).start()
            return carry

        def drain(j, carry):
            _chunk_copy(zrow, xs_hbm.at[pl.ds(0, SEG_ALIGN), :], zsem).wait()
            return carry

        def fill_block(b, carry):
            row0 = pl.multiple_of(b * MOE_ROWS, MOE_ROWS)
            _chunk_copy(zbuf, xs_hbm.at[pl.ds(row0, MOE_ROWS), :], zsem).start()
            return carry

        def drain_block(b, carry):
            _chunk_copy(zbuf, xs_hbm.at[pl.ds(0, MOE_ROWS), :], zsem).wait()
            return carry

        lax.fori_loop(0, ntail_ref[0], fill, 0)
        lax.fori_loop(0, ntail_ref[0], drain, 0)
        lax.fori_loop(nused_ref[0], n_blocks, fill_block, 0)
        lax.fori_loop(nused_ref[0], n_blocks, drain_block, 0)
        _chunk_wait_n(buf, xs_hbm, sem, slot, nch_ref[i], True)
        if n_tiles > 1:
            _chunk_wait_n(buf, xs_hbm, sem, 1 - slot, nch_ref[jnp.maximum(i - 1, 0)], True)


def _dispatch(n_chunks, n_used, n_tail, chunk_tab, tail_tab, lp, h2, n_blocks):
    k, t = lp.shape
    d = h2.shape[1]
    n_tiles = t // MOE_TILE
    grid_spec = pltpu.PrefetchScalarGridSpec(
        num_scalar_prefetch=3,
        grid=(n_tiles,),
        in_specs=[pl.BlockSpec((1, 1, CHUNK_TABLE), lambda i, nc, nu, nt: (i, 0, 0),
                               memory_space=pltpu.SMEM),
                  pl.BlockSpec((1, 1, TAIL_TABLE), lambda i, nc, nu, nt: (0, 0, 0),
                               memory_space=pltpu.SMEM),
                  pl.BlockSpec((k, MOE_TILE), lambda i, nc, nu, nt: (0, i)),
                  pl.BlockSpec((MOE_TILE, d), lambda i, nc, nu, nt: (i, 0))],
        out_specs=pl.BlockSpec(memory_space=pl.ANY),
        scratch_shapes=[pltpu.VMEM((2, LOCAL_ROWS, d), BF16), pltpu.VMEM((MOE_ROWS, d), BF16),
                        pltpu.SemaphoreType.DMA((2,)), pltpu.SemaphoreType.DMA],
    )
    return pl.pallas_call(
        functools.partial(_dispatch_kernel, n_tiles=n_tiles, n_blocks=n_blocks),
        grid_spec=grid_spec,
        out_shape=jax.ShapeDtypeStruct((n_blocks * MOE_ROWS, d), BF16),
        compiler_params=pltpu.CompilerParams(
            dimension_semantics=("arbitrary",), vmem_limit_bytes=VMEM_LIMIT),
        name="dispatch",
    )(n_chunks, n_used, n_tail, chunk_tab, tail_tab, lp, h2)


def _expert_mlp_kernel(blk_e_ref, nused_ref, x_ref, wgu_ref, bgu_ref, wdn_ref, bdn_ref, y_ref,
                       wgu_bf, wdn_bf):
    i = pl.program_id(0)
    nused = nused_ref[0]
    new_expert = jnp.logical_or(i == 0, blk_e_ref[i] != blk_e_ref[jnp.maximum(i - 1, 0)])

    @pl.when(jnp.logical_and(new_expert, i < nused))
    def _():
        wgu_bf[...] = wgu_ref[...].astype(BF16)
        wdn_bf[...] = wdn_ref[...].astype(BF16)

    @pl.when(i < nused)
    def _():
        x = x_ref[...]
        gu = jnp.dot(x, wgu_bf[...], preferred_element_type=F32) + bgu_ref[...]
        d_ff = gu.shape[1] // 2
        gate = jnp.minimum(gu[:, :d_ff], SWIGLU_LIMIT)
        up = jnp.clip(gu[:, d_ff:], -SWIGLU_LIMIT, SWIGLU_LIMIT)
        act = gate * _sigmoid(SWIGLU_ALPHA * gate) * (up + 1.0)
        y = jnp.dot(act.astype(BF16), wdn_bf[...], preferred_element_type=F32) + bdn_ref[...]
        y_ref[...] = y.astype(BF16)

    @pl.when(i >= nused)
    def _():
        y_ref[...] = jnp.zeros_like(y_ref)


def _expert_mlp(blk_expert, n_used, xs, w_gu, b_gu, w_dn, b_dn):
    d = xs.shape[1]
    n_blocks = blk_expert.shape[0]
    n_rows = n_blocks * MOE_ROWS
    e, _, f2 = w_gu.shape
    grid_spec = pltpu.PrefetchScalarGridSpec(
        num_scalar_prefetch=2,
        grid=(n_blocks,),
        in_specs=[
            pl.BlockSpec((MOE_ROWS, d), lambda i, be, nu: (jnp.minimum(i, nu[0] - 1), 0)),
            pl.BlockSpec((None, d, f2), lambda i, be, nu: (be[i], 0, 0)),
            pl.BlockSpec((None, 1, f2), lambda i, be, nu: (be[i], 0, 0)),
            pl.BlockSpec((None, f2 // 2, d), lambda i, be, nu: (be[i], 0, 0)),
            pl.BlockSpec((None, 1, d), lambda i, be, nu: (be[i], 0, 0)),
        ],
        out_specs=pl.BlockSpec((MOE_ROWS, d), lambda i, be, nu: (i, 0)),
        scratch_shapes=[pltpu.VMEM((d, f2), BF16), pltpu.VMEM((f2 // 2, d), BF16)],
    )
    return pl.pallas_call(
        _expert_mlp_kernel,
        grid_spec=grid_spec,
        out_shape=jax.ShapeDtypeStruct((n_rows, d), BF16),
        compiler_params=pltpu.CompilerParams(
            dimension_semantics=("arbitrary",), vmem_limit_bytes=VMEM_LIMIT),
        name="experts",
    )(blk_expert, n_used, xs, w_gu, b_gu.reshape(e, 1, f2), w_dn, b_dn.reshape(e, 1, d))


def _moe_combine_kernel(nch_ref, tab_cur_ref, tab_nxt_ref, y_hbm, lpt_ref, gate_ref, x1_ref,
                        p_ref, gp_ref, wg_ref, wp_ref, gfin_ref, o_ref, buf, sem, *, n_tiles):
    i = pl.program_id(0)
    slot = i % 2

    @pl.when(i == 0)
    def _():
        buf[...] = jnp.zeros_like(buf)
        _chunk_start_n(buf, y_hbm, tab_cur_ref, sem, 0, nch_ref[0], False)

    @pl.when(i + 1 < n_tiles)
    def _():
        _chunk_start_n(buf, y_hbm, tab_nxt_ref, sem, 1 - slot,
                       nch_ref[jnp.minimum(i + 1, n_tiles - 1)], False)

    _chunk_wait_n(buf, y_hbm, sem, slot, nch_ref[i], False)
    lpt = lpt_ref[...]
    gates = gate_ref[...]
    tm = lpt.shape[0]
    cols = lax.broadcasted_iota(I32, (tm, LOCAL_ROWS), 1)
    weights = jnp.zeros((tm, LOCAL_ROWS), F32)
    for k in range(TOP_K):
        weights = jnp.where(cols == lpt[:, k:k + 1], gates[:, k:k + 1], weights)
    x2 = x1_ref[...] + jnp.dot(weights.astype(BF16), buf[slot], preferred_element_type=F32)
    hg = _rms(x2, gp_ref[...]).astype(BF16)
    gate = _sigmoid(jnp.dot(hg, wg_ref[...], preferred_element_type=F32))
    emb = jnp.dot(p_ref[...].astype(BF16), wp_ref[...], preferred_element_type=F32)
    x3 = x2 + gate * emb
    o_ref[...] = _rms(x3, gfin_ref[...])


def _moe_combine(n_chunks, chunk_tab, y, lpt, gates_tk, x1, p2, ple_g, w_gate, w_proj, fin_g):
    t, d = x1.shape
    n_tiles = t // MOE_TILE
    last = n_tiles - 1
    ple = p2.shape[1]
    row = lambda i, nc: (i, 0)
    const = lambda i, nc: (0, 0)
    grid_spec = pltpu.PrefetchScalarGridSpec(
        num_scalar_prefetch=1,
        grid=(n_tiles,),
        in_specs=[pl.BlockSpec((1, 1, CHUNK_TABLE), lambda i, nc: (i, 0, 0),
                               memory_space=pltpu.SMEM),
                  pl.BlockSpec((1, 1, CHUNK_TABLE),
                               lambda i, nc: (jnp.minimum(i + 1, last), 0, 0),
                               memory_space=pltpu.SMEM),
                  pl.BlockSpec(memory_space=pl.ANY),
                  pl.BlockSpec((MOE_TILE, TOP_K), row),
                  pl.BlockSpec((MOE_TILE, TOP_K), row),
                  pl.BlockSpec((MOE_TILE, d), row),
                  pl.BlockSpec((MOE_TILE, ple), row),
                  pl.BlockSpec((1, d), const),
                  pl.BlockSpec((d, d), const),
                  pl.BlockSpec((ple, d), const),
                  pl.BlockSpec((1, d), const)],
        out_specs=pl.BlockSpec((MOE_TILE, d), row),
        scratch_shapes=[pltpu.VMEM((2, LOCAL_ROWS, d), BF16), pltpu.SemaphoreType.DMA((2,))],
    )
    return pl.pallas_call(
        functools.partial(_moe_combine_kernel, n_tiles=n_tiles),
        grid_spec=grid_spec,
        out_shape=jax.ShapeDtypeStruct((t, d), F32),
        compiler_params=pltpu.CompilerParams(
            dimension_semantics=("arbitrary",), vmem_limit_bytes=VMEM_LIMIT),
        name="combine",
    )(n_chunks, chunk_tab, chunk_tab, y, lpt, gates_tk, x1, p2, ple_g.reshape(1, d),
      w_gate, w_proj, fin_g.reshape(1, d))


def _moe_tables(topi, lrank, cnt):
    k, t = topi.shape
    n_tiles = t // MOE_TILE
    cnt = cnt.astype(I32)
    pc = (cnt + SEG_ALIGN - 1) // SEG_ALIGN * SEG_ALIGN
    l_end = jnp.cumsum(pc, axis=1)
    l_start = l_end - pc
    tot = jnp.sum(pc, axis=0)
    e_pad = (tot + MOE_ROWS - 1) // MOE_ROWS * MOE_ROWS
    e_end = jnp.cumsum(e_pad)
    seg_row0 = (e_end - e_pad)[None, :] + jnp.cumsum(pc, axis=0) - pc
    eids = jnp.arange(N_EXPERTS, dtype=I32)[:, None, None, None]
    topi3 = topi.reshape(k, n_tiles, MOE_TILE)
    base = jnp.sum(jnp.where(topi3[None] == eids, l_start.T[:, None, :, None], 0), axis=0)
    lp = (lrank.reshape(k, n_tiles, MOE_TILE) + base).reshape(k, t)
    j16 = jnp.arange(CHUNK_TABLE, dtype=I32)[None, :] * SEG_ALIGN
    ce = jnp.minimum(jnp.sum((l_end[:, :, None] <= j16[:, None, :]).astype(I32), axis=1),
                     N_EXPERTS - 1)
    pick = ce[:, None, :] == jnp.arange(N_EXPERTS, dtype=I32)[None, :, None]
    seg0 = jnp.sum(jnp.where(pick, seg_row0[:, :, None], 0), axis=1)
    loc0 = jnp.sum(jnp.where(pick, l_start[:, :, None], 0), axis=1)
    n_chunks = l_end[:, -1] // SEG_ALIGN
    valid = jnp.arange(CHUNK_TABLE, dtype=I32)[None, :] < n_chunks[:, None]
    tab = jnp.where(valid, (seg0 + j16 - loc0) // SEG_ALIGN, 0).astype(I32)
    n_rows_max = k * t + n_tiles * N_EXPERTS * (SEG_ALIGN - 1) + N_EXPERTS * (MOE_ROWS - 1)
    n_blocks = (n_rows_max + MOE_ROWS - 1) // MOE_ROWS
    n_used = (e_end[-1] // MOE_ROWS).astype(I32).reshape(1)
    blk_row0 = jnp.arange(n_blocks, dtype=I32) * MOE_ROWS
    blk_expert = jnp.minimum(
        jnp.sum((e_end[None, :] <= blk_row0[:, None]).astype(I32), axis=1), N_EXPERTS - 1)
    t_len = (e_pad - tot) // SEG_ALIGN
    t_end = jnp.cumsum(t_len)
    t_first = (e_end - e_pad + tot) // SEG_ALIGN
    f = jnp.arange(TAIL_TABLE, dtype=I32)
    te = jnp.minimum(jnp.sum((t_end[None, :] <= f[:, None]).astype(I32), axis=1),
                     N_EXPERTS - 1)
    te_pick = te[:, None] == jnp.arange(N_EXPERTS, dtype=I32)[None, :]
    tail = (jnp.sum(jnp.where(te_pick, (t_first - t_end + t_len)[None, :], 0), axis=1) + f)
    tail = jnp.where(f < t_end[-1], tail, 0).astype(I32)
    n_tail = t_end[-1].astype(I32).reshape(1)
    return (lp, n_chunks.astype(I32), tab.reshape(n_tiles, 1, CHUNK_TABLE),
            tail.reshape(1, 1, TAIL_TABLE), n_tail, blk_expert, n_used)


def _layer(x2, p2, seq, norm_mix_g, w_in, b_f, lam_re, lam_im, log_dt, b_re, b_im, c_re, c_im,
           d_skip, w_glu, b_glu, attn_out_g, ssm_out_g, w_out, norm_ffn_g, w_router, b_router,
           w_gu, b_gu, w_dn, b_dn, norm_ple_g, w_ple_gate, w_ple_proj, final_g):
    t, d = x2.shape
    nb = t // seq
    aw = ATTN_WIDTH
    w_main = jnp.concatenate([w_in[:, :3 * aw], w_in[:, 3 * aw + N_HEADS:]], axis=1).astype(BF16)
    wft = w_in[:, 3 * aw:3 * aw + N_HEADS].T.astype(BF16)

    a_re, a_im, bbr, bbi = _ssm_prep(lam_re, lam_im, log_dt, b_re, b_im)
    bre = _block_diag(bbr).astype(BF16)
    bim = _block_diag(bbi).astype(BF16)
    cre = _block_diag(jnp.transpose(c_re, (0, 2, 1))).astype(BF16)
    cim = _block_diag(jnp.transpose(c_im, (0, 2, 1))).astype(BF16)

    q, k, v, u, c_t = _in_proj(x2, norm_mix_g, w_main, wft, b_f, seq=seq)
    shp = (nb, seq, aw)
    attn = _attention(q.reshape(shp), k.reshape(shp), v.reshape(shp), c_t)
    ssm = _ssm(u.reshape(nb, seq, SSM_WIDTH), bre, bim, cre, cim, a_re, a_im,
               d_skip, w_glu.astype(BF16), b_glu, ssm_out_g)

    x1, h2, topi, gates, lrank, cnt = _out_proj(
        x2, attn.reshape(t, aw), ssm.reshape(t, SSM_WIDTH), attn_out_g,
        w_out[:aw].astype(BF16), w_out[aw:].astype(BF16), norm_ffn_g,
        w_router.T.astype(BF16), b_router, tm=MOE_TILE)

    lp, n_chunks, chunk_tab, tail_tab, n_tail, blk_expert, n_used = _moe_tables(
        topi, lrank, cnt[:, :, 0])
    xs = _dispatch(n_chunks, n_used, n_tail, chunk_tab, tail_tab, lp, h2, blk_expert.shape[0])
    y = _expert_mlp(blk_expert, n_used, xs, w_gu, b_gu, w_dn, b_dn)
    return _moe_combine(n_chunks, chunk_tab, y, lp.T, gates.T, x1, p2, norm_ple_g,
                        w_ple_gate.astype(BF16), w_ple_proj.astype(BF16), final_g)


def kernel(x, p, norm_mix_g, w_in, b_f, lam_re, lam_im, log_dt, b_re, b_im, c_re, c_im, d_skip, w_glu, b_glu, attn_out_g, ssm_out_g, w_out, norm_ffn_g, w_router, b_router, w_gu, b_gu, w_dn, b_dn, norm_ple_g, w_ple_gate, w_ple_proj, norm_final_g):
    bsz, seq, d = x.shape
    depth = w_in.shape[0]
    assert depth == 1, "one layer: the final rmsnorm is fused into the layer's last kernel"
    out = _layer(x.reshape(bsz * seq, d), p[0].reshape(bsz * seq, -1), seq,
                 norm_mix_g[0], w_in[0], b_f[0], lam_re[0], lam_im[0], log_dt[0],
                 b_re[0], b_im[0], c_re[0], c_im[0], d_skip[0], w_glu[0], b_glu[0],
                 attn_out_g[0], ssm_out_g[0], w_out[0], norm_ffn_g[0], w_router[0],
                 b_router[0], w_gu[0], b_gu[0], w_dn[0], b_dn[0], norm_ple_g[0],
                 w_ple_gate[0], w_ple_proj[0], norm_final_g)
    return out.reshape(bsz, seq, d)
```

```python
import functools
import math

import jax
import jax.numpy as jnp
from jax import lax
from jax.experimental import pallas as pl
from jax.experimental.pallas import tpu as pltpu

F32 = jnp.float32
BF16 = jnp.bfloat16
I32 = jnp.int32

NORM_EPS = 1e-5
HEAD_DIM = 64
N_HEADS = 8
ATTN_WIDTH = 512
SSM_WIDTH = 512
SSM_GROUP = 16
N_SSM_GROUPS = 32
SSM_STATE = 64
N_STATE = N_SSM_GROUPS * SSM_STATE
N_EXPERTS = 32
TOP_K = 4
SWIGLU_LIMIT = 7.0
SWIGLU_ALPHA = 1.702
LANES = 128
MOE_ROWS = 1024
NEG_BIG = -1e30
LOG2E = math.log2(math.e)
VMEM_LIMIT = 56 * 1024 * 1024

_NT = (((1,), (1,)), ((), ()))


def _rms(xf, g):
    ms = jnp.mean(xf * xf, axis=-1, keepdims=True)
    return xf * lax.rsqrt(ms + NORM_EPS) * g


def _sigmoid(x):
    return 1.0 / (1.0 + jnp.exp(-x))


def _ssm_prep_kernel(lr_ref, li_ref, ldt_ref, brt_ref, bit_ref,
                     ar_ref, ai_ref, bbr_ref, bbi_ref):
    lr = lr_ref[...]
    li = li_ref[...]
    dt = jnp.exp(ldt_ref[...])
    mag = jnp.exp(lr * dt)
    ab_re = mag * jnp.cos(li * dt)
    ab_im = mag * jnp.sin(li * dt)
    den = lr * lr + li * li
    nr = ab_re - 1.0
    z_re = (nr * lr + ab_im * li) / den
    z_im = (ab_im * lr - nr * li) / den
    ar_ref[...] = ab_re
    ai_ref[...] = ab_im
    br = brt_ref[...]
    bi = bit_ref[...]
    bbr_ref[...] = z_re * br - z_im * bi
    bbi_ref[...] = z_re * bi + z_im * br


def _ssm_prep(lam_re, lam_im, log_dt, b_re, b_im):
    g, p, c = b_re.shape
    brt = jnp.transpose(b_re, (0, 2, 1))
    bit = jnp.transpose(b_im, (0, 2, 1))
    return pl.pallas_call(
        _ssm_prep_kernel,
        out_shape=(jax.ShapeDtypeStruct((g, 1, p), F32), jax.ShapeDtypeStruct((g, 1, p), F32),
                   jax.ShapeDtypeStruct((g, c, p), F32), jax.ShapeDtypeStruct((g, c, p), F32)),
        name="ssm_prep",
    )(lam_re.reshape(g, 1, p), lam_im.reshape(g, 1, p), log_dt.reshape(g, 1, 1), brt, bit)


def _block_diag(w):
    g, a, b = w.shape
    half = g // 2
    eye = jnp.eye(half, dtype=w.dtype)
    w4 = w.reshape(2, half, a, b)
    out = jnp.einsum('hgab,gk->hgakb', w4, eye)
    return out.reshape(2, half * a, half * b)


def _inproj_kernel(x_ref, g_ref, w_ref, wft_ref, bf_ref, tri_ref,
                   q_ref, k_ref, v_ref, u_ref, c_ref, carry_ref, *, tiles_per_seq, tm):
    i = pl.program_id(0)

    @pl.when(i % tiles_per_seq == 0)
    def _():
        carry_ref[...] = jnp.zeros_like(carry_ref)

    h = _rms(x_ref[...], g_ref[...]).astype(BF16)
    proj = jnp.dot(h, w_ref[...], preferred_element_type=F32)
    aw = ATTN_WIDTH
    q_ref[...] = (proj[:, 0:aw] * (LOG2E * HEAD_DIM ** -0.5)).astype(BF16)
    k_ref[...] = proj[:, aw:2 * aw].astype(BF16)
    v_ref[...] = proj[:, 2 * aw:3 * aw].astype(BF16)
    u_ref[...] = proj[:, 3 * aw:3 * aw + SSM_WIDTH].astype(BF16)

    fl = lax.dot_general(wft_ref[...], h, _NT, preferred_element_type=F32)
    z = fl + bf_ref[...]
    lf = jnp.minimum(z, 0.0) - jnp.log1p(jnp.exp(-jnp.abs(z)))
    hi = lf.astype(BF16)
    lo = (lf - hi.astype(F32)).astype(BF16)
    tri = tri_ref[...]
    cs = (jnp.dot(hi, tri, preferred_element_type=F32)
          + jnp.dot(lo, tri, preferred_element_type=F32))
    c = cs + carry_ref[:, 0:1]
    c_ref[...] = c * LOG2E
    carry_ref[...] = jnp.broadcast_to(c[:, tm - 1:tm], carry_ref.shape)


def _in_proj(x2, norm_g, w_main, wft, b_f, *, seq, tm=1024):
    t, d = x2.shape
    n_main = w_main.shape[1]
    tri = jnp.triu(jnp.ones((tm, tm), F32)).astype(BF16)
    kern = functools.partial(_inproj_kernel, tiles_per_seq=seq // tm, tm=tm)
    row = lambda i: (i, 0)
    const = lambda i: (0, 0)
    act = jax.ShapeDtypeStruct((t, ATTN_WIDTH), BF16)
    return pl.pallas_call(
        kern,
        grid=(t // tm,),
        in_specs=[pl.BlockSpec((tm, d), row),
                  pl.BlockSpec((1, d), const),
                  pl.BlockSpec((d, n_main), const),
                  pl.BlockSpec((N_HEADS, d), const),
                  pl.BlockSpec((N_HEADS, 1), const),
                  pl.BlockSpec((tm, tm), const)],
        out_specs=[pl.BlockSpec((tm, ATTN_WIDTH), row)] * 4
        + [pl.BlockSpec((N_HEADS, tm), lambda i: (0, i))],
        out_shape=[act, act, act, act, jax.ShapeDtypeStruct((N_HEADS, t), F32)],
        scratch_shapes=[pltpu.VMEM((N_HEADS, LANES), F32)],
        compiler_params=pltpu.CompilerParams(
            dimension_semantics=("arbitrary",), vmem_limit_bytes=VMEM_LIMIT),
        name="in_proj",
    )(x2, norm_g.reshape(1, d), w_main, wft, b_f.reshape(N_HEADS, 1), tri)


def _attn_kernel(q_ref, k_ref, v_ref, c_ref, o_ref, *, tq, n_pairs):
    i = pl.program_id(2)
    lane = lax.broadcasted_iota(I32, (1, LANES), 1)
    first = lane < HEAD_DIM
    n_heads = 2 * n_pairs
    q_heads = []
    for p in range(n_pairs):
        q2 = q_ref[:, p * LANES:(p + 1) * LANES]
        zero = jnp.zeros_like(q2)
        q_heads += [jnp.where(first, q2, zero), jnp.where(first, zero, q2)]
    half = tq // 2

    def block(off, width, r0, carry, masked):
        cj = c_ref[:, pl.ds(off, width)]
        out = []
        for p in range(n_pairs):
            kj = k_ref[pl.ds(off, width), p * LANES:(p + 1) * LANES]
            vj = v_ref[pl.ds(off, width), p * LANES:(p + 1) * LANES]
            one = jnp.ones_like(vj)
            v_heads = (jnp.where(first, vj, one), jnp.where(first, one, vj))
            for h in range(2):
                m, acc = carry[2 * p + h]
                s = (lax.dot_general(q_heads[2 * p + h][r0:], kj, _NT,
                                     preferred_element_type=F32)
                     - cj[2 * p + h:2 * p + h + 1, :])
                if masked:
                    rr = lax.broadcasted_iota(I32, s.shape, 0)
                    cc = lax.broadcasted_iota(I32, s.shape, 1)
                    s = jnp.where(cc <= rr, s, NEG_BIG)
                m_new = jnp.maximum(m[r0:], jnp.max(s, axis=-1, keepdims=True))
                alpha = jnp.exp2(m[r0:] - m_new)
                pr = jnp.exp2(s - m_new).astype(BF16)
                acc_new = alpha * acc[r0:] + jnp.dot(pr, v_heads[h], preferred_element_type=F32)
                if r0:
                    m_new = jnp.concatenate([m[:r0], m_new], axis=0)
                    acc_new = jnp.concatenate([acc[:r0], acc_new], axis=0)
                out.append((m_new, acc_new))
        return tuple(out)

    init_one = (jnp.full((tq, 1), NEG_BIG, F32), jnp.zeros((tq, LANES), F32))
    carry = lax.fori_loop(
        0, i, lambda j, c: block(pl.multiple_of(j * tq, tq), tq, 0, c, False),
        (init_one,) * n_heads)
    diag = pl.multiple_of(i * tq, tq)
    for n in range(tq // half):
        carry = block(pl.multiple_of(diag + n * half, half), half, n * half, carry, True)
    outs = []
    for p in range(n_pairs):
        acc_a = carry[2 * p][1]
        acc_b = carry[2 * p + 1][1]
        outs.append(jnp.where(first, acc_a / pltpu.roll(acc_a, HEAD_DIM, axis=1),
                              acc_b / pltpu.roll(acc_b, HEAD_DIM, axis=1)))
    o_ref[...] = jnp.concatenate(outs, axis=-1).astype(BF16)


def _attention(q, k, v, c_t, *, tq=1024, n_pairs=2):
    b, s, w = q.shape
    wb = n_pairs * LANES
    n_groups = w // wb
    c3 = c_t.reshape(n_groups, 2 * n_pairs, b * s)
    kern = functools.partial(_attn_kernel, tq=tq, n_pairs=n_pairs)
    return pl.pallas_call(
        kern,
        grid=(b, n_groups, s // tq),
        in_specs=[pl.BlockSpec((None, tq, wb), lambda bi, g, i: (bi, i, g)),
                  pl.BlockSpec((None, s, wb), lambda bi, g, i: (bi, 0, g)),
                  pl.BlockSpec((None, s, wb), lambda bi, g, i: (bi, 0, g)),
                  pl.BlockSpec((None, 2 * n_pairs, s), lambda bi, g, i: (g, 0, bi))],
        out_specs=pl.BlockSpec((None, tq, wb), lambda bi, g, i: (bi, i, g)),
        out_shape=jax.ShapeDtypeStruct((b, s, w), BF16),
        compiler_params=pltpu.CompilerParams(
            dimension_semantics=("arbitrary", "arbitrary", "arbitrary"),
            vmem_limit_bytes=VMEM_LIMIT),
        name="attention",
    )(q, k, v, c3)


def _ssm_kernel(u_ref, bre_ref, bim_ref, cre_ref, cim_ref, ar_ref, ai_ref, dsk_ref,
                wglu_ref, bglu_ref, g_ref, o_ref,
                us_ref, xr_ref, xi_ref, str_ref, sti_ref, res_ref, *, tt, nb):
    i = pl.program_id(0)

    @pl.when(i == 0)
    def _():
        str_ref[...] = jnp.zeros_like(str_ref)
        sti_ref[...] = jnp.zeros_like(sti_ref)

    n_chunks = SSM_WIDTH // LANES
    for b in range(nb):
        ub32 = u_ref[b].astype(F32)
        for c in range(n_chunks):
            us_ref[c, pl.ds(b, tt, stride=nb), :] = ub32[:, c * LANES:(c + 1) * LANES]
    uf = jnp.concatenate([us_ref[c] for c in range(n_chunks)], axis=-1)
    ub = uf.astype(BF16)
    half_in = SSM_WIDTH // 2
    half_st = N_STATE // 2
    for hf in range(2):
        uh = ub[:, hf * half_in:(hf + 1) * half_in]
        xr_ref[:, hf * half_st:(hf + 1) * half_st] = jnp.dot(
            uh, bre_ref[hf], preferred_element_type=F32)
        xi_ref[:, hf * half_st:(hf + 1) * half_st] = jnp.dot(
            uh, bim_ref[hf], preferred_element_type=F32)

    n_col_groups = 2
    wcol = N_STATE // n_col_groups
    unroll = 4
    for cg in range(n_col_groups):
        cols = slice(cg * wcol, (cg + 1) * wcol)
        ar = jnp.broadcast_to(ar_ref[:, cols], (nb, wcol))
        ai = jnp.broadcast_to(ai_ref[:, cols], (nb, wcol))

        def steps(tb, carry, cols=cols, ar=ar, ai=ai):
            sr, si = carry
            for k in range(unroll):
                r0 = pl.multiple_of((tb * unroll + k) * nb, nb)
                br = xr_ref[pl.ds(r0, nb), cols]
                bi = xi_ref[pl.ds(r0, nb), cols]
                nr = ar * sr - ai * si + br
                ni = ar * si + ai * sr + bi
                xr_ref[pl.ds(r0, nb), cols] = nr
                xi_ref[pl.ds(r0, nb), cols] = ni
                sr, si = nr, ni
            return sr, si

        sr, si = lax.fori_loop(0, tt // unroll, steps, (str_ref[:, cols], sti_ref[:, cols]))
        str_ref[:, cols] = sr
        sti_ref[:, cols] = si

    ys = []
    for hf in range(2):
        xr = xr_ref[:, hf * half_st:(hf + 1) * half_st].astype(BF16)
        xi = xi_ref[:, hf * half_st:(hf + 1) * half_st].astype(BF16)
        ys.append(jnp.dot(xr, cre_ref[hf], preferred_element_type=F32)
                  - jnp.dot(xi, cim_ref[hf], preferred_element_type=F32))
    y = jnp.concatenate(ys, axis=-1) + dsk_ref[...] * uf
    gl = 0.5 * y * (1.0 + jnp.tanh(math.sqrt(2.0 / math.pi) * (y + 0.044715 * (y * y * y))))
    zz = jnp.dot(gl.astype(BF16), wglu_ref[...], preferred_element_type=F32) + bglu_ref[...]
    out = gl * _sigmoid(zz)
    res = _rms(out, g_ref[...])
    for c in range(n_chunks):
        res_ref[c] = res[:, c * LANES:(c + 1) * LANES]
    for b in range(nb):
        o_ref[b] = jnp.concatenate(
            [res_ref[c, pl.ds(b, tt, stride=nb), :] for c in range(n_chunks)],
            axis=-1).astype(BF16)


def _ssm(u3, bre, bim, cre, cim, a_re, a_im, d_skip, w_glu, b_glu, out_g, *, tt=128):
    nb, s, w = u3.shape
    rows = tt * nb
    kern = functools.partial(_ssm_kernel, tt=tt, nb=nb)
    c3 = lambda i: (0, 0, 0)
    c2 = lambda i: (0, 0)
    return pl.pallas_call(
        kern,
        grid=(s // tt,),
        in_specs=[pl.BlockSpec((nb, tt, w), lambda i: (0, i, 0)),
                  pl.BlockSpec(bre.shape, c3), pl.BlockSpec(bim.shape, c3),
                  pl.BlockSpec(cre.shape, c3), pl.BlockSpec(cim.shape, c3),
                  pl.BlockSpec((1, N_STATE), c2), pl.BlockSpec((1, N_STATE), c2),
                  pl.BlockSpec((1, w), c2),
                  pl.BlockSpec((w, w), c2), pl.BlockSpec((1, w), c2), pl.BlockSpec((1, w), c2)],
        out_specs=pl.BlockSpec((nb, tt, w), lambda i: (0, i, 0)),
        out_shape=jax.ShapeDtypeStruct((nb, s, w), BF16),
        scratch_shapes=[pltpu.VMEM((w // LANES, rows, LANES), F32),
                        pltpu.VMEM((rows, N_STATE), F32), pltpu.VMEM((rows, N_STATE), F32),
                        pltpu.VMEM((nb, N_STATE), F32), pltpu.VMEM((nb, N_STATE), F32),
                        pltpu.VMEM((w // LANES, rows, LANES), F32)],
        compiler_params=pltpu.CompilerParams(
            dimension_semantics=("arbitrary",), vmem_limit_bytes=VMEM_LIMIT),
        name="ssm",
    )(u3, bre, bim, cre, cim, a_re.reshape(1, N_STATE), a_im.reshape(1, N_STATE),
      d_skip.reshape(1, w), w_glu, b_glu.reshape(1, w), out_g.reshape(1, w))


def _outproj_kernel(x_ref, a_ref, s_ref, ga_ref, woa_ref, wos_ref, gf_ref, wrt_ref, br_ref,
                    tri_ref, trie_ref, x1_ref, h2_ref, topi_ref, gate_ref, rank_ref, cnt_ref,
                    *, tm):
    a = _rms(a_ref[...].astype(F32), ga_ref[...]).astype(BF16)
    x1 = (x_ref[...] + jnp.dot(a, woa_ref[...], preferred_element_type=F32)
          + jnp.dot(s_ref[...], wos_ref[...], preferred_element_type=F32))
    x1_ref[...] = x1
    h2 = _rms(x1, gf_ref[...]).astype(BF16)
    h2_ref[...] = h2

    lg = lax.dot_general(wrt_ref[...], h2, _NT,
                         preferred_element_type=F32) + br_ref[...]
    ids = lax.broadcasted_iota(I32, (N_EXPERTS, tm), 0)
    work = lg
    vals, idxs = [], []
    for _ in range(TOP_K):
        m = jnp.max(work, axis=0, keepdims=True)
        idx = jnp.min(jnp.where(work == m, ids, N_EXPERTS), axis=0, keepdims=True)
        vals.append(m)
        idxs.append(idx)
        work = jnp.where(ids == idx, -jnp.inf, work)
    exps = [jnp.exp(v - vals[0]) for v in vals]
    den = exps[0] + exps[1] + exps[2] + exps[3]
    gate_ref[...] = jnp.concatenate([e / den for e in exps], axis=0)
    topi_ref[...] = jnp.concatenate(idxs, axis=0)

    sel = jnp.zeros((N_EXPERTS, tm), F32)
    for idx in idxs:
        sel = sel + jnp.where(ids == idx, 1.0, 0.0)
    cnt = jnp.broadcast_to(jnp.sum(sel, axis=1, keepdims=True), (N_EXPERTS, LANES))
    padded = jnp.floor((cnt + (SEG_ALIGN - 1)) * (1.0 / SEG_ALIGN)) * SEG_ALIGN
    seg_start = jnp.dot(trie_ref[...], padded.astype(BF16), preferred_element_type=F32)
    before = (jnp.dot(sel.astype(BF16), tri_ref[...], preferred_element_type=F32)
              + seg_start[:, 0:1])
    ranks = [jnp.sum(jnp.where(ids == idx, before, 0.0), axis=0, keepdims=True) for idx in idxs]
    rank_ref[...] = jnp.concatenate(ranks, axis=0).astype(I32)
    cnt_ref[0] = cnt


def _out_proj(x2, attn, ssm, attn_g, wo_a, wo_s, ffn_g, wrt, b_router, *, tm=512):
    t, d = x2.shape
    tri = jnp.triu(jnp.ones((tm, tm), F32), k=1).astype(BF16)
    tri_e = jnp.tril(jnp.ones((N_EXPERTS, N_EXPERTS), F32), k=-1).astype(BF16)
    kern = functools.partial(_outproj_kernel, tm=tm)
    row = lambda i: (i, 0)
    const = lambda i: (0, 0)
    colblk = lambda i: (0, i)
    return pl.pallas_call(
        kern,
        grid=(t // tm,),
        in_specs=[pl.BlockSpec((tm, d), row),
                  pl.BlockSpec((tm, ATTN_WIDTH), row),
                  pl.BlockSpec((tm, SSM_WIDTH), row),
                  pl.BlockSpec((1, ATTN_WIDTH), const),
                  pl.BlockSpec((ATTN_WIDTH, d), const),
                  pl.BlockSpec((SSM_WIDTH, d), const),
                  pl.BlockSpec((1, d), const),
                  pl.BlockSpec((N_EXPERTS, d), const),
                  pl.BlockSpec((N_EXPERTS, 1), const),
                  pl.BlockSpec((tm, tm), const),
                  pl.BlockSpec((N_EXPERTS, N_EXPERTS), const)],
        out_specs=[pl.BlockSpec((tm, d), row),
                   pl.BlockSpec((tm, d), row),
                   pl.BlockSpec((TOP_K, tm), colblk),
                   pl.BlockSpec((TOP_K, tm), colblk),
                   pl.BlockSpec((TOP_K, tm), colblk),
                   pl.BlockSpec((1, N_EXPERTS, LANES), lambda i: (i, 0, 0))],
        out_shape=[jax.ShapeDtypeStruct((t, d), F32),
                   jax.ShapeDtypeStruct((t, d), BF16),
                   jax.ShapeDtypeStruct((TOP_K, t), I32),
                   jax.ShapeDtypeStruct((TOP_K, t), F32),
                   jax.ShapeDtypeStruct((TOP_K, t), I32),
                   jax.ShapeDtypeStruct((t // tm, N_EXPERTS, LANES), F32)],
        compiler_params=pltpu.CompilerParams(
            dimension_semantics=("arbitrary",), vmem_limit_bytes=VMEM_LIMIT),
        name="out_proj",
    )(x2, attn, ssm, attn_g.reshape(1, -1), wo_a, wo_s, ffn_g.reshape(1, d), wrt,
      b_router.reshape(N_EXPERTS, 1), tri, tri_e)


SEG_ALIGN = 16
MOE_TILE = 512
LOCAL_ROWS = TOP_K * MOE_TILE + N_EXPERTS * SEG_ALIGN
CHUNK_TABLE = 256
TAIL_TABLE = N_EXPERTS * (MOE_ROWS // SEG_ALIGN)
START_GROUP = 4
WAIT_GROUP = 8


def _chunk_copy(src, dst, sem):
    return pltpu.make_async_copy(src, dst, sem)


def _chunk_wait_n(local_buf, hbm, sem, slot, n, to_hbm):
    def wait_rows(rows):
        loc = local_buf.at[slot, pl.ds(0, rows), :]
        far = hbm.at[pl.ds(0, rows), :]

        def body(j, carry):
            (_chunk_copy(loc, far, sem.at[slot]) if to_hbm
             else _chunk_copy(far, loc, sem.at[slot])).wait()
            return carry
        return body

    n_groups = n // WAIT_GROUP
    lax.fori_loop(0, n_groups, wait_rows(WAIT_GROUP * SEG_ALIGN), 0)
    lax.fori_loop(n_groups * WAIT_GROUP, n, wait_rows(SEG_ALIGN), 0)


def _chunk_start_n(local_buf, hbm, tab_ref, sem, slot, n, to_hbm):
    def start(j, priority):
        loc = local_buf.at[slot, pl.ds(pl.multiple_of(j * SEG_ALIGN, SEG_ALIGN), SEG_ALIGN), :]
        far = hbm.at[pl.ds(pl.multiple_of(tab_ref[0, 0, j] * SEG_ALIGN, SEG_ALIGN), SEG_ALIGN), :]
        (_chunk_copy(loc, far, sem.at[slot]) if to_hbm
         else _chunk_copy(far, loc, sem.at[slot])).start(priority=priority)

    def group(g, carry):
        for u in range(START_GROUP):
            start(g * START_GROUP + u, u % 2)
        return carry

    def single(j, carry):
        start(j, 0)
        return carry

    n_groups = n // START_GROUP
    lax.fori_loop(0, n_groups, group, 0)
    lax.fori_loop(n_groups * START_GROUP, n, single, 0)


def _dispatch_kernel(nch_ref, nused_ref, ntail_ref, tab_ref, tail_ref, lp_ref, h2_ref, xs_hbm,
                     buf, zbuf, sem, zsem, *, n_tiles, n_blocks):
    i = pl.program_id(0)
    slot = i % 2

    @pl.when(i >= 2)
    def _():
        _chunk_wait_n(buf, xs_hbm, sem, slot, nch_ref[jnp.maximum(i - 2, 0)], True)

    lp = lp_ref[...]
    tm = lp.shape[1]

    rows = lax.broadcasted_iota(I32, (LOCAL_ROWS, tm), 0)
    onehot = jnp.zeros((LOCAL_ROWS, tm), F32)
    for k in range(TOP_K):
        onehot = jnp.where(rows == lp[k:k + 1, :], 1.0, onehot)
    buf[slot] = jnp.dot(onehot.astype(BF16), h2_ref[...],
                        preferred_element_type=F32).astype(BF16)
    _chunk_start_n(buf, xs_hbm, tab_ref, sem, slot, nch_ref[i], True)

    @pl.when(i == n_tiles - 1)
    def _():
        zbuf[...] = jnp.zeros_like(zbuf)
        zrow = zbuf.at[pl.ds(0, SEG_ALIGN), :]

        def fill(j, carry):
            row0 = pl.multiple_of(tail_ref[0, 0, j] * SEG_ALIGN, SEG_ALIGN)
            _chunk_copy(zrow, xs_hbm.at[pl.ds(row0, SEG_ALIGN), :], zsem).start()
            return carry

        def drain(j, carry):
            _chunk_copy(zrow, xs_hbm.at[pl.ds(0, SEG_ALIGN), :], zsem).wait()
            return carry

        def fill_block(b, carry):
            row0 = pl.multiple_of(b * MOE_ROWS, MOE_ROWS)
            _chunk_copy(zbuf, xs_hbm.at[pl.ds(row0, MOE_ROWS), :], zsem).start()
            return carry

        def drain_block(b, carry):
            _chunk_copy(zbuf, xs_hbm.at[pl.ds(0, MOE_ROWS), :], zsem).wait()
            return carry

        lax.fori_loop(0, ntail_ref[0], fill, 0)
        lax.fori_loop(0, ntail_ref[0], drain, 0)
        lax.fori_loop(nused_ref[0], n_blocks, fill_block, 0)
        lax.fori_loop(nused_ref[0], n_blocks, drain_block, 0)
        _chunk_wait_n(buf, xs_hbm, sem, slot, nch_ref[i], True)
        if n_tiles > 1:
            _chunk_wait_n(buf, xs_hbm, sem, 1 - slot, nch_ref[jnp.maximum(i - 1, 0)], True)


def _dispatch(n_chunks, n_used, n_tail, chunk_tab, tail_tab, lp, h2, n_blocks):
    k, t = lp.shape
    d = h2.shape[1]
    n_tiles = t // MOE_TILE
    grid_spec = pltpu.PrefetchScalarGridSpec(
        num_scalar_prefetch=3,
        grid=(n_tiles,),
        in_specs=[pl.BlockSpec((1, 1, CHUNK_TABLE), lambda i, nc, nu, nt: (i, 0, 0),
                               memory_space=pltpu.SMEM),
                  pl.BlockSpec((1, 1, TAIL_TABLE), lambda i, nc, nu, nt: (0, 0, 0),
                               memory_space=pltpu.SMEM),
                  pl.BlockSpec((k, MOE_TILE), lambda i, nc, nu, nt: (0, i)),
                  pl.BlockSpec((MOE_TILE, d), lambda i, nc, nu, nt: (i, 0))],
        out_specs=pl.BlockSpec(memory_space=pl.ANY),
        scratch_shapes=[pltpu.VMEM((2, LOCAL_ROWS, d), BF16), pltpu.VMEM((MOE_ROWS, d), BF16),
                        pltpu.SemaphoreType.DMA((2,)), pltpu.SemaphoreType.DMA],
    )
    return pl.pallas_call(
        functools.partial(_dispatch_kernel, n_tiles=n_tiles, n_blocks=n_blocks),
        grid_spec=grid_spec,
        out_shape=jax.ShapeDtypeStruct((n_blocks * MOE_ROWS, d), BF16),
        compiler_params=pltpu.CompilerParams(
            dimension_semantics=("arbitrary",), vmem_limit_bytes=VMEM_LIMIT),
        name="dispatch",
    )(n_chunks, n_used, n_tail, chunk_tab, tail_tab, lp, h2)


def _expert_mlp_kernel(blk_e_ref, nused_ref, x_ref, wgu_ref, bgu_ref, wdn_ref, bdn_ref, y_ref,
                       wgu_bf, wdn_bf):
    i = pl.program_id(0)
    nused = nused_ref[0]
    new_expert = jnp.logical_or(i == 0, blk_e_ref[i] != blk_e_ref[jnp.maximum(i - 1, 0)])

    @pl.when(jnp.logical_and(new_expert, i < nused))
    def _():
        wgu_bf[...] = wgu_ref[...].astype(BF16)
        wdn_bf[...] = wdn_ref[...].astype(BF16)

    @pl.when(i < nused)
    def _():
        x = x_ref[...]
        gu = jnp.dot(x, wgu_bf[...], preferred_element_type=F32) + bgu_ref[...]
        d_ff = gu.shape[1] // 2
        gate = jnp.minimum(gu[:, :d_ff], SWIGLU_LIMIT)
        up = jnp.clip(gu[:, d_ff:], -SWIGLU_LIMIT, SWIGLU_LIMIT)
        act = gate * _sigmoid(SWIGLU_ALPHA * gate) * (up + 1.0)
        y = jnp.dot(act.astype(BF16), wdn_bf[...], preferred_element_type=F32) + bdn_ref[...]
        y_ref[...] = y.astype(BF16)

    @pl.when(i >= nused)
    def _():
        y_ref[...] = jnp.zeros_like(y_ref)


def _expert_mlp(blk_expert, n_used, xs, w_gu, b_gu, w_dn, b_dn):
    d = xs.shape[1]
    n_blocks = blk_expert.shape[0]
    n_rows = n_blocks * MOE_ROWS
    e, _, f2 = w_gu.shape
    grid_spec = pltpu.PrefetchScalarGridSpec(
        num_scalar_prefetch=2,
        grid=(n_blocks,),
        in_specs=[
            pl.BlockSpec((MOE_ROWS, d), lambda i, be, nu: (jnp.minimum(i, nu[0] - 1), 0)),
            pl.BlockSpec((None, d, f2), lambda i, be, nu: (be[i], 0, 0)),
            pl.BlockSpec((None, 1, f2), lambda i, be, nu: (be[i], 0, 0)),
            pl.BlockSpec((None, f2 // 2, d), lambda i, be, nu: (be[i], 0, 0)),
            pl.BlockSpec((None, 1, d), lambda i, be, nu: (be[i], 0, 0)),
        ],
        out_specs=pl.BlockSpec((MOE_ROWS, d), lambda i, be, nu: (i, 0)),
        scratch_shapes=[pltpu.VMEM((d, f2), BF16), pltpu.VMEM((f2 // 2, d), BF16)],
    )
    return pl.pallas_call(
        _expert_mlp_kernel,
        grid_spec=grid_spec,
        out_shape=jax.ShapeDtypeStruct((n_rows, d), BF16),
        compiler_params=pltpu.CompilerParams(
            dimension_semantics=("arbitrary",), vmem_limit_bytes=VMEM_LIMIT),
        name="experts",
    )(blk_expert, n_used, xs, w_gu, b_gu.reshape(e, 1, f2), w_dn, b_dn.reshape(e, 1, d))


def _moe_combine_kernel(nch_ref, tab_cur_ref, tab_nxt_ref, y_hbm, lpt_ref, gate_ref, x1_ref,
                        p_ref, gp_ref, wg_ref, wp_ref, gfin_ref, o_ref, buf, sem, *, n_tiles):
    i = pl.program_id(0)
    slot = i % 2

    @pl.when(i == 0)
    def _():
        buf[...] = jnp.zeros_like(buf)
        _chunk_start_n(buf, y_hbm, tab_cur_ref, sem, 0, nch_ref[0], False)

    @pl.when(i + 1 < n_tiles)
    def _():
        _chunk_start_n(buf, y_hbm, tab_nxt_ref, sem, 1 - slot,
                       nch_ref[jnp.minimum(i + 1, n_tiles - 1)], False)

    _chunk_wait_n(buf, y_hbm, sem, slot, nch_ref[i], False)
    lpt = lpt_ref[...]
    gates = gate_ref[...]
    tm = lpt.shape[0]
    cols = lax.broadcasted_iota(I32, (tm, LOCAL_ROWS), 1)
    weights = jnp.zeros((tm, LOCAL_ROWS), F32)
    for k in range(TOP_K):
        weights = jnp.where(cols == lpt[:, k:k + 1], gates[:, k:k + 1], weights)
    x2 = x1_ref[...] + jnp.dot(weights.astype(BF16), buf[slot], preferred_element_type=F32)
    hg = _rms(x2, gp_ref[...]).astype(BF16)
    gate = _sigmoid(jnp.dot(hg, wg_ref[...], preferred_element_type=F32))
    emb = jnp.dot(p_ref[...].astype(BF16), wp_ref[...], preferred_element_type=F32)
    x3 = x2 + gate * emb
    o_ref[...] = _rms(x3, gfin_ref[...])


def _moe_combine(n_chunks, chunk_tab, y, lpt, gates_tk, x1, p2, ple_g, w_gate, w_proj, fin_g):
    t, d = x1.shape
    n_tiles = t // MOE_TILE
    last = n_tiles - 1
    ple = p2.shape[1]
    row = lambda i, nc: (i, 0)
    const = lambda i, nc: (0, 0)
    grid_spec = pltpu.PrefetchScalarGridSpec(
        num_scalar_prefetch=1,
        grid=(n_tiles,),
        in_specs=[pl.BlockSpec((1, 1, CHUNK_TABLE), lambda i, nc: (i, 0, 0),
                               memory_space=pltpu.SMEM),
                  pl.BlockSpec((1, 1, CHUNK_TABLE),
                               lambda i, nc: (jnp.minimum(i + 1, last), 0, 0),
                               memory_space=pltpu.SMEM),
                  pl.BlockSpec(memory_space=pl.ANY),
                  pl.BlockSpec((MOE_TILE, TOP_K), row),
                  pl.BlockSpec((MOE_TILE, TOP_K), row),
                  pl.BlockSpec((MOE_TILE, d), row),
                  pl.BlockSpec((MOE_TILE, ple), row),
                  pl.BlockSpec((1, d), const),
                  pl.BlockSpec((d, d), const),
                  pl.BlockSpec((ple, d), const),
                  pl.BlockSpec((1, d), const)],
        out_specs=pl.BlockSpec((MOE_TILE, d), row),
        scratch_shapes=[pltpu.VMEM((2, LOCAL_ROWS, d), BF16), pltpu.SemaphoreType.DMA((2,))],
    )
    return pl.pallas_call(
        functools.partial(_moe_combine_kernel, n_tiles=n_tiles),
        grid_spec=grid_spec,
        out_shape=jax.ShapeDtypeStruct((t, d), F32),
        compiler_params=pltpu.CompilerParams(
            dimension_semantics=("arbitrary",), vmem_limit_bytes=VMEM_LIMIT),
        name="combine",
    )(n_chunks, chunk_tab, chunk_tab, y, lpt, gates_tk, x1, p2, ple_g.reshape(1, d),
      w_gate, w_proj, fin_g.reshape(1, d))


def _moe_tables(cnt, n_assign):
    n_tiles = cnt.shape[0]
    cnt = cnt.astype(I32)
    pc = (cnt + SEG_ALIGN - 1) // SEG_ALIGN * SEG_ALIGN
    l_end = jnp.cumsum(pc, axis=1)
    l_start = l_end - pc
    tot = jnp.sum(pc, axis=0)
    e_pad = (tot + MOE_ROWS - 1) // MOE_ROWS * MOE_ROWS
    e_end = jnp.cumsum(e_pad)
    seg_row0 = (e_end - e_pad)[None, :] + jnp.cumsum(pc, axis=0) - pc
    j16 = jnp.arange(CHUNK_TABLE, dtype=I32)[None, :] * SEG_ALIGN
    ce = jnp.minimum(jnp.sum((l_end[:, :, None] <= j16[:, None, :]).astype(I32), axis=1),
                     N_EXPERTS - 1)
    pick = ce[:, None, :] == jnp.arange(N_EXPERTS, dtype=I32)[None, :, None]
    seg0 = jnp.sum(jnp.where(pick, seg_row0[:, :, None], 0), axis=1)
    loc0 = jnp.sum(jnp.where(pick, l_start[:, :, None], 0), axis=1)
    n_chunks = l_end[:, -1] // SEG_ALIGN
    valid = jnp.arange(CHUNK_TABLE, dtype=I32)[None, :] < n_chunks[:, None]
    tab = jnp.where(valid, (seg0 + j16 - loc0) // SEG_ALIGN, 0).astype(I32)
    n_rows_max = n_assign + n_tiles * N_EXPERTS * (SEG_ALIGN - 1) + N_EXPERTS * (MOE_ROWS - 1)
    n_blocks = (n_rows_max + MOE_ROWS - 1) // MOE_ROWS
    n_used = (e_end[-1] // MOE_ROWS).astype(I32).reshape(1)
    blk_row0 = jnp.arange(n_blocks, dtype=I32) * MOE_ROWS
    blk_expert = jnp.minimum(
        jnp.sum((e_end[None, :] <= blk_row0[:, None]).astype(I32), axis=1), N_EXPERTS - 1)
    t_len = (e_pad - tot) // SEG_ALIGN
    t_end = jnp.cumsum(t_len)
    t_first = (e_end - e_pad + tot) // SEG_ALIGN
    f = jnp.arange(TAIL_TABLE, dtype=I32)
    te = jnp.minimum(jnp.sum((t_end[None, :] <= f[:, None]).astype(I32), axis=1),
                     N_EXPERTS - 1)
    te_pick = te[:, None] == jnp.arange(N_EXPERTS, dtype=I32)[None, :]
    tail = (jnp.sum(jnp.where(te_pick, (t_first - t_end + t_len)[None, :], 0), axis=1) + f)
    tail = jnp.where(f < t_end[-1], tail, 0).astype(I32)
    n_tail = t_end[-1].astype(I32).reshape(1)
    return (n_chunks.astype(I32), tab.reshape(n_tiles, 1, CHUNK_TABLE),
            tail.reshape(1, 1, TAIL_TABLE), n_tail, blk_expert, n_used)


def _layer(x2, p2, seq, norm_mix_g, w_in, b_f, lam_re, lam_im, log_dt, b_re, b_im, c_re, c_im,
           d_skip, w_glu, b_glu, attn_out_g, ssm_out_g, w_out, norm_ffn_g, w_router, b_router,
           w_gu, b_gu, w_dn, b_dn, norm_ple_g, w_ple_gate, w_ple_proj, final_g):
    t, d = x2.shape
    nb = t // seq
    aw = ATTN_WIDTH
    w_main = jnp.concatenate([w_in[:, :3 * aw], w_in[:, 3 * aw + N_HEADS:]], axis=1).astype(BF16)
    wft = w_in[:, 3 * aw:3 * aw + N_HEADS].T.astype(BF16)

    a_re, a_im, bbr, bbi = _ssm_prep(lam_re, lam_im, log_dt, b_re, b_im)
    bre = _block_diag(bbr).astype(BF16)
    bim = _block_diag(bbi).astype(BF16)
    cre = _block_diag(jnp.transpose(c_re, (0, 2, 1))).astype(BF16)
    cim = _block_diag(jnp.transpose(c_im, (0, 2, 1))).astype(BF16)

    q, k, v, u, c_t = _in_proj(x2, norm_mix_g, w_main, wft, b_f, seq=seq)
    shp = (nb, seq, aw)
    attn = _attention(q.reshape(shp), k.reshape(shp), v.reshape(shp), c_t)
    ssm = _ssm(u.reshape(nb, seq, SSM_WIDTH), bre, bim, cre, cim, a_re, a_im,
               d_skip, w_glu.astype(BF16), b_glu, ssm_out_g)

    x1, h2, _, gates, lp, cnt = _out_proj(
        x2, attn.reshape(t, aw), ssm.reshape(t, SSM_WIDTH), attn_out_g,
        w_out[:aw].astype(BF16), w_out[aw:].astype(BF16), norm_ffn_g,
        w_router.T.astype(BF16), b_router, tm=MOE_TILE)

    n_chunks, chunk_tab, tail_tab, n_tail, blk_expert, n_used = _moe_tables(
        cnt[:, :, 0], TOP_K * t)
    xs = _dispatch(n_chunks, n_used, n_tail, chunk_tab, tail_tab, lp, h2, blk_expert.shape[0])
    y = _expert_mlp(blk_expert, n_used, xs, w_gu, b_gu, w_dn, b_dn)
    return _moe_combine(n_chunks, chunk_tab, y, lp.T, gates.T, x1, p2, norm_ple_g,
                        w_ple_gate.astype(BF16), w_ple_proj.astype(BF16), final_g)


def kernel(x, p, norm_mix_g, w_in, b_f, lam_re, lam_im, log_dt, b_re, b_im, c_re, c_im, d_skip, w_glu, b_glu, attn_out_g, ssm_out_g, w_out, norm_ffn_g, w_router, b_router, w_gu, b_gu, w_dn, b_dn, norm_ple_g, w_ple_gate, w_ple_proj, norm_final_g):
    bsz, seq, d = x.shape
    depth = w_in.shape[0]
    assert depth == 1, "one layer: the final rmsnorm is fused into the layer's last kernel"
    out = _layer(x.reshape(bsz * seq, d), p[0].reshape(bsz * seq, -1), seq,
                 norm_mix_g[0], w_in[0], b_f[0], lam_re[0], lam_im[0], log_dt[0],
                 b_re[0], b_im[0], c_re[0], c_im[0], d_skip[0], w_glu[0], b_glu[0],
                 attn_out_g[0], ssm_out_g[0], w_out[0], norm_ffn_g[0], w_router[0],
                 b_router[0], w_gu[0], b_gu[0], w_dn[0], b_dn[0], norm_ple_g[0],
                 w_ple_gate[0], w_ple_proj[0], norm_final_g)
    return out.reshape(bsz, seq, d)
```

```python
import functools
import math

import jax
import jax.numpy as jnp
from jax import lax
from jax.experimental import pallas as pl
from jax.experimental.pallas import tpu as pltpu

F32 = jnp.float32
BF16 = jnp.bfloat16
I32 = jnp.int32

NORM_EPS = 1e-5
HEAD_DIM = 64
N_HEADS = 8
ATTN_WIDTH = 512
SSM_WIDTH = 512
SSM_GROUP = 16
N_SSM_GROUPS = 32
SSM_STATE = 64
N_STATE = N_SSM_GROUPS * SSM_STATE
N_EXPERTS = 32
TOP_K = 4
SWIGLU_LIMIT = 7.0
SWIGLU_ALPHA = 1.702
LANES = 128
MOE_ROWS = 1024
NEG_BIG = -1e30
LOG2E = math.log2(math.e)
VMEM_LIMIT = 56 * 1024 * 1024

_NT = (((1,), (1,)), ((), ()))


def _rms(xf, g):
    ms = jnp.mean(xf * xf, axis=-1, keepdims=True)
    return xf * lax.rsqrt(ms + NORM_EPS) * g


def _sigmoid(x):
    return 1.0 / (1.0 + jnp.exp(-x))


def _ssm_prep_kernel(lr_ref, li_ref, ldt_ref, brt_ref, bit_ref,
                     ar_ref, ai_ref, bbr_ref, bbi_ref):
    lr = lr_ref[...]
    li = li_ref[...]
    dt = jnp.exp(ldt_ref[...])
    mag = jnp.exp(lr * dt)
    ab_re = mag * jnp.cos(li * dt)
    ab_im = mag * jnp.sin(li * dt)
    den = lr * lr + li * li
    nr = ab_re - 1.0
    z_re = (nr * lr + ab_im * li) / den
    z_im = (ab_im * lr - nr * li) / den
    ar_ref[...] = ab_re
    ai_ref[...] = ab_im
    br = brt_ref[...]
    bi = bit_ref[...]
    bbr_ref[...] = z_re * br - z_im * bi
    bbi_ref[...] = z_re * bi + z_im * br


def _ssm_prep(lam_re, lam_im, log_dt, b_re, b_im):
    g, p, c = b_re.shape
    brt = jnp.transpose(b_re, (0, 2, 1))
    bit = jnp.transpose(b_im, (0, 2, 1))
    return pl.pallas_call(
        _ssm_prep_kernel,
        out_shape=(jax.ShapeDtypeStruct((g, 1, p), F32), jax.ShapeDtypeStruct((g, 1, p), F32),
                   jax.ShapeDtypeStruct((g, c, p), F32), jax.ShapeDtypeStruct((g, c, p), F32)),
        name="ssm_prep",
    )(lam_re.reshape(g, 1, p), lam_im.reshape(g, 1, p), log_dt.reshape(g, 1, 1), brt, bit)


def _block_diag(w):
    g, a, b = w.shape
    half = g // 2
    eye = jnp.eye(half, dtype=w.dtype)
    w4 = w.reshape(2, half, a, b)
    out = jnp.einsum('hgab,gk->hgakb', w4, eye)
    return out.reshape(2, half * a, half * b)


def _inproj_kernel(x_ref, g_ref, w_ref, wft_ref, bf_ref, tri_ref,
                   q_ref, k_ref, v_ref, u_ref, c_ref, carry_ref, *, tiles_per_seq, tm):
    i = pl.program_id(0)

    @pl.when(i % tiles_per_seq == 0)
    def _():
        carry_ref[...] = jnp.zeros_like(carry_ref)

    h = _rms(x_ref[...], g_ref[...]).astype(BF16)
    proj = jnp.dot(h, w_ref[...], preferred_element_type=F32)
    aw = ATTN_WIDTH
    q_ref[...] = (proj[:, 0:aw] * (LOG2E * HEAD_DIM ** -0.5)).astype(BF16)
    k_ref[...] = proj[:, aw:2 * aw].astype(BF16)
    v_ref[...] = proj[:, 2 * aw:3 * aw].astype(BF16)
    u_ref[...] = proj[:, 3 * aw:3 * aw + SSM_WIDTH].astype(BF16)

    fl = lax.dot_general(wft_ref[...], h, _NT, preferred_element_type=F32)
    z = fl + bf_ref[...]
    lf = jnp.minimum(z, 0.0) - jnp.log1p(jnp.exp(-jnp.abs(z)))
    hi = lf.astype(BF16)
    lo = (lf - hi.astype(F32)).astype(BF16)
    tri = tri_ref[...]
    cs = (jnp.dot(hi, tri, preferred_element_type=F32)
          + jnp.dot(lo, tri, preferred_element_type=F32))
    c = cs + carry_ref[:, 0:1]
    c_ref[...] = c * LOG2E
    carry_ref[...] = jnp.broadcast_to(c[:, tm - 1:tm], carry_ref.shape)


def _in_proj(x2, norm_g, w_main, wft, b_f, *, seq, tm=1024):
    t, d = x2.shape
    n_main = w_main.shape[1]
    tri = jnp.triu(jnp.ones((tm, tm), F32)).astype(BF16)
    kern = functools.partial(_inproj_kernel, tiles_per_seq=seq // tm, tm=tm)
    row = lambda i: (i, 0)
    const = lambda i: (0, 0)
    act = jax.ShapeDtypeStruct((t, ATTN_WIDTH), BF16)
    return pl.pallas_call(
        kern,
        grid=(t // tm,),
        in_specs=[pl.BlockSpec((tm, d), row),
                  pl.BlockSpec((1, d), const),
                  pl.BlockSpec((d, n_main), const),
                  pl.BlockSpec((N_HEADS, d), const),
                  pl.BlockSpec((N_HEADS, 1), const),
                  pl.BlockSpec((tm, tm), const)],
        out_specs=[pl.BlockSpec((tm, ATTN_WIDTH), row)] * 4
        + [pl.BlockSpec((N_HEADS, tm), lambda i: (0, i))],
        out_shape=[act, act, act, act, jax.ShapeDtypeStruct((N_HEADS, t), F32)],
        scratch_shapes=[pltpu.VMEM((N_HEADS, LANES), F32)],
        compiler_params=pltpu.CompilerParams(
            dimension_semantics=("arbitrary",), vmem_limit_bytes=VMEM_LIMIT),
        name="in_proj",
    )(x2, norm_g.reshape(1, d), w_main, wft, b_f.reshape(N_HEADS, 1), tri)


def _attn_kernel(q_ref, k_ref, v_ref, c_ref, o_ref, *, tq, n_pairs):
    i = pl.program_id(2)
    lane = lax.broadcasted_iota(I32, (1, LANES), 1)
    first = lane < HEAD_DIM
    n_heads = 2 * n_pairs
    q_heads = []
    for p in range(n_pairs):
        q2 = q_ref[:, p * LANES:(p + 1) * LANES]
        zero = jnp.zeros_like(q2)
        q_heads += [jnp.where(first, q2, zero), jnp.where(first, zero, q2)]
    half = tq // 2

    def block(off, width, r0, carry, masked):
        cj = c_ref[:, pl.ds(off, width)]
        out = []
        for p in range(n_pairs):
            kj = k_ref[pl.ds(off, width), p * LANES:(p + 1) * LANES]
            vj = v_ref[pl.ds(off, width), p * LANES:(p + 1) * LANES]
            one = jnp.ones_like(vj)
            v_heads = (jnp.where(first, vj, one), jnp.where(first, one, vj))
            for h in range(2):
                m, acc = carry[2 * p + h]
                s = (lax.dot_general(q_heads[2 * p + h][r0:], kj, _NT,
                                     preferred_element_type=F32)
                     - cj[2 * p + h:2 * p + h + 1, :])
                if masked:
                    rr = lax.broadcasted_iota(I32, s.shape, 0)
                    cc = lax.broadcasted_iota(I32, s.shape, 1)
                    s = jnp.where(cc <= rr, s, NEG_BIG)
                m_new = jnp.maximum(m[r0:], jnp.max(s, axis=-1, keepdims=True))
                alpha = jnp.exp2(m[r0:] - m_new)
                pr = jnp.exp2(s - m_new).astype(BF16)
                acc_new = alpha * acc[r0:] + jnp.dot(pr, v_heads[h], preferred_element_type=F32)
                if r0:
                    m_new = jnp.concatenate([m[:r0], m_new], axis=0)
                    acc_new = jnp.concatenate([acc[:r0], acc_new], axis=0)
                out.append((m_new, acc_new))
        return tuple(out)

    init_one = (jnp.full((tq, 1), NEG_BIG, F32), jnp.zeros((tq, LANES), F32))
    carry = lax.fori_loop(
        0, i, lambda j, c: block(pl.multiple_of(j * tq, tq), tq, 0, c, False),
        (init_one,) * n_heads)
    diag = pl.multiple_of(i * tq, tq)
    for n in range(tq // half):
        carry = block(pl.multiple_of(diag + n * half, half), half, n * half, carry, True)
    outs = []
    for p in range(n_pairs):
        acc_a = carry[2 * p][1]
        acc_b = carry[2 * p + 1][1]
        outs.append(jnp.where(first, acc_a / pltpu.roll(acc_a, HEAD_DIM, axis=1),
                              acc_b / pltpu.roll(acc_b, HEAD_DIM, axis=1)))
    o_ref[...] = jnp.concatenate(outs, axis=-1).astype(BF16)


def _attention(q, k, v, c_t, *, tq=1024, n_pairs=2):
    b, s, w = q.shape
    wb = n_pairs * LANES
    n_groups = w // wb
    c3 = c_t.reshape(n_groups, 2 * n_pairs, b * s)
    kern = functools.partial(_attn_kernel, tq=tq, n_pairs=n_pairs)
    return pl.pallas_call(
        kern,
        grid=(b, n_groups, s // tq),
        in_specs=[pl.BlockSpec((None, tq, wb), lambda bi, g, i: (bi, i, g)),
                  pl.BlockSpec((None, s, wb), lambda bi, g, i: (bi, 0, g)),
                  pl.BlockSpec((None, s, wb), lambda bi, g, i: (bi, 0, g)),
                  pl.BlockSpec((None, 2 * n_pairs, s), lambda bi, g, i: (g, 0, bi))],
        out_specs=pl.BlockSpec((None, tq, wb), lambda bi, g, i: (bi, i, g)),
        out_shape=jax.ShapeDtypeStruct((b, s, w), BF16),
        compiler_params=pltpu.CompilerParams(
            dimension_semantics=("arbitrary", "arbitrary", "arbitrary"),
            vmem_limit_bytes=VMEM_LIMIT),
        name="attention",
    )(q, k, v, c3)


def _ssm_kernel(u_ref, bre_ref, bim_ref, cre_ref, cim_ref, ar_ref, ai_ref, dsk_ref,
                wglu_ref, bglu_ref, g_ref, o_ref,
                us_ref, xr_ref, xi_ref, str_ref, sti_ref, res_ref, *, tt, nb):
    i = pl.program_id(0)

    @pl.when(i == 0)
    def _():
        str_ref[...] = jnp.zeros_like(str_ref)
        sti_ref[...] = jnp.zeros_like(sti_ref)

    n_chunks = SSM_WIDTH // LANES
    for b in range(nb):
        ub32 = u_ref[b].astype(F32)
        for c in range(n_chunks):
            us_ref[c, pl.ds(b, tt, stride=nb), :] = ub32[:, c * LANES:(c + 1) * LANES]
    uf = jnp.concatenate([us_ref[c] for c in range(n_chunks)], axis=-1)
    ub = uf.astype(BF16)
    half_in = SSM_WIDTH // 2
    half_st = N_STATE // 2
    for hf in range(2):
        uh = ub[:, hf * half_in:(hf + 1) * half_in]
        xr_ref[:, hf * half_st:(hf + 1) * half_st] = jnp.dot(
            uh, bre_ref[hf], preferred_element_type=F32)
        xi_ref[:, hf * half_st:(hf + 1) * half_st] = jnp.dot(
            uh, bim_ref[hf], preferred_element_type=F32)

    n_col_groups = 2
    wcol = N_STATE // n_col_groups
    unroll = 4
    for cg in range(n_col_groups):
        cols = slice(cg * wcol, (cg + 1) * wcol)
        ar = jnp.broadcast_to(ar_ref[:, cols], (nb, wcol))
        ai = jnp.broadcast_to(ai_ref[:, cols], (nb, wcol))

        def steps(tb, carry, cols=cols, ar=ar, ai=ai):
            sr, si = carry
            for k in range(unroll):
                r0 = pl.multiple_of((tb * unroll + k) * nb, nb)
                br = xr_ref[pl.ds(r0, nb), cols]
                bi = xi_ref[pl.ds(r0, nb), cols]
                nr = ar * sr - ai * si + br
                ni = ar * si + ai * sr + bi
                xr_ref[pl.ds(r0, nb), cols] = nr
                xi_ref[pl.ds(r0, nb), cols] = ni
                sr, si = nr, ni
            return sr, si

        sr, si = lax.fori_loop(0, tt // unroll, steps, (str_ref[:, cols], sti_ref[:, cols]))
        str_ref[:, cols] = sr
        sti_ref[:, cols] = si

    ys = []
    for hf in range(2):
        xr = xr_ref[:, hf * half_st:(hf + 1) * half_st].astype(BF16)
        xi = xi_ref[:, hf * half_st:(hf + 1) * half_st].astype(BF16)
        ys.append(jnp.dot(xr, cre_ref[hf], preferred_element_type=F32)
                  - jnp.dot(xi, cim_ref[hf], preferred_element_type=F32))
    y = jnp.concatenate(ys, axis=-1) + dsk_ref[...] * uf
    gl = 0.5 * y * (1.0 + jnp.tanh(math.sqrt(2.0 / math.pi) * (y + 0.044715 * (y * y * y))))
    zz = jnp.dot(gl.astype(BF16), wglu_ref[...], preferred_element_type=F32) + bglu_ref[...]
    out = gl * _sigmoid(zz)
    res = _rms(out, g_ref[...])
    for c in range(n_chunks):
        res_ref[c] = res[:, c * LANES:(c + 1) * LANES]
    for b in range(nb):
        o_ref[b] = jnp.concatenate(
            [res_ref[c, pl.ds(b, tt, stride=nb), :] for c in range(n_chunks)],
            axis=-1).astype(BF16)


def _ssm(u3, bre, bim, cre, cim, a_re, a_im, d_skip, w_glu, b_glu, out_g, *, tt=128):
    nb, s, w = u3.shape
    rows = tt * nb
    kern = functools.partial(_ssm_kernel, tt=tt, nb=nb)
    c3 = lambda i: (0, 0, 0)
    c2 = lambda i: (0, 0)
    return pl.pallas_call(
        kern,
        grid=(s // tt,),
        in_specs=[pl.BlockSpec((nb, tt, w), lambda i: (0, i, 0)),
                  pl.BlockSpec(bre.shape, c3), pl.BlockSpec(bim.shape, c3),
                  pl.BlockSpec(cre.shape, c3), pl.BlockSpec(cim.shape, c3),
                  pl.BlockSpec((1, N_STATE), c2), pl.BlockSpec((1, N_STATE), c2),
                  pl.BlockSpec((1, w), c2),
                  pl.BlockSpec((w, w), c2), pl.BlockSpec((1, w), c2), pl.BlockSpec((1, w), c2)],
        out_specs=pl.BlockSpec((nb, tt, w), lambda i: (0, i, 0)),
        out_shape=jax.ShapeDtypeStruct((nb, s, w), BF16),
        scratch_shapes=[pltpu.VMEM((w // LANES, rows, LANES), F32),
                        pltpu.VMEM((rows, N_STATE), F32), pltpu.VMEM((rows, N_STATE), F32),
                        pltpu.VMEM((nb, N_STATE), F32), pltpu.VMEM((nb, N_STATE), F32),
                        pltpu.VMEM((w // LANES, rows, LANES), F32)],
        compiler_params=pltpu.CompilerParams(
            dimension_semantics=("arbitrary",), vmem_limit_bytes=VMEM_LIMIT),
        name="ssm",
    )(u3, bre, bim, cre, cim, a_re.reshape(1, N_STATE), a_im.reshape(1, N_STATE),
      d_skip.reshape(1, w), w_glu, b_glu.reshape(1, w), out_g.reshape(1, w))


def _outproj_kernel(x_ref, a_ref, s_ref, ga_ref, woa_ref, wos_ref, gf_ref, wrt_ref, br_ref,
                    tri_ref, trie_ref, x1_ref, h2_ref, topi_ref, gate_ref, rank_ref, cnt_ref,
                    *, tm):
    a = _rms(a_ref[...].astype(F32), ga_ref[...]).astype(BF16)
    x1 = (x_ref[...] + jnp.dot(a, woa_ref[...], preferred_element_type=F32)
          + jnp.dot(s_ref[...], wos_ref[...], preferred_element_type=F32))
    x1_ref[...] = x1
    h2 = _rms(x1, gf_ref[...]).astype(BF16)
    h2_ref[...] = h2

    lg = lax.dot_general(wrt_ref[...], h2, _NT,
                         preferred_element_type=F32) + br_ref[...]
    ids = lax.broadcasted_iota(I32, (N_EXPERTS, tm), 0)
    work = lg
    vals, idxs = [], []
    for _ in range(TOP_K):
        m = jnp.max(work, axis=0, keepdims=True)
        idx = jnp.min(jnp.where(work == m, ids, N_EXPERTS), axis=0, keepdims=True)
        vals.append(m)
        idxs.append(idx)
        work = jnp.where(ids == idx, -jnp.inf, work)
    exps = [jnp.exp(v - vals[0]) for v in vals]
    den = exps[0] + exps[1] + exps[2] + exps[3]
    gate_ref[...] = jnp.concatenate([e / den for e in exps], axis=0)
    topi_ref[...] = jnp.concatenate(idxs, axis=0)

    sel = jnp.zeros((N_EXPERTS, tm), F32)
    for idx in idxs:
        sel = sel + jnp.where(ids == idx, 1.0, 0.0)
    cnt = jnp.broadcast_to(jnp.sum(sel, axis=1, keepdims=True), (N_EXPERTS, LANES))
    padded = jnp.floor((cnt + (SEG_ALIGN - 1)) * (1.0 / SEG_ALIGN)) * SEG_ALIGN
    seg_start = jnp.dot(trie_ref[...], padded.astype(BF16), preferred_element_type=F32)
    before = (jnp.dot(sel.astype(BF16), tri_ref[...], preferred_element_type=F32)
              + seg_start[:, 0:1])
    ranks = [jnp.sum(jnp.where(ids == idx, before, 0.0), axis=0, keepdims=True) for idx in idxs]
    rank_ref[...] = jnp.concatenate(ranks, axis=0).astype(I32)
    cnt_ref[0] = cnt


def _out_proj(x2, attn, ssm, attn_g, wo_a, wo_s, ffn_g, wrt, b_router, *, tm=512):
    t, d = x2.shape
    tri = jnp.triu(jnp.ones((tm, tm), F32), k=1).astype(BF16)
    tri_e = jnp.tril(jnp.ones((N_EXPERTS, N_EXPERTS), F32), k=-1).astype(BF16)
    kern = functools.partial(_outproj_kernel, tm=tm)
    row = lambda i: (i, 0)
    const = lambda i: (0, 0)
    colblk = lambda i: (0, i)
    return pl.pallas_call(
        kern,
        grid=(t // tm,),
        in_specs=[pl.BlockSpec((tm, d), row),
                  pl.BlockSpec((tm, ATTN_WIDTH), row),
                  pl.BlockSpec((tm, SSM_WIDTH), row),
                  pl.BlockSpec((1, ATTN_WIDTH), const),
                  pl.BlockSpec((ATTN_WIDTH, d), const),
                  pl.BlockSpec((SSM_WIDTH, d), const),
                  pl.BlockSpec((1, d), const),
                  pl.BlockSpec((N_EXPERTS, d), const),
                  pl.BlockSpec((N_EXPERTS, 1), const),
                  pl.BlockSpec((tm, tm), const),
                  pl.BlockSpec((N_EXPERTS, N_EXPERTS), const)],
        out_specs=[pl.BlockSpec((tm, d), row),
                   pl.BlockSpec((tm, d), row),
                   pl.BlockSpec((TOP_K, tm), colblk),
                   pl.BlockSpec((TOP_K, tm), colblk),
                   pl.BlockSpec((TOP_K, tm), colblk),
                   pl.BlockSpec((1, N_EXPERTS, LANES), lambda i: (i, 0, 0))],
        out_shape=[jax.ShapeDtypeStruct((t, d), F32),
                   jax.ShapeDtypeStruct((t, d), BF16),
                   jax.ShapeDtypeStruct((TOP_K, t), I32),
                   jax.ShapeDtypeStruct((TOP_K, t), F32),
                   jax.ShapeDtypeStruct((TOP_K, t), I32),
                   jax.ShapeDtypeStruct((t // tm, N_EXPERTS, LANES), F32)],
        compiler_params=pltpu.CompilerParams(
            dimension_semantics=("arbitrary",), vmem_limit_bytes=VMEM_LIMIT),
        name="out_proj",
    )(x2, attn, ssm, attn_g.reshape(1, -1), wo_a, wo_s, ffn_g.reshape(1, d), wrt,
      b_router.reshape(N_EXPERTS, 1), tri, tri_e)


SEG_ALIGN = 16
MOE_TILE = 512
LOCAL_ROWS = TOP_K * MOE_TILE + N_EXPERTS * SEG_ALIGN
CHUNK_TABLE = 256
TAIL_TABLE = N_EXPERTS * (MOE_ROWS // SEG_ALIGN)
START_GROUP = 4
MIN_CHUNKS = TOP_K * MOE_TILE // SEG_ALIGN
WAIT_GROUP = 8


def _chunk_copy(src, dst, sem):
    return pltpu.make_async_copy(src, dst, sem)


def _chunk_wait_n(local_buf, hbm, sem, slot, n, to_hbm):
    def wait_rows(rows):
        loc = local_buf.at[slot, pl.ds(0, rows), :]
        far = hbm.at[pl.ds(0, rows), :]

        def body(j, carry):
            (_chunk_copy(loc, far, sem.at[slot]) if to_hbm
             else _chunk_copy(far, loc, sem.at[slot])).wait()
            return carry
        return body

    n_groups = n // WAIT_GROUP
    lax.fori_loop(0, n_groups, wait_rows(WAIT_GROUP * SEG_ALIGN), 0)
    lax.fori_loop(n_groups * WAIT_GROUP, n, wait_rows(SEG_ALIGN), 0)


def _chunk_start(local_buf, hbm, tab_ref, sem, slot, j, to_hbm, priority):
    loc = local_buf.at[slot, pl.ds(pl.multiple_of(j * SEG_ALIGN, SEG_ALIGN), SEG_ALIGN), :]
    far = hbm.at[pl.ds(pl.multiple_of(tab_ref[0, 0, j] * SEG_ALIGN, SEG_ALIGN), SEG_ALIGN), :]
    (_chunk_copy(loc, far, sem.at[slot]) if to_hbm
     else _chunk_copy(far, loc, sem.at[slot])).start(priority=priority)


def _chunk_start_n(local_buf, hbm, tab_ref, sem, slot, n, to_hbm, first=0):
    def group(g, carry):
        for u in range(START_GROUP):
            _chunk_start(local_buf, hbm, tab_ref, sem, slot, g * START_GROUP + u, to_hbm, u % 2)
        return carry

    def single(j, carry):
        _chunk_start(local_buf, hbm, tab_ref, sem, slot, j, to_hbm, 0)
        return carry

    n_groups = n // START_GROUP
    lax.fori_loop(first // START_GROUP, n_groups, group, 0)
    lax.fori_loop(n_groups * START_GROUP, n, single, 0)


def _dispatch_kernel(nch_ref, nused_ref, ntail_ref, tab_ref, tail_ref, lp_ref, h2_ref, xs_hbm,
                     buf, zbuf, sem, zsem, *, n_tiles, n_blocks):
    i = pl.program_id(0)
    slot = i % 2

    @pl.when(i >= 2)
    def _():
        _chunk_wait_n(buf, xs_hbm, sem, slot, nch_ref[jnp.maximum(i - 2, 0)], True)

    lp = lp_ref[...]
    tm = lp.shape[1]

    rows = lax.broadcasted_iota(I32, (LOCAL_ROWS, tm), 0)
    onehot = jnp.zeros((LOCAL_ROWS, tm), F32)
    for k in range(TOP_K):
        onehot = jnp.where(rows == lp[k:k + 1, :], 1.0, onehot)
    buf[slot] = jnp.dot(onehot.astype(BF16), h2_ref[...],
                        preferred_element_type=F32).astype(BF16)
    _chunk_start_n(buf, xs_hbm, tab_ref, sem, slot, nch_ref[i], True)

    @pl.when(i == n_tiles - 1)
    def _():
        zbuf[...] = jnp.zeros_like(zbuf)
        zrow = zbuf.at[pl.ds(0, SEG_ALIGN), :]

        def fill(j, carry):
            row0 = pl.multiple_of(tail_ref[0, 0, j] * SEG_ALIGN, SEG_ALIGN)
            _chunk_copy(zrow, xs_hbm.at[pl.ds(row0, SEG_ALIGN), :], zsem).start()
            return carry

        def drain(j, carry):
            _chunk_copy(zrow, xs_hbm.at[pl.ds(0, SEG_ALIGN), :], zsem).wait()
            return carry

        def fill_block(b, carry):
            row0 = pl.multiple_of(b * MOE_ROWS, MOE_ROWS)
            _chunk_copy(zbuf, xs_hbm.at[pl.ds(row0, MOE_ROWS), :], zsem).start()
            return carry

        def drain_block(b, carry):
            _chunk_copy(zbuf, xs_hbm.at[pl.ds(0, MOE_ROWS), :], zsem).wait()
            return carry

        lax.fori_loop(0, ntail_ref[0], fill, 0)
        lax.fori_loop(0, ntail_ref[0], drain, 0)
        lax.fori_loop(nused_ref[0], n_blocks, fill_block, 0)
        lax.fori_loop(nused_ref[0], n_blocks, drain_block, 0)
        _chunk_wait_n(buf, xs_hbm, sem, slot, nch_ref[i], True)
        if n_tiles > 1:
            _chunk_wait_n(buf, xs_hbm, sem, 1 - slot, nch_ref[jnp.maximum(i - 1, 0)], True)


def _dispatch(n_chunks, n_used, n_tail, chunk_tab, tail_tab, lp, h2, n_blocks):
    k, t = lp.shape
    d = h2.shape[1]
    n_tiles = t // MOE_TILE
    grid_spec = pltpu.PrefetchScalarGridSpec(
        num_scalar_prefetch=3,
        grid=(n_tiles,),
        in_specs=[pl.BlockSpec((1, 1, CHUNK_TABLE), lambda i, nc, nu, nt: (i, 0, 0),
                               memory_space=pltpu.SMEM),
                  pl.BlockSpec((1, 1, TAIL_TABLE), lambda i, nc, nu, nt: (0, 0, 0),
                               memory_space=pltpu.SMEM),
                  pl.BlockSpec((k, MOE_TILE), lambda i, nc, nu, nt: (0, i)),
                  pl.BlockSpec((MOE_TILE, d), lambda i, nc, nu, nt: (i, 0))],
        out_specs=pl.BlockSpec(memory_space=pl.ANY),
        scratch_shapes=[pltpu.VMEM((2, LOCAL_ROWS, d), BF16), pltpu.VMEM((MOE_ROWS, d), BF16),
                        pltpu.SemaphoreType.DMA((2,)), pltpu.SemaphoreType.DMA],
    )
    return pl.pallas_call(
        functools.partial(_dispatch_kernel, n_tiles=n_tiles, n_blocks=n_blocks),
        grid_spec=grid_spec,
        out_shape=jax.ShapeDtypeStruct((n_blocks * MOE_ROWS, d), BF16),
        compiler_params=pltpu.CompilerParams(
            dimension_semantics=("arbitrary",), vmem_limit_bytes=VMEM_LIMIT),
        name="dispatch",
    )(n_chunks, n_used, n_tail, chunk_tab, tail_tab, lp, h2)


def _expert_mlp_kernel(blk_e_ref, nused_ref, x_ref, wgu_ref, bgu_ref, wdn_ref, bdn_ref, y_ref,
                       wgu_bf, wdn_bf):
    i = pl.program_id(0)
    nused = nused_ref[0]
    new_expert = jnp.logical_or(i == 0, blk_e_ref[i] != blk_e_ref[jnp.maximum(i - 1, 0)])

    @pl.when(jnp.logical_and(new_expert, i < nused))
    def _():
        wgu_bf[...] = wgu_ref[...].astype(BF16)
        wdn_bf[...] = wdn_ref[...].astype(BF16)

    @pl.when(i < nused)
    def _():
        x = x_ref[...]
        gu = jnp.dot(x, wgu_bf[...], preferred_element_type=F32) + bgu_ref[...]
        d_ff = gu.shape[1] // 2
        gate = jnp.minimum(gu[:, :d_ff], SWIGLU_LIMIT)
        up = jnp.clip(gu[:, d_ff:], -SWIGLU_LIMIT, SWIGLU_LIMIT)
        act = gate * _sigmoid(SWIGLU_ALPHA * gate) * (up + 1.0)
        y = jnp.dot(act.astype(BF16), wdn_bf[...], preferred_element_type=F32) + bdn_ref[...]
        y_ref[...] = y.astype(BF16)

    @pl.when(i >= nused)
    def _():
        y_ref[...] = jnp.zeros_like(y_ref)


def _expert_mlp(blk_expert, n_used, xs, w_gu, b_gu, w_dn, b_dn):
    d = xs.shape[1]
    n_blocks = blk_expert.shape[0]
    n_rows = n_blocks * MOE_ROWS
    e, _, f2 = w_gu.shape
    grid_spec = pltpu.PrefetchScalarGridSpec(
        num_scalar_prefetch=2,
        grid=(n_blocks,),
        in_specs=[
            pl.BlockSpec((MOE_ROWS, d), lambda i, be, nu: (jnp.minimum(i, nu[0] - 1), 0)),
            pl.BlockSpec((None, d, f2), lambda i, be, nu: (be[i], 0, 0)),
            pl.BlockSpec((None, 1, f2), lambda i, be, nu: (be[i], 0, 0)),
            pl.BlockSpec((None, f2 // 2, d), lambda i, be, nu: (be[i], 0, 0)),
            pl.BlockSpec((None, 1, d), lambda i, be, nu: (be[i], 0, 0)),
        ],
        out_specs=pl.BlockSpec((MOE_ROWS, d), lambda i, be, nu: (i, 0)),
        scratch_shapes=[pltpu.VMEM((d, f2), BF16), pltpu.VMEM((f2 // 2, d), BF16)],
    )
    return pl.pallas_call(
        _expert_mlp_kernel,
        grid_spec=grid_spec,
        out_shape=jax.ShapeDtypeStruct((n_rows, d), BF16),
        compiler_params=pltpu.CompilerParams(
            dimension_semantics=("arbitrary",), vmem_limit_bytes=VMEM_LIMIT),
        name="experts",
    )(blk_expert, n_used, xs, w_gu, b_gu.reshape(e, 1, f2), w_dn, b_dn.reshape(e, 1, d))


def _moe_combine_kernel(nch_ref, tab_cur_ref, tab_nxt_ref, y_hbm, lpt_ref, gate_ref, x1_ref,
                        p_ref, gp_ref, wg_ref, wp_ref, gfin_ref, o_ref, buf, sem, *, n_tiles):
    i = pl.program_id(0)
    slot = i % 2

    @pl.when(i == 0)
    def _():
        buf[...] = jnp.zeros_like(buf)
        _chunk_start_n(buf, y_hbm, tab_cur_ref, sem, 0, nch_ref[0], False)

    @pl.when(i + 1 < n_tiles)
    def _():
        _chunk_start_n(buf, y_hbm, tab_nxt_ref, sem, 1 - slot,
                       nch_ref[jnp.minimum(i + 1, n_tiles - 1)], False, first=MIN_CHUNKS)

    _chunk_wait_n(buf, y_hbm, sem, slot, nch_ref[i], False)
    lpt = lpt_ref[...]
    gates = gate_ref[...]
    tm = lpt.shape[0]
    cols = lax.broadcasted_iota(I32, (tm, LOCAL_ROWS), 1)
    weights = jnp.zeros((tm, LOCAL_ROWS), F32)
    for k in range(TOP_K):
        weights = jnp.where(cols == lpt[:, k:k + 1], gates[:, k:k + 1], weights)
    weights = weights.astype(BF16)
    n_col = 4
    wc = buf.shape[2] // n_col
    per = MIN_CHUNKS // n_col
    moe = []
    for c in range(n_col):
        for j in range(c * per, (c + 1) * per):
            _chunk_start(buf, y_hbm, tab_nxt_ref, sem, 1 - slot, j, False, j % 2)
        moe.append(jnp.dot(weights, buf[slot, :, c * wc:(c + 1) * wc],
                           preferred_element_type=F32))
    x2 = x1_ref[...] + jnp.concatenate(moe, axis=-1)
    hg = _rms(x2, gp_ref[...]).astype(BF16)
    gate = _sigmoid(jnp.dot(hg, wg_ref[...], preferred_element_type=F32))
    emb = jnp.dot(p_ref[...].astype(BF16), wp_ref[...], preferred_element_type=F32)
    x3 = x2 + gate * emb
    o_ref[...] = _rms(x3, gfin_ref[...])

    @pl.when(i == n_tiles - 1)
    def _():
        _chunk_wait_n(buf, y_hbm, sem, 1 - slot, MIN_CHUNKS, False)


def _moe_combine(n_chunks, chunk_tab, y, lpt, gates_tk, x1, p2, ple_g, w_gate, w_proj, fin_g):
    t, d = x1.shape
    n_tiles = t // MOE_TILE
    last = n_tiles - 1
    ple = p2.shape[1]
    row = lambda i, nc: (i, 0)
    const = lambda i, nc: (0, 0)
    grid_spec = pltpu.PrefetchScalarGridSpec(
        num_scalar_prefetch=1,
        grid=(n_tiles,),
        in_specs=[pl.BlockSpec((1, 1, CHUNK_TABLE), lambda i, nc: (i, 0, 0),
                               memory_space=pltpu.SMEM),
                  pl.BlockSpec((1, 1, CHUNK_TABLE),
                               lambda i, nc: (jnp.minimum(i + 1, last), 0, 0),
                               memory_space=pltpu.SMEM),
                  pl.BlockSpec(memory_space=pl.ANY),
                  pl.BlockSpec((MOE_TILE, TOP_K), row),
                  pl.BlockSpec((MOE_TILE, TOP_K), row),
                  pl.BlockSpec((MOE_TILE, d), row),
                  pl.BlockSpec((MOE_TILE, ple), row),
                  pl.BlockSpec((1, d), const),
                  pl.BlockSpec((d, d), const),
                  pl.BlockSpec((ple, d), const),
                  pl.BlockSpec((1, d), const)],
        out_specs=pl.BlockSpec((MOE_TILE, d), row),
        scratch_shapes=[pltpu.VMEM((2, LOCAL_ROWS, d), BF16), pltpu.SemaphoreType.DMA((2,))],
    )
    return pl.pallas_call(
        functools.partial(_moe_combine_kernel, n_tiles=n_tiles),
        grid_spec=grid_spec,
        out_shape=jax.ShapeDtypeStruct((t, d), F32),
        compiler_params=pltpu.CompilerParams(
            dimension_semantics=("arbitrary",), vmem_limit_bytes=VMEM_LIMIT),
        name="combine",
    )(n_chunks, chunk_tab, chunk_tab, y, lpt, gates_tk, x1, p2, ple_g.reshape(1, d),
      w_gate, w_proj, fin_g.reshape(1, d))


def _moe_tables(cnt, n_assign):
    n_tiles = cnt.shape[0]
    cnt = cnt.astype(I32)
    pc = (cnt + SEG_ALIGN - 1) // SEG_ALIGN * SEG_ALIGN
    l_end = jnp.cumsum(pc, axis=1)
    l_start = l_end - pc
    tot = jnp.sum(pc, axis=0)
    e_pad = (tot + MOE_ROWS - 1) // MOE_ROWS * MOE_ROWS
    e_end = jnp.cumsum(e_pad)
    seg_row0 = (e_end - e_pad)[None, :] + jnp.cumsum(pc, axis=0) - pc
    j16 = jnp.arange(CHUNK_TABLE, dtype=I32)[None, :] * SEG_ALIGN
    ce = jnp.minimum(jnp.sum((l_end[:, :, None] <= j16[:, None, :]).astype(I32), axis=1),
                     N_EXPERTS - 1)
    pick = ce[:, None, :] == jnp.arange(N_EXPERTS, dtype=I32)[None, :, None]
    seg0 = jnp.sum(jnp.where(pick, seg_row0[:, :, None], 0), axis=1)
    loc0 = jnp.sum(jnp.where(pick, l_start[:, :, None], 0), axis=1)
    n_chunks = l_end[:, -1] // SEG_ALIGN
    valid = jnp.arange(CHUNK_TABLE, dtype=I32)[None, :] < n_chunks[:, None]
    tab = jnp.where(valid, (seg0 + j16 - loc0) // SEG_ALIGN, 0).astype(I32)
    n_rows_max = n_assign + n_tiles * N_EXPERTS * (SEG_ALIGN - 1) + N_EXPERTS * (MOE_ROWS - 1)
    n_blocks = (n_rows_max + MOE_ROWS - 1) // MOE_ROWS
    n_used = (e_end[-1] // MOE_ROWS).astype(I32).reshape(1)
    blk_row0 = jnp.arange(n_blocks, dtype=I32) * MOE_ROWS
    blk_expert = jnp.minimum(
        jnp.sum((e_end[None, :] <= blk_row0[:, None]).astype(I32), axis=1), N_EXPERTS - 1)
    t_len = (e_pad - tot) // SEG_ALIGN
    t_end = jnp.cumsum(t_len)
    t_first = (e_end - e_pad + tot) // SEG_ALIGN
    f = jnp.arange(TAIL_TABLE, dtype=I32)
    te = jnp.minimum(jnp.sum((t_end[None, :] <= f[:, None]).astype(I32), axis=1),
                     N_EXPERTS - 1)
    te_pick = te[:, None] == jnp.arange(N_EXPERTS, dtype=I32)[None, :]
    tail = (jnp.sum(jnp.where(te_pick, (t_first - t_end + t_len)[None, :], 0), axis=1) + f)
    tail = jnp.where(f < t_end[-1], tail, 0).astype(I32)
    n_tail = t_end[-1].astype(I32).reshape(1)
    return (n_chunks.astype(I32), tab.reshape(n_tiles, 1, CHUNK_TABLE),
            tail.reshape(1, 1, TAIL_TABLE), n_tail, blk_expert, n_used)


def _layer(x2, p2, seq, norm_mix_g, w_in, b_f, lam_re, lam_im, log_dt, b_re, b_im, c_re, c_im,
           d_skip, w_glu, b_glu, attn_out_g, ssm_out_g, w_out, norm_ffn_g, w_router, b_router,
           w_gu, b_gu, w_dn, b_dn, norm_ple_g, w_ple_gate, w_ple_proj, final_g):
    t, d = x2.shape
    nb = t // seq
    aw = ATTN_WIDTH
    w_main = jnp.concatenate([w_in[:, :3 * aw], w_in[:, 3 * aw + N_HEADS:]], axis=1).astype(BF16)
    wft = w_in[:, 3 * aw:3 * aw + N_HEADS].T.astype(BF16)

    a_re, a_im, bbr, bbi = _ssm_prep(lam_re, lam_im, log_dt, b_re, b_im)
    bre = _block_diag(bbr).astype(BF16)
    bim = _block_diag(bbi).astype(BF16)
    cre = _block_diag(jnp.transpose(c_re, (0, 2, 1))).astype(BF16)
    cim = _block_diag(jnp.transpose(c_im, (0, 2, 1))).astype(BF16)

    q, k, v, u, c_t = _in_proj(x2, norm_mix_g, w_main, wft, b_f, seq=seq)
    shp = (nb, seq, aw)
    attn = _attention(q.reshape(shp), k.reshape(shp), v.reshape(shp), c_t)
    ssm = _ssm(u.reshape(nb, seq, SSM_WIDTH), bre, bim, cre, cim, a_re, a_im,
               d_skip, w_glu.astype(BF16), b_glu, ssm_out_g)

    x1, h2, _, gates, lp, cnt = _out_proj(
        x2, attn.reshape(t, aw), ssm.reshape(t, SSM_WIDTH), attn_out_g,
        w_out[:aw].astype(BF16), w_out[aw:].astype(BF16), norm_ffn_g,
        w_router.T.astype(BF16), b_router, tm=MOE_TILE)

    n_chunks, chunk_tab, tail_tab, n_tail, blk_expert, n_used = _moe_tables(
        cnt[:, :, 0], TOP_K * t)
    xs = _dispatch(n_chunks, n_used, n_tail, chunk_tab, tail_tab, lp, h2, blk_expert.shape[0])
    y = _expert_mlp(blk_expert, n_used, xs, w_gu, b_gu, w_dn, b_dn)
    return _moe_combine(n_chunks, chunk_tab, y, lp.T, gates.T, x1, p2, norm_ple_g,
                        w_ple_gate.astype(BF16), w_ple_proj.astype(BF16), final_g)


def kernel(x, p, norm_mix_g, w_in, b_f, lam_re, lam_im, log_dt, b_re, b_im, c_re, c_im, d_skip, w_glu, b_glu, attn_out_g, ssm_out_g, w_out, norm_ffn_g, w_router, b_router, w_gu, b_gu, w_dn, b_dn, norm_ple_g, w_ple_gate, w_ple_proj, norm_final_g):
    bsz, seq, d = x.shape
    depth = w_in.shape[0]
    assert depth == 1, "one layer: the final rmsnorm is fused into the layer's last kernel"
    out = _layer(x.reshape(bsz * seq, d), p[0].reshape(bsz * seq, -1), seq,
                 norm_mix_g[0], w_in[0], b_f[0], lam_re[0], lam_im[0], log_dt[0],
                 b_re[0], b_im[0], c_re[0], c_im[0], d_skip[0], w_glu[0], b_glu[0],
                 attn_out_g[0], ssm_out_g[0], w_out[0], norm_ffn_g[0], w_router[0],
                 b_router[0], w_gu[0], b_gu[0], w_dn[0], b_dn[0], norm_ple_g[0],
                 w_ple_gate[0], w_ple_proj[0], norm_final_g)
    return out.reshape(bsz, seq, d)
```

```python
import functools
import math

import jax
import jax.numpy as jnp
from jax import lax
from jax.experimental import pallas as pl
from jax.experimental.pallas import tpu as pltpu

F32 = jnp.float32
BF16 = jnp.bfloat16
I32 = jnp.int32

NORM_EPS = 1e-5
HEAD_DIM = 64
N_HEADS = 8
ATTN_WIDTH = 512
SSM_WIDTH = 512
SSM_GROUP = 16
N_SSM_GROUPS = 32
SSM_STATE = 64
N_STATE = N_SSM_GROUPS * SSM_STATE
N_EXPERTS = 32
TOP_K = 4
SWIGLU_LIMIT = 7.0
SWIGLU_ALPHA = 1.702
LANES = 128
MOE_ROWS = 1024
NEG_BIG = -1e30
LOG2E = math.log2(math.e)
VMEM_LIMIT = 56 * 1024 * 1024

_NT = (((1,), (1,)), ((), ()))


def _rms(xf, g):
    ms = jnp.mean(xf * xf, axis=-1, keepdims=True)
    return xf * lax.rsqrt(ms + NORM_EPS) * g


def _sigmoid(x):
    return 1.0 / (1.0 + jnp.exp(-x))


def _ssm_prep_kernel(lr_ref, li_ref, ldt_ref, brt_ref, bit_ref,
                     ar_ref, ai_ref, bbr_ref, bbi_ref):
    lr = lr_ref[...]
    li = li_ref[...]
    dt = jnp.exp(ldt_ref[...])
    mag = jnp.exp(lr * dt)
    ab_re = mag * jnp.cos(li * dt)
    ab_im = mag * jnp.sin(li * dt)
    den = lr * lr + li * li
    nr = ab_re - 1.0
    z_re = (nr * lr + ab_im * li) / den
    z_im = (ab_im * lr - nr * li) / den
    ar_ref[...] = ab_re
    ai_ref[...] = ab_im
    br = brt_ref[...]
    bi = bit_ref[...]
    bbr_ref[...] = z_re * br - z_im * bi
    bbi_ref[...] = z_re * bi + z_im * br


def _ssm_prep(lam_re, lam_im, log_dt, b_re, b_im):
    g, p, c = b_re.shape
    brt = jnp.transpose(b_re, (0, 2, 1))
    bit = jnp.transpose(b_im, (0, 2, 1))
    return pl.pallas_call(
        _ssm_prep_kernel,
        out_shape=(jax.ShapeDtypeStruct((g, 1, p), F32), jax.ShapeDtypeStruct((g, 1, p), F32),
                   jax.ShapeDtypeStruct((g, c, p), F32), jax.ShapeDtypeStruct((g, c, p), F32)),
        name="ssm_prep",
    )(lam_re.reshape(g, 1, p), lam_im.reshape(g, 1, p), log_dt.reshape(g, 1, 1), brt, bit)


def _block_diag(w):
    g, a, b = w.shape
    half = g // 2
    eye = jnp.eye(half, dtype=w.dtype)
    w4 = w.reshape(2, half, a, b)
    out = jnp.einsum('hgab,gk->hgakb', w4, eye)
    return out.reshape(2, half * a, half * b)


def _inproj_kernel(x_ref, g_ref, w_ref, wft_ref, bf_ref, tri_ref,
                   q_ref, k_ref, v_ref, u_ref, c_ref, carry_ref, *, tiles_per_seq, tm):
    i = pl.program_id(0)

    @pl.when(i % tiles_per_seq == 0)
    def _():
        carry_ref[...] = jnp.zeros_like(carry_ref)

    h = _rms(x_ref[...], g_ref[...]).astype(BF16)
    proj = jnp.dot(h, w_ref[...], preferred_element_type=F32)
    aw = ATTN_WIDTH
    q_ref[...] = (proj[:, 0:aw] * (LOG2E * HEAD_DIM ** -0.5)).astype(BF16)
    k_ref[...] = proj[:, aw:2 * aw].astype(BF16)
    v_ref[...] = proj[:, 2 * aw:3 * aw].astype(BF16)
    u_ref[...] = proj[:, 3 * aw:3 * aw + SSM_WIDTH].astype(BF16)

    fl = lax.dot_general(wft_ref[...], h, _NT, preferred_element_type=F32)
    z = fl + bf_ref[...]
    lf = jnp.minimum(z, 0.0) - jnp.log1p(jnp.exp(-jnp.abs(z)))
    hi = lf.astype(BF16)
    lo = (lf - hi.astype(F32)).astype(BF16)
    tri = tri_ref[...]
    cs = (jnp.dot(hi, tri, preferred_element_type=F32)
          + jnp.dot(lo, tri, preferred_element_type=F32))
    c = cs + carry_ref[:, 0:1]
    c_ref[...] = c * LOG2E
    carry_ref[...] = jnp.broadcast_to(c[:, tm - 1:tm], carry_ref.shape)


def _in_proj(x2, norm_g, w_main, wft, b_f, *, seq, tm=1024):
    t, d = x2.shape
    n_main = w_main.shape[1]
    tri = jnp.triu(jnp.ones((tm, tm), F32)).astype(BF16)
    kern = functools.partial(_inproj_kernel, tiles_per_seq=seq // tm, tm=tm)
    row = lambda i: (i, 0)
    const = lambda i: (0, 0)
    act = jax.ShapeDtypeStruct((t, ATTN_WIDTH), BF16)
    return pl.pallas_call(
        kern,
        grid=(t // tm,),
        in_specs=[pl.BlockSpec((tm, d), row),
                  pl.BlockSpec((1, d), const),
                  pl.BlockSpec((d, n_main), const),
                  pl.BlockSpec((N_HEADS, d), const),
                  pl.BlockSpec((N_HEADS, 1), const),
                  pl.BlockSpec((tm, tm), const)],
        out_specs=[pl.BlockSpec((tm, ATTN_WIDTH), row)] * 4
        + [pl.BlockSpec((N_HEADS, tm), lambda i: (0, i))],
        out_shape=[act, act, act, act, jax.ShapeDtypeStruct((N_HEADS, t), F32)],
        scratch_shapes=[pltpu.VMEM((N_HEADS, LANES), F32)],
        compiler_params=pltpu.CompilerParams(
            dimension_semantics=("arbitrary",), vmem_limit_bytes=VMEM_LIMIT),
        name="in_proj",
    )(x2, norm_g.reshape(1, d), w_main, wft, b_f.reshape(N_HEADS, 1), tri)


def _attn_kernel(q_ref, k_ref, v_ref, c_ref, o_ref, *, tq, n_pairs):
    i = pl.program_id(2)
    lane = lax.broadcasted_iota(I32, (1, LANES), 1)
    first = lane < HEAD_DIM
    n_heads = 2 * n_pairs
    q_heads = []
    for p in range(n_pairs):
        q2 = q_ref[:, p * LANES:(p + 1) * LANES]
        zero = jnp.zeros_like(q2)
        q_heads += [jnp.where(first, q2, zero), jnp.where(first, zero, q2)]
    half = tq // 2

    def block(off, width, r0, carry, masked):
        cj = c_ref[:, pl.ds(off, width)]
        out = []
        for p in range(n_pairs):
            kj = k_ref[pl.ds(off, width), p * LANES:(p + 1) * LANES]
            vj = v_ref[pl.ds(off, width), p * LANES:(p + 1) * LANES]
            one = jnp.ones_like(vj)
            v_heads = (jnp.where(first, vj, one), jnp.where(first, one, vj))
            for h in range(2):
                m, acc = carry[2 * p + h]
                s = (lax.dot_general(q_heads[2 * p + h][r0:], kj, _NT,
                                     preferred_element_type=F32)
                     - cj[2 * p + h:2 * p + h + 1, :])
                if masked:
                    rr = lax.broadcasted_iota(I32, s.shape, 0)
                    cc = lax.broadcasted_iota(I32, s.shape, 1)
                    s = jnp.where(cc <= rr, s, NEG_BIG)
                m_new = jnp.maximum(m[r0:], jnp.max(s, axis=-1, keepdims=True))
                alpha = jnp.exp2(m[r0:] - m_new)
                pr = jnp.exp2(s - m_new).astype(BF16)
                acc_new = alpha * acc[r0:] + jnp.dot(pr, v_heads[h], preferred_element_type=F32)
                if r0:
                    m_new = jnp.concatenate([m[:r0], m_new], axis=0)
                    acc_new = jnp.concatenate([acc[:r0], acc_new], axis=0)
                out.append((m_new, acc_new))
        return tuple(out)

    init_one = (jnp.full((tq, 1), NEG_BIG, F32), jnp.zeros((tq, LANES), F32))
    carry = lax.fori_loop(
        0, i, lambda j, c: block(pl.multiple_of(j * tq, tq), tq, 0, c, False),
        (init_one,) * n_heads)
    diag = pl.multiple_of(i * tq, tq)
    for n in range(tq // half):
        carry = block(pl.multiple_of(diag + n * half, half), half, n * half, carry, True)
    outs = []
    for p in range(n_pairs):
        acc_a = carry[2 * p][1]
        acc_b = carry[2 * p + 1][1]
        outs.append(jnp.where(first, acc_a / pltpu.roll(acc_a, HEAD_DIM, axis=1),
                              acc_b / pltpu.roll(acc_b, HEAD_DIM, axis=1)))
    o_ref[...] = jnp.concatenate(outs, axis=-1).astype(BF16)


def _attention(q, k, v, c_t, *, tq=1024, n_pairs=2):
    b, s, w = q.shape
    wb = n_pairs * LANES
    n_groups = w // wb
    c3 = c_t.reshape(n_groups, 2 * n_pairs, b * s)
    kern = functools.partial(_attn_kernel, tq=tq, n_pairs=n_pairs)
    return pl.pallas_call(
        kern,
        grid=(b, n_groups, s // tq),
        in_specs=[pl.BlockSpec((None, tq, wb), lambda bi, g, i: (bi, i, g)),
                  pl.BlockSpec((None, s, wb), lambda bi, g, i: (bi, 0, g)),
                  pl.BlockSpec((None, s, wb), lambda bi, g, i: (bi, 0, g)),
                  pl.BlockSpec((None, 2 * n_pairs, s), lambda bi, g, i: (g, 0, bi))],
        out_specs=pl.BlockSpec((None, tq, wb), lambda bi, g, i: (bi, i, g)),
        out_shape=jax.ShapeDtypeStruct((b, s, w), BF16),
        compiler_params=pltpu.CompilerParams(
            dimension_semantics=("arbitrary", "arbitrary", "arbitrary"),
            vmem_limit_bytes=VMEM_LIMIT),
        name="attention",
    )(q, k, v, c3)


def _ssm_kernel(u_ref, bre_ref, bim_ref, cre_ref, cim_ref, ar_ref, ai_ref, dsk_ref,
                wglu_ref, bglu_ref, g_ref, o_ref,
                us_ref, xr_ref, xi_ref, str_ref, sti_ref, res_ref, *, tt, nb):
    i = pl.program_id(0)

    @pl.when(i == 0)
    def _():
        str_ref[...] = jnp.zeros_like(str_ref)
        sti_ref[...] = jnp.zeros_like(sti_ref)

    n_chunks = SSM_WIDTH // LANES
    for b in range(nb):
        ub32 = u_ref[b].astype(F32)
        for c in range(n_chunks):
            us_ref[c, pl.ds(b, tt, stride=nb), :] = ub32[:, c * LANES:(c + 1) * LANES]
    uf = jnp.concatenate([us_ref[c] for c in range(n_chunks)], axis=-1)
    ub = uf.astype(BF16)
    half_in = SSM_WIDTH // 2
    half_st = N_STATE // 2
    for hf in range(2):
        uh = ub[:, hf * half_in:(hf + 1) * half_in]
        xr_ref[:, hf * half_st:(hf + 1) * half_st] = jnp.dot(
            uh, bre_ref[hf], preferred_element_type=F32)
        xi_ref[:, hf * half_st:(hf + 1) * half_st] = jnp.dot(
            uh, bim_ref[hf], preferred_element_type=F32)

    n_col_groups = 2
    wcol = N_STATE // n_col_groups
    unroll = 4
    for cg in range(n_col_groups):
        cols = slice(cg * wcol, (cg + 1) * wcol)
        ar = jnp.broadcast_to(ar_ref[:, cols], (nb, wcol))
        ai = jnp.broadcast_to(ai_ref[:, cols], (nb, wcol))

        def steps(tb, carry, cols=cols, ar=ar, ai=ai):
            sr, si = carry
            for k in range(unroll):
                r0 = pl.multiple_of((tb * unroll + k) * nb, nb)
                br = xr_ref[pl.ds(r0, nb), cols]
                bi = xi_ref[pl.ds(r0, nb), cols]
                nr = ar * sr - ai * si + br
                ni = ar * si + ai * sr + bi
                xr_ref[pl.ds(r0, nb), cols] = nr
                xi_ref[pl.ds(r0, nb), cols] = ni
                sr, si = nr, ni
            return sr, si

        sr, si = lax.fori_loop(0, tt // unroll, steps, (str_ref[:, cols], sti_ref[:, cols]))
        str_ref[:, cols] = sr
        sti_ref[:, cols] = si

    ys = []
    for hf in range(2):
        xr = xr_ref[:, hf * half_st:(hf + 1) * half_st].astype(BF16)
        xi = xi_ref[:, hf * half_st:(hf + 1) * half_st].astype(BF16)
        ys.append(jnp.dot(xr, cre_ref[hf], preferred_element_type=F32)
                  - jnp.dot(xi, cim_ref[hf], preferred_element_type=F32))
    y = jnp.concatenate(ys, axis=-1) + dsk_ref[...] * uf
    gl = 0.5 * y * (1.0 + jnp.tanh(math.sqrt(2.0 / math.pi) * (y + 0.044715 * (y * y * y))))
    zz = jnp.dot(gl.astype(BF16), wglu_ref[...], preferred_element_type=F32) + bglu_ref[...]
    out = gl * _sigmoid(zz)
    res = _rms(out, g_ref[...])
    for c in range(n_chunks):
        res_ref[c] = res[:, c * LANES:(c + 1) * LANES]
    for b in range(nb):
        o_ref[b] = jnp.concatenate(
            [res_ref[c, pl.ds(b, tt, stride=nb), :] for c in range(n_chunks)],
            axis=-1).astype(BF16)


def _ssm(u3, bre, bim, cre, cim, a_re, a_im, d_skip, w_glu, b_glu, out_g, *, tt=128):
    nb, s, w = u3.shape
    rows = tt * nb
    kern = functools.partial(_ssm_kernel, tt=tt, nb=nb)
    c3 = lambda i: (0, 0, 0)
    c2 = lambda i: (0, 0)
    return pl.pallas_call(
        kern,
        grid=(s // tt,),
        in_specs=[pl.BlockSpec((nb, tt, w), lambda i: (0, i, 0)),
                  pl.BlockSpec(bre.shape, c3), pl.BlockSpec(bim.shape, c3),
                  pl.BlockSpec(cre.shape, c3), pl.BlockSpec(cim.shape, c3),
                  pl.BlockSpec((1, N_STATE), c2), pl.BlockSpec((1, N_STATE), c2),
                  pl.BlockSpec((1, w), c2),
                  pl.BlockSpec((w, w), c2), pl.BlockSpec((1, w), c2), pl.BlockSpec((1, w), c2)],
        out_specs=pl.BlockSpec((nb, tt, w), lambda i: (0, i, 0)),
        out_shape=jax.ShapeDtypeStruct((nb, s, w), BF16),
        scratch_shapes=[pltpu.VMEM((w // LANES, rows, LANES), F32),
                        pltpu.VMEM((rows, N_STATE), F32), pltpu.VMEM((rows, N_STATE), F32),
                        pltpu.VMEM((nb, N_STATE), F32), pltpu.VMEM((nb, N_STATE), F32),
                        pltpu.VMEM((w // LANES, rows, LANES), F32)],
        compiler_params=pltpu.CompilerParams(
            dimension_semantics=("arbitrary",), vmem_limit_bytes=VMEM_LIMIT),
        name="ssm",
    )(u3, bre, bim, cre, cim, a_re.reshape(1, N_STATE), a_im.reshape(1, N_STATE),
      d_skip.reshape(1, w), w_glu, b_glu.reshape(1, w), out_g.reshape(1, w))


def _outproj_kernel(x_ref, a_ref, s_ref, ga_ref, woa_ref, wos_ref, gf_ref, wrt_ref, br_ref,
                    tri_ref, trie_ref, x1_ref, h2_ref, topi_ref, gate_ref, rank_ref, cnt_ref,
                    *, tm):
    a = _rms(a_ref[...].astype(F32), ga_ref[...]).astype(BF16)
    x1 = (x_ref[...] + jnp.dot(a, woa_ref[...], preferred_element_type=F32)
          + jnp.dot(s_ref[...], wos_ref[...], preferred_element_type=F32))
    x1_ref[...] = x1
    h2 = _rms(x1, gf_ref[...]).astype(BF16)
    h2_ref[...] = h2

    lg = lax.dot_general(wrt_ref[...], h2, _NT,
                         preferred_element_type=F32) + br_ref[...]
    ids = lax.broadcasted_iota(I32, (N_EXPERTS, tm), 0)
    work = lg
    vals, idxs = [], []
    for _ in range(TOP_K):
        m = jnp.max(work, axis=0, keepdims=True)
        idx = jnp.min(jnp.where(work == m, ids, N_EXPERTS), axis=0, keepdims=True)
        vals.append(m)
        idxs.append(idx)
        work = jnp.where(ids == idx, -jnp.inf, work)
    exps = [jnp.exp(v - vals[0]) for v in vals]
    den = exps[0] + exps[1] + exps[2] + exps[3]
    gate_ref[...] = jnp.concatenate([e / den for e in exps], axis=0)
    topi_ref[...] = jnp.concatenate(idxs, axis=0)

    sel = jnp.zeros((N_EXPERTS, tm), F32)
    for idx in idxs:
        sel = sel + jnp.where(ids == idx, 1.0, 0.0)
    cnt = jnp.broadcast_to(jnp.sum(sel, axis=1, keepdims=True), (N_EXPERTS, LANES))
    padded = jnp.floor((cnt + (SEG_ALIGN - 1)) * (1.0 / SEG_ALIGN)) * SEG_ALIGN
    seg_start = jnp.dot(trie_ref[...], padded.astype(BF16), preferred_element_type=F32)
    before = (jnp.dot(sel.astype(BF16), tri_ref[...], preferred_element_type=F32)
              + seg_start[:, 0:1])
    ranks = [jnp.sum(jnp.where(ids == idx, before, 0.0), axis=0, keepdims=True) for idx in idxs]
    rank_ref[...] = jnp.concatenate(ranks, axis=0).astype(I32)
    cnt_ref[0] = cnt


def _out_proj(x2, attn, ssm, attn_g, wo_a, wo_s, ffn_g, wrt, b_router, *, tm=512):
    t, d = x2.shape
    tri = jnp.triu(jnp.ones((tm, tm), F32), k=1).astype(BF16)
    tri_e = jnp.tril(jnp.ones((N_EXPERTS, N_EXPERTS), F32), k=-1).astype(BF16)
    kern = functools.partial(_outproj_kernel, tm=tm)
    row = lambda i: (i, 0)
    const = lambda i: (0, 0)
    colblk = lambda i: (0, i)
    return pl.pallas_call(
        kern,
        grid=(t // tm,),
        in_specs=[pl.BlockSpec((tm, d), row),
                  pl.BlockSpec((tm, ATTN_WIDTH), row),
                  pl.BlockSpec((tm, SSM_WIDTH), row),
                  pl.BlockSpec((1, ATTN_WIDTH), const),
                  pl.BlockSpec((ATTN_WIDTH, d), const),
                  pl.BlockSpec((SSM_WIDTH, d), const),
                  pl.BlockSpec((1, d), const),
                  pl.BlockSpec((N_EXPERTS, d), const),
                  pl.BlockSpec((N_EXPERTS, 1), const),
                  pl.BlockSpec((tm, tm), const),
                  pl.BlockSpec((N_EXPERTS, N_EXPERTS), const)],
        out_specs=[pl.BlockSpec((tm, d), row),
                   pl.BlockSpec((tm, d), row),
                   pl.BlockSpec((TOP_K, tm), colblk),
                   pl.BlockSpec((TOP_K, tm), colblk),
                   pl.BlockSpec((TOP_K, tm), colblk),
                   pl.BlockSpec((1, N_EXPERTS, LANES), lambda i: (i, 0, 0))],
        out_shape=[jax.ShapeDtypeStruct((t, d), F32),
                   jax.ShapeDtypeStruct((t, d), BF16),
                   jax.ShapeDtypeStruct((TOP_K, t), I32),
                   jax.ShapeDtypeStruct((TOP_K, t), F32),
                   jax.ShapeDtypeStruct((TOP_K, t), I32),
                   jax.ShapeDtypeStruct((t // tm, N_EXPERTS, LANES), F32)],
        compiler_params=pltpu.CompilerParams(
            dimension_semantics=("arbitrary",), vmem_limit_bytes=VMEM_LIMIT),
        name="out_proj",
    )(x2, attn, ssm, attn_g.reshape(1, -1), wo_a, wo_s, ffn_g.reshape(1, d), wrt,
      b_router.reshape(N_EXPERTS, 1), tri, tri_e)


SEG_ALIGN = 16
MOE_TILE = 512
LOCAL_ROWS = TOP_K * MOE_TILE + N_EXPERTS * SEG_ALIGN
CHUNK_TABLE = 256
TAIL_TABLE = N_EXPERTS * (MOE_ROWS // SEG_ALIGN)
START_GROUP = 4
MIN_CHUNKS = TOP_K * MOE_TILE // SEG_ALIGN
WAIT_GROUP = 8


def _chunk_copy(src, dst, sem):
    return pltpu.make_async_copy(src, dst, sem)


def _chunk_wait_n(local_buf, hbm, sem, slot, n, to_hbm):
    def wait_rows(rows):
        loc = local_buf.at[slot, pl.ds(0, rows), :]
        far = hbm.at[pl.ds(0, rows), :]

        def body(j, carry):
            (_chunk_copy(loc, far, sem.at[slot]) if to_hbm
             else _chunk_copy(far, loc, sem.at[slot])).wait()
            return carry
        return body

    n_groups = n // WAIT_GROUP
    lax.fori_loop(0, n_groups, wait_rows(WAIT_GROUP * SEG_ALIGN), 0)
    lax.fori_loop(n_groups * WAIT_GROUP, n, wait_rows(SEG_ALIGN), 0)


def _chunk_start(local_buf, hbm, tab_ref, sem, slot, j, to_hbm, priority):
    loc = local_buf.at[slot, pl.ds(pl.multiple_of(j * SEG_ALIGN, SEG_ALIGN), SEG_ALIGN), :]
    far = hbm.at[pl.ds(pl.multiple_of(tab_ref[0, 0, j] * SEG_ALIGN, SEG_ALIGN), SEG_ALIGN), :]
    (_chunk_copy(loc, far, sem.at[slot]) if to_hbm
     else _chunk_copy(far, loc, sem.at[slot])).start(priority=priority)


def _chunk_start_n(local_buf, hbm, tab_ref, sem, slot, n, to_hbm, first=0):
    def group(g, carry):
        for u in range(START_GROUP):
            _chunk_start(local_buf, hbm, tab_ref, sem, slot, g * START_GROUP + u, to_hbm, u % 2)
        return carry

    def single(j, carry):
        _chunk_start(local_buf, hbm, tab_ref, sem, slot, j, to_hbm, 0)
        return carry

    n_groups = n // START_GROUP
    lax.fori_loop(first // START_GROUP, n_groups, group, 0)
    lax.fori_loop(n_groups * START_GROUP, n, single, 0)


def _dispatch_kernel(nch_ref, nused_ref, ntail_ref, tab_ref, tail_ref, lp_ref, h2_ref, xs_hbm,
                     buf, zbuf, sem, zsem, *, n_tiles, n_blocks):
    i = pl.program_id(0)
    slot = i % 2

    @pl.when(i >= 2)
    def _():
        _chunk_wait_n(buf, xs_hbm, sem, slot, nch_ref[jnp.maximum(i - 2, 0)], True)

    lp = lp_ref[...]
    tm = lp.shape[1]

    rows = lax.broadcasted_iota(I32, (LOCAL_ROWS, tm), 0)
    onehot = jnp.zeros((LOCAL_ROWS, tm), F32)
    for k in range(TOP_K):
        onehot = jnp.where(rows == lp[k:k + 1, :], 1.0, onehot)
    buf[slot] = jnp.dot(onehot.astype(BF16), h2_ref[...],
                        preferred_element_type=F32).astype(BF16)
    _chunk_start_n(buf, xs_hbm, tab_ref, sem, slot, nch_ref[i], True)

    @pl.when(i == n_tiles - 1)
    def _():
        zbuf[...] = jnp.zeros_like(zbuf)
        zrow = zbuf.at[pl.ds(0, SEG_ALIGN), :]

        def fill(j, carry):
            row0 = pl.multiple_of(tail_ref[0, 0, j] * SEG_ALIGN, SEG_ALIGN)
            _chunk_copy(zrow, xs_hbm.at[pl.ds(row0, SEG_ALIGN), :], zsem).start()
            return carry

        def drain(j, carry):
            _chunk_copy(zrow, xs_hbm.at[pl.ds(0, SEG_ALIGN), :], zsem).wait()
            return carry

        def fill_block(b, carry):
            row0 = pl.multiple_of(b * MOE_ROWS, MOE_ROWS)
            _chunk_copy(zbuf, xs_hbm.at[pl.ds(row0, MOE_ROWS), :], zsem).start()
            return carry

        def drain_block(b, carry):
            _chunk_copy(zbuf, xs_hbm.at[pl.ds(0, MOE_ROWS), :], zsem).wait()
            return carry

        lax.fori_loop(0, ntail_ref[0], fill, 0)
        lax.fori_loop(0, ntail_ref[0], drain, 0)
        lax.fori_loop(nused_ref[0], n_blocks, fill_block, 0)
        lax.fori_loop(nused_ref[0], n_blocks, drain_block, 0)
        _chunk_wait_n(buf, xs_hbm, sem, slot, nch_ref[i], True)
        if n_tiles > 1:
            _chunk_wait_n(buf, xs_hbm, sem, 1 - slot, nch_ref[jnp.maximum(i - 1, 0)], True)


def _dispatch(n_chunks, n_used, n_tail, chunk_tab, tail_tab, lp, h2, n_blocks):
    k, t = lp.shape
    d = h2.shape[1]
    n_tiles = t // MOE_TILE
    grid_spec = pltpu.PrefetchScalarGridSpec(
        num_scalar_prefetch=3,
        grid=(n_tiles,),
        in_specs=[pl.BlockSpec((1, 1, CHUNK_TABLE), lambda i, nc, nu, nt: (i, 0, 0),
                               memory_space=pltpu.SMEM),
                  pl.BlockSpec((1, 1, TAIL_TABLE), lambda i, nc, nu, nt: (0, 0, 0),
                               memory_space=pltpu.SMEM),
                  pl.BlockSpec((k, MOE_TILE), lambda i, nc, nu, nt: (0, i)),
                  pl.BlockSpec((MOE_TILE, d), lambda i, nc, nu, nt: (i, 0))],
        out_specs=pl.BlockSpec(memory_space=pl.ANY),
        scratch_shapes=[pltpu.VMEM((2, LOCAL_ROWS, d), BF16), pltpu.VMEM((MOE_ROWS, d), BF16),
                        pltpu.SemaphoreType.DMA((2,)), pltpu.SemaphoreType.DMA],
    )
    return pl.pallas_call(
        functools.partial(_dispatch_kernel, n_tiles=n_tiles, n_blocks=n_blocks),
        grid_spec=grid_spec,
        out_shape=jax.ShapeDtypeStruct((n_blocks * MOE_ROWS, d), BF16),
        compiler_params=pltpu.CompilerParams(
            dimension_semantics=("arbitrary",), vmem_limit_bytes=VMEM_LIMIT),
        name="dispatch",
    )(n_chunks, n_used, n_tail, chunk_tab, tail_tab, lp, h2)


def _expert_mlp_kernel(blk_e_ref, nused_ref, x_ref, wgu_ref, bgu_ref, wdn_ref, bdn_ref, y_ref,
                       wgu_bf, wdn_bf):
    i = pl.program_id(0)
    nused = nused_ref[0]
    new_expert = jnp.logical_or(i == 0, blk_e_ref[i] != blk_e_ref[jnp.maximum(i - 1, 0)])

    @pl.when(jnp.logical_and(new_expert, i < nused))
    def _():
        wgu_bf[...] = wgu_ref[...].astype(BF16)
        wdn_bf[...] = wdn_ref[...].astype(BF16)

    @pl.when(i < nused)
    def _():
        x = x_ref[...]
        d_ff = wdn_bf.shape[0]
        n_col = 4
        wc = d_ff // n_col
        acts = []
        for c in range(n_col):
            g = (jnp.dot(x, wgu_bf[:, c * wc:(c + 1) * wc], preferred_element_type=F32)
                 + bgu_ref[:, c * wc:(c + 1) * wc])
            u = (jnp.dot(x, wgu_bf[:, d_ff + c * wc:d_ff + (c + 1) * wc],
                         preferred_element_type=F32)
                 + bgu_ref[:, d_ff + c * wc:d_ff + (c + 1) * wc])
            gate = jnp.minimum(g, SWIGLU_LIMIT)
            up = jnp.clip(u, -SWIGLU_LIMIT, SWIGLU_LIMIT)
            acts.append((gate * _sigmoid(SWIGLU_ALPHA * gate) * (up + 1.0)).astype(BF16))
        act = jnp.concatenate(acts, axis=-1)
        y = jnp.dot(act, wdn_bf[...], preferred_element_type=F32) + bdn_ref[...]
        y_ref[...] = y.astype(BF16)

    @pl.when(i >= nused)
    def _():
        y_ref[...] = jnp.zeros_like(y_ref)


def _expert_mlp(blk_expert, n_used, xs, w_gu, b_gu, w_dn, b_dn):
    d = xs.shape[1]
    n_blocks = blk_expert.shape[0]
    n_rows = n_blocks * MOE_ROWS
    e, _, f2 = w_gu.shape
    grid_spec = pltpu.PrefetchScalarGridSpec(
        num_scalar_prefetch=2,
        grid=(n_blocks,),
        in_specs=[
            pl.BlockSpec((MOE_ROWS, d), lambda i, be, nu: (jnp.minimum(i, nu[0] - 1), 0)),
            pl.BlockSpec((None, d, f2), lambda i, be, nu: (be[i], 0, 0)),
            pl.BlockSpec((None, 1, f2), lambda i, be, nu: (be[i], 0, 0)),
            pl.BlockSpec((None, f2 // 2, d), lambda i, be, nu: (be[i], 0, 0)),
            pl.BlockSpec((None, 1, d), lambda i, be, nu: (be[i], 0, 0)),
        ],
        out_specs=pl.BlockSpec((MOE_ROWS, d), lambda i, be, nu: (i, 0)),
        scratch_shapes=[pltpu.VMEM((d, f2), BF16), pltpu.VMEM((f2 // 2, d), BF16)],
    )
    return pl.pallas_call(
        _expert_mlp_kernel,
        grid_spec=grid_spec,
        out_shape=jax.ShapeDtypeStruct((n_rows, d), BF16),
        compiler_params=pltpu.CompilerParams(
            dimension_semantics=("arbitrary",), vmem_limit_bytes=VMEM_LIMIT),
        name="experts",
    )(blk_expert, n_used, xs, w_gu, b_gu.reshape(e, 1, f2), w_dn, b_dn.reshape(e, 1, d))


def _moe_combine_kernel(nch_ref, tab_cur_ref, tab_nxt_ref, y_hbm, lpt_ref, gate_ref, x1_ref,
                        p_ref, gp_ref, wg_ref, wp_ref, gfin_ref, o_ref, buf, sem, *, n_tiles):
    i = pl.program_id(0)
    slot = i % 2

    @pl.when(i == 0)
    def _():
        buf[...] = jnp.zeros_like(buf)
        _chunk_start_n(buf, y_hbm, tab_cur_ref, sem, 0, nch_ref[0], False)

    @pl.when(i + 1 < n_tiles)
    def _():
        _chunk_start_n(buf, y_hbm, tab_nxt_ref, sem, 1 - slot,
                       nch_ref[jnp.minimum(i + 1, n_tiles - 1)], False, first=MIN_CHUNKS)

    _chunk_wait_n(buf, y_hbm, sem, slot, nch_ref[i], False)
    lpt = lpt_ref[...]
    gates = gate_ref[...]
    tm = lpt.shape[0]
    cols = lax.broadcasted_iota(I32, (tm, LOCAL_ROWS), 1)
    weights = jnp.zeros((tm, LOCAL_ROWS), F32)
    for k in range(TOP_K):
        weights = jnp.where(cols == lpt[:, k:k + 1], gates[:, k:k + 1], weights)
    weights = weights.astype(BF16)
    n_col = 4
    wc = buf.shape[2] // n_col
    per = MIN_CHUNKS // n_col
    moe = []
    for c in range(n_col):
        for j in range(c * per, (c + 1) * per):
            _chunk_start(buf, y_hbm, tab_nxt_ref, sem, 1 - slot, j, False, j % 2)
        moe.append(jnp.dot(weights, buf[slot, :, c * wc:(c + 1) * wc],
                           preferred_element_type=F32))
    x2 = x1_ref[...] + jnp.concatenate(moe, axis=-1)
    hg = _rms(x2, gp_ref[...]).astype(BF16)
    gate = _sigmoid(jnp.dot(hg, wg_ref[...], preferred_element_type=F32))
    emb = jnp.dot(p_ref[...].astype(BF16), wp_ref[...], preferred_element_type=F32)
    x3 = x2 + gate * emb
    o_ref[...] = _rms(x3, gfin_ref[...])

    @pl.when(i == n_tiles - 1)
    def _():
        _chunk_wait_n(buf, y_hbm, sem, 1 - slot, MIN_CHUNKS, False)


def _moe_combine(n_chunks, chunk_tab, y, lpt, gates_tk, x1, p2, ple_g, w_gate, w_proj, fin_g):
    t, d = x1.shape
    n_tiles = t // MOE_TILE
    last = n_tiles - 1
    ple = p2.shape[1]
    row = lambda i, nc: (i, 0)
    const = lambda i, nc: (0, 0)
    grid_spec = pltpu.PrefetchScalarGridSpec(
        num_scalar_prefetch=1,
        grid=(n_tiles,),
        in_specs=[pl.BlockSpec((1, 1, CHUNK_TABLE), lambda i, nc: (i, 0, 0),
                               memory_space=pltpu.SMEM),
                  pl.BlockSpec((1, 1, CHUNK_TABLE),
                               lambda i, nc: (jnp.minimum(i + 1, last), 0, 0),
                               memory_space=pltpu.SMEM),
                  pl.BlockSpec(memory_space=pl.ANY),
                  pl.BlockSpec((MOE_TILE, TOP_K), row),
                  pl.BlockSpec((MOE_TILE, TOP_K), row),
                  pl.BlockSpec((MOE_TILE, d), row),
                  pl.BlockSpec((MOE_TILE, ple), row),
                  pl.BlockSpec((1, d), const),
                  pl.BlockSpec((d, d), const),
                  pl.BlockSpec((ple, d), const),
                  pl.BlockSpec((1, d), const)],
        out_specs=pl.BlockSpec((MOE_TILE, d), row),
        scratch_shapes=[pltpu.VMEM((2, LOCAL_ROWS, d), BF16), pltpu.SemaphoreType.DMA((2,))],
    )
    return pl.pallas_call(
        functools.partial(_moe_combine_kernel, n_tiles=n_tiles),
        grid_spec=grid_spec,
        out_shape=jax.ShapeDtypeStruct((t, d), F32),
        compiler_params=pltpu.CompilerParams(
            dimension_semantics=("arbitrary",), vmem_limit_bytes=VMEM_LIMIT),
        name="combine",
    )(n_chunks, chunk_tab, chunk_tab, y, lpt, gates_tk, x1, p2, ple_g.reshape(1, d),
      w_gate, w_proj, fin_g.reshape(1, d))


def _moe_tables(cnt, n_assign):
    n_tiles = cnt.shape[0]
    cnt = cnt.astype(I32)
    pc = (cnt + SEG_ALIGN - 1) // SEG_ALIGN * SEG_ALIGN
    l_end = jnp.cumsum(pc, axis=1)
    l_start = l_end - pc
    tot = jnp.sum(pc, axis=0)
    e_pad = (tot + MOE_ROWS - 1) // MOE_ROWS * MOE_ROWS
    e_end = jnp.cumsum(e_pad)
    seg_row0 = (e_end - e_pad)[None, :] + jnp.cumsum(pc, axis=0) - pc
    j16 = jnp.arange(CHUNK_TABLE, dtype=I32)[None, :] * SEG_ALIGN
    ce = jnp.minimum(jnp.sum((l_end[:, :, None] <= j16[:, None, :]).astype(I32), axis=1),
                     N_EXPERTS - 1)
    pick = ce[:, None, :] == jnp.arange(N_EXPERTS, dtype=I32)[None, :, None]
    seg0 = jnp.sum(jnp.where(pick, seg_row0[:, :, None], 0), axis=1)
    loc0 = jnp.sum(jnp.where(pick, l_start[:, :, None], 0), axis=1)
    n_chunks = l_end[:, -1] // SEG_ALIGN
    valid = jnp.arange(CHUNK_TABLE, dtype=I32)[None, :] < n_chunks[:, None]
    tab = jnp.where(valid, (seg0 + j16 - loc0) // SEG_ALIGN, 0).astype(I32)
    n_rows_max = n_assign + n_tiles * N_EXPERTS * (SEG_ALIGN - 1) + N_EXPERTS * (MOE_ROWS - 1)
    n_blocks = (n_rows_max + MOE_ROWS - 1) // MOE_ROWS
    n_used = (e_end[-1] // MOE_ROWS).astype(I32).reshape(1)
    blk_row0 = jnp.arange(n_blocks, dtype=I32) * MOE_ROWS
    blk_expert = jnp.minimum(
        jnp.sum((e_end[None, :] <= blk_row0[:, None]).astype(I32), axis=1), N_EXPERTS - 1)
    t_len = (e_pad - tot) // SEG_ALIGN
    t_end = jnp.cumsum(t_len)
    t_first = (e_end - e_pad + tot) // SEG_ALIGN
    f = jnp.arange(TAIL_TABLE, dtype=I32)
    te = jnp.minimum(jnp.sum((t_end[None, :] <= f[:, None]).astype(I32), axis=1),
                     N_EXPERTS - 1)
    te_pick = te[:, None] == jnp.arange(N_EXPERTS, dtype=I32)[None, :]
    tail = (jnp.sum(jnp.where(te_pick, (t_first - t_end + t_len)[None, :], 0), axis=1) + f)
    tail = jnp.where(f < t_end[-1], tail, 0).astype(I32)
    n_tail = t_end[-1].astype(I32).reshape(1)
    return (n_chunks.astype(I32), tab.reshape(n_tiles, 1, CHUNK_TABLE),
            tail.reshape(1, 1, TAIL_TABLE), n_tail, blk_expert, n_used)


def _layer(x2, p2, seq, norm_mix_g, w_in, b_f, lam_re, lam_im, log_dt, b_re, b_im, c_re, c_im,
           d_skip, w_glu, b_glu, attn_out_g, ssm_out_g, w_out, norm_ffn_g, w_router, b_router,
           w_gu, b_gu, w_dn, b_dn, norm_ple_g, w_ple_gate, w_ple_proj, final_g):
    t, d = x2.shape
    nb = t // seq
    aw = ATTN_WIDTH
    w_main = jnp.concatenate([w_in[:, :3 * aw], w_in[:, 3 * aw + N_HEADS:]], axis=1).astype(BF16)
    wft = w_in[:, 3 * aw:3 * aw + N_HEADS].T.astype(BF16)

    a_re, a_im, bbr, bbi = _ssm_prep(lam_re, lam_im, log_dt, b_re, b_im)
    bre = _block_diag(bbr).astype(BF16)
    bim = _block_diag(bbi).astype(BF16)
    cre = _block_diag(jnp.transpose(c_re, (0, 2, 1))).astype(BF16)
    cim = _block_diag(jnp.transpose(c_im, (0, 2, 1))).astype(BF16)

    q, k, v, u, c_t = _in_proj(x2, norm_mix_g, w_main, wft, b_f, seq=seq)
    shp = (nb, seq, aw)
    attn = _attention(q.reshape(shp), k.reshape(shp), v.reshape(shp), c_t)
    ssm = _ssm(u.reshape(nb, seq, SSM_WIDTH), bre, bim, cre, cim, a_re, a_im,
               d_skip, w_glu.astype(BF16), b_glu, ssm_out_g)

    x1, h2, _, gates, lp, cnt = _out_proj(
        x2, attn.reshape(t, aw), ssm.reshape(t, SSM_WIDTH), attn_out_g,
        w_out[:aw].astype(BF16), w_out[aw:].astype(BF16), norm_ffn_g,
        w_router.T.astype(BF16), b_router, tm=MOE_TILE)

    n_chunks, chunk_tab, tail_tab, n_tail, blk_expert, n_used = _moe_tables(
        cnt[:, :, 0], TOP_K * t)
    xs = _dispatch(n_chunks, n_used, n_tail, chunk_tab, tail_tab, lp, h2, blk_expert.shape[0])
    y = _expert_mlp(blk_expert, n_used, xs, w_gu, b_gu, w_dn, b_dn)
    return _moe_combine(n_chunks, chunk_tab, y, lp.T, gates.T, x1, p2, norm_ple_g,
                        w_ple_gate.astype(BF16), w_ple_proj.astype(BF16), final_g)


def kernel(x, p, norm_mix_g, w_in, b_f, lam_re, lam_im, log_dt, b_re, b_im, c_re, c_im, d_skip, w_glu, b_glu, attn_out_g, ssm_out_g, w_out, norm_ffn_g, w_router, b_router, w_gu, b_gu, w_dn, b_dn, norm_ple_g, w_ple_gate, w_ple_proj, norm_final_g):
    bsz, seq, d = x.shape
    depth = w_in.shape[0]
    assert depth == 1, "one layer: the final rmsnorm is fused into the layer's last kernel"
    out = _layer(x.reshape(bsz * seq, d), p[0].reshape(bsz * seq, -1), seq,
                 norm_mix_g[0], w_in[0], b_f[0], lam_re[0], lam_im[0], log_dt[0],
                 b_re[0], b_im[0], c_re[0], c_im[0], d_skip[0], w_glu[0], b_glu[0],
                 attn_out_g[0], ssm_out_g[0], w_out[0], norm_ffn_g[0], w_router[0],
                 b_router[0], w_gu[0], b_gu[0], w_dn[0], b_dn[0], norm_ple_g[0],
                 w_ple_gate[0], w_ple_proj[0], norm_final_g)
    return out.reshape(bsz, seq, d)
```
